```python
import math
import jax, jax.numpy as jnp
from jax import lax
import numpy as np

D_MODEL = 1024
BATCH = 8
SEQ = 2048
DEPTH = 1
DEC_BATCH = 128
DEC_SEQ = 8
PAST_LEN = 16384
PAGE_SIZE = 128

MIX_WIDTH = D_MODEL
POOL_WIDTH = MIX_WIDTH // 2
SSM_WIDTH = MIX_WIDTH - POOL_WIDTH
POOL_WINDOWS = (2, 4, 8, 16)
N_POOL_GROUPS = len(POOL_WINDOWS)
POOL_GROUP = POOL_WIDTH // N_POOL_GROUPS
POOL_BUF = max(POOL_WINDOWS) - 1
SSM_GROUP = 16
N_SSM_GROUPS = SSM_WIDTH // SSM_GROUP
SSM_STATE = 64
DT_MIN = 1e-3
DT_MAX = 1e-1
N_EXPERT_GROUPS = 4
EXPERTS_PER_GROUP = 4
N_EXPERTS = N_EXPERT_GROUPS * EXPERTS_PER_GROUP
EXPERT_TOP_K = 2
EXPERT_HIDDEN = 256
N_MOD = 6
EPS = 1e-6

kernel_name = 'hybrid_pool_s5_hmoe_adaln_step'


def rmsnorm(x, g):
    xf = x.astype(jnp.float32)
    y = xf * lax.rsqrt(jnp.mean(xf * xf, axis=-1, keepdims=True) + EPS)
    return (y * g.astype(jnp.float32)).astype(x.dtype)


def modulate(h, shift, scale):
    return h * (1 + scale[:, None, :]) + shift[:, None, :]


def pool_mixer(u, buf, start_pos, pool_w, pool_scale):
    n, L, _ = u.shape
    uf = u.astype(jnp.float32)
    z = jnp.concatenate([buf.astype(jnp.float32), uf], axis=1)
    cs = jnp.concatenate([jnp.zeros((n, 1, POOL_WIDTH), jnp.float32), jnp.cumsum(z, axis=1)], axis=1)
    end = cs[:, POOL_BUF + 1:]
    pos = start_pos + jnp.arange(L)
    outs = []
    for k, w in enumerate(POOL_WINDOWS):
        sl = slice(k * POOL_GROUP, (k + 1) * POOL_GROUP)
        start = cs[:, POOL_BUF + 1 - w:POOL_BUF + 1 - w + L, sl]
        cnt = jnp.minimum(w, pos + 1).astype(jnp.float32)[None, :, None]
        outs.append((end[..., sl] - start) / cnt - uf[..., sl])
    pooled = jnp.stack(outs, axis=2)
    mixed = jnp.einsum('nlgc,gcd->nlgd', pooled, pool_w.astype(jnp.float32)).reshape(n, L, POOL_WIDTH)
    return mixed * pool_scale.astype(jnp.float32), z[:, -POOL_BUF:]


def complex_affine_combine(e1, e2):
    a1r, a1i, b1r, b1i = e1
    a2r, a2i, b2r, b2i = e2
    return (a2r * a1r - a2i * a1i,
            a2r * a1i + a2i * a1r,
            a2r * b1r - a2i * b1i + b2r,
            a2r * b1i + a2i * b1r + b2i)


def ssm_mixer(u, h0_re, h0_im, a_re, a_im, log_dt, b_re, b_im, c_re, c_im, d, glu_w, glu_b):
    f32 = jnp.float32
    n, L, _ = u.shape
    uf = u.astype(f32).reshape(n, L, N_SSM_GROUPS, SSM_GROUP)
    a_re = a_re.astype(f32)
    a_im = a_im.astype(f32)
    dt = jnp.exp(log_dt.astype(f32))[:, None]
    mag = jnp.exp(a_re * dt)
    lb_re = mag * jnp.cos(a_im * dt)
    lb_im = mag * jnp.sin(a_im * dt)
    den = a_re * a_re + a_im * a_im
    nr = lb_re - 1.0
    ni = lb_im
    k_re = (nr * a_re + ni * a_im) / den
    k_im = (ni * a_re - nr * a_im) / den
    b_re = b_re.astype(f32)
    b_im = b_im.astype(f32)
    bb_re = k_re[..., None] * b_re - k_im[..., None] * b_im
    bb_im = k_re[..., None] * b_im + k_im[..., None] * b_re
    bu_re = jnp.einsum('nlgh,gph->nlgp', uf, bb_re)
    bu_im = jnp.einsum('nlgh,gph->nlgp', uf, bb_im)
    h0_re = h0_re.astype(f32)
    h0_im = h0_im.astype(f32)
    bu_re = bu_re.at[:, 0].add(lb_re * h0_re - lb_im * h0_im)
    bu_im = bu_im.at[:, 0].add(lb_re * h0_im + lb_im * h0_re)
    ar = jnp.broadcast_to(lb_re, bu_re.shape)
    ai = jnp.broadcast_to(lb_im, bu_im.shape)
    _, _, h_re, h_im = lax.associative_scan(complex_affine_combine, (ar, ai, bu_re, bu_im), axis=1)
    y = (jnp.einsum('nlgp,ghp->nlgh', h_re, c_re.astype(f32))
         - jnp.einsum('nlgp,ghp->nlgh', h_im, c_im.astype(f32))
         + d.astype(f32) * uf).reshape(n, L, SSM_WIDTH)
    g = jax.nn.gelu(y)
    out = g * jax.nn.sigmoid(g @ glu_w.astype(f32) + glu_b.astype(f32))
    return out, h_re[:, -1], h_im[:, -1]


def hier_moe(h, rw1, rb1, rw2, rb2, w_gate, w_up, w_down):
    n, L, D = h.shape
    t = h.reshape(n * L, D)
    lg1 = (t @ rw1 + rb1).astype(jnp.float32)
    p1 = jax.nn.softmax(lg1, axis=-1)
    p_top, g_idx = lax.top_k(p1, 1)
    lg2 = (jnp.einsum('td,gde->tge', t, rw2) + rb2).astype(jnp.float32)
    lg2_sel = jnp.einsum('tge,tg->te', lg2, jax.nn.one_hot(g_idx[:, 0], N_EXPERT_GROUPS, dtype=jnp.float32))
    v2, e_idx = lax.top_k(lg2_sel, EXPERT_TOP_K)
    w2 = jax.nn.softmax(v2, axis=-1) * p_top
    flat = g_idx * EXPERTS_PER_GROUP + e_idx
    gates = jnp.sum(jax.nn.one_hot(flat, N_EXPERTS, dtype=jnp.float32) * w2[..., None], axis=1)
    a = jnp.einsum('td,edf->tef', t, w_gate)
    b = jnp.einsum('td,edf->tef', t, w_up)
    hid = (jax.nn.silu(a) * b * gates[:, :, None].astype(a.dtype)).astype(t.dtype)
    out = jnp.einsum('tef,efd->td', hid, w_down)
    return out.reshape(n, L, D)


def run_layer(x, c, pool_buf, h0_re, h0_im, start_pos, prm):
    mod = jax.nn.silu(c) @ prm['w_ada'] + prm['b_ada']
    sh1, sc1, g1, sh2, sc2, g2 = jnp.split(mod, N_MOD, axis=-1)
    h = modulate(rmsnorm(x, prm['norm1_g']), sh1, sc1)
    u = h @ prm['w_in']
    pool_out, new_buf = pool_mixer(u[..., :POOL_WIDTH], pool_buf, start_pos, prm['pool_w'], prm['pool_scale'])
    ssm_out, nh_re, nh_im = ssm_mixer(u[..., POOL_WIDTH:], h0_re, h0_im, prm['ssm_a_re'], prm['ssm_a_im'],
                                      prm['ssm_log_dt'], prm['ssm_b_re'], prm['ssm_b_im'], prm['ssm_c_re'],
                                      prm['ssm_c_im'], prm['ssm_d'], prm['glu_w'], prm['glu_b'])
    mix = jnp.concatenate([pool_out, ssm_out], axis=-1).astype(x.dtype) @ prm['w_out']
    x = x + g1[:, None, :] * mix
    h2 = modulate(rmsnorm(x, prm['norm2_g']), sh2, sc2)
    x = x + g2[:, None, :] * hier_moe(h2, prm['router_w1'], prm['router_b1'], prm['router_w2'], prm['router_b2'],
                                      prm['exp_w_gate'], prm['exp_w_up'], prm['exp_w_down'])
    return x, new_buf, nh_re, nh_im


def setup_inputs(seed: int = 0) -> dict:
    key = jax.random.key(seed)
    ks = iter(jax.random.split(key, 40))
    f32 = jnp.float32

    def nrm(shape, std):
        return jax.random.normal(next(ks), shape, f32) * std

    inp = {}
    inp['x_prompt'] = nrm((BATCH, SEQ, D_MODEL), 1.0)
    inp['x_sample'] = nrm((DEC_BATCH, DEC_SEQ, D_MODEL), 1.0)
    inp['c_prompt'] = nrm((BATCH, D_MODEL), 1.0)
    inp['c_sample'] = nrm((DEC_BATCH, D_MODEL), 1.0)
    inp['state_pool'] = nrm((DEPTH, DEC_BATCH, POOL_BUF, POOL_WIDTH), 1.0)
    inp['state_ssm_re'] = nrm((DEPTH, DEC_BATCH, N_SSM_GROUPS, SSM_STATE), 0.1)
    inp['state_ssm_im'] = nrm((DEPTH, DEC_BATCH, N_SSM_GROUPS, SSM_STATE), 0.1)
    inp['w_ada'] = nrm((DEPTH, D_MODEL, N_MOD * D_MODEL), 0.5 * D_MODEL ** -0.5)
    inp['b_ada'] = nrm((DEPTH, N_MOD * D_MODEL), 0.02)
    inp['norm1_g'] = 1.0 + nrm((DEPTH, D_MODEL), 0.05)
    inp['w_in'] = nrm((DEPTH, D_MODEL, MIX_WIDTH), D_MODEL ** -0.5)
    inp['pool_w'] = nrm((DEPTH, N_POOL_GROUPS, POOL_GROUP, POOL_GROUP), POOL_GROUP ** -0.5)
    inp['pool_scale'] = 1.0 + nrm((DEPTH, POOL_WIDTH), 0.1)
    inp['ssm_a_re'] = -0.5 + nrm((DEPTH, N_SSM_GROUPS, SSM_STATE), 0.01)
    inp['ssm_a_im'] = math.pi * jnp.arange(SSM_STATE, dtype=f32) + nrm((DEPTH, N_SSM_GROUPS, SSM_STATE), 0.01)
    inp['ssm_log_dt'] = jax.random.uniform(next(ks), (DEPTH, N_SSM_GROUPS), f32, math.log(DT_MIN), math.log(DT_MAX))
    inp['ssm_b_re'] = nrm((DEPTH, N_SSM_GROUPS, SSM_STATE, SSM_GROUP), (2 * SSM_GROUP) ** -0.5)
    inp['ssm_b_im'] = nrm((DEPTH, N_SSM_GROUPS, SSM_STATE, SSM_GROUP), (2 * SSM_GROUP) ** -0.5)
    inp['ssm_c_re'] = nrm((DEPTH, N_SSM_GROUPS, SSM_GROUP, SSM_STATE), 2.0 * SSM_STATE ** -0.5)
    inp['ssm_c_im'] = nrm((DEPTH, N_SSM_GROUPS, SSM_GROUP, SSM_STATE), 2.0 * SSM_STATE ** -0.5)
    inp['ssm_d'] = nrm((DEPTH, N_SSM_GROUPS, SSM_GROUP), 1.0)
    inp['glu_w'] = nrm((DEPTH, SSM_WIDTH, SSM_WIDTH), SSM_WIDTH ** -0.5)
    inp['glu_b'] = nrm((DEPTH, SSM_WIDTH), 0.02)
    inp['w_out'] = nrm((DEPTH, MIX_WIDTH, D_MODEL), MIX_WIDTH ** -0.5)
    inp['norm2_g'] = 1.0 + nrm((DEPTH, D_MODEL), 0.05)
    inp['router_w1'] = nrm((DEPTH, D_MODEL, N_EXPERT_GROUPS), D_MODEL ** -0.5)
    inp['router_b1'] = nrm((DEPTH, N_EXPERT_GROUPS), 0.01)
    inp['router_w2'] = nrm((DEPTH, N_EXPERT_GROUPS, D_MODEL, EXPERTS_PER_GROUP), D_MODEL ** -0.5)
    inp['router_b2'] = nrm((DEPTH, N_EXPERT_GROUPS, EXPERTS_PER_GROUP), 0.01)
    inp['exp_w_gate'] = nrm((DEPTH, N_EXPERTS, D_MODEL, EXPERT_HIDDEN), D_MODEL ** -0.5)
    inp['exp_w_up'] = nrm((DEPTH, N_EXPERTS, D_MODEL, EXPERT_HIDDEN), D_MODEL ** -0.5)
    inp['exp_w_down'] = nrm((DEPTH, N_EXPERTS, EXPERT_HIDDEN, D_MODEL), EXPERT_HIDDEN ** -0.5)
    inp['final_norm_g'] = 1.0 + nrm((D_MODEL,), 0.05)
    return inp


def reference(x_prompt, x_sample, c_prompt, c_sample, state_pool, state_ssm_re, state_ssm_im,
              w_ada, b_ada, norm1_g, w_in, pool_w, pool_scale, ssm_a_re, ssm_a_im, ssm_log_dt,
              ssm_b_re, ssm_b_im, ssm_c_re, ssm_c_im, ssm_d, glu_w, glu_b, w_out, norm2_g,
              router_w1, router_b1, router_w2, router_b2, exp_w_gate, exp_w_up, exp_w_down, final_norm_g):
    nb = x_prompt.shape[0]
    xp = x_prompt
    xs = x_sample
    pool_p, pool_s, re_p, im_p, re_s, im_s = [], [], [], [], [], []
    for l in range(DEPTH):
        prm = dict(w_ada=w_ada[l], b_ada=b_ada[l], norm1_g=norm1_g[l], w_in=w_in[l], pool_w=pool_w[l],
                   pool_scale=pool_scale[l], ssm_a_re=ssm_a_re[l], ssm_a_im=ssm_a_im[l], ssm_log_dt=ssm_log_dt[l],
                   ssm_b_re=ssm_b_re[l], ssm_b_im=ssm_b_im[l], ssm_c_re=ssm_c_re[l], ssm_c_im=ssm_c_im[l],
                   ssm_d=ssm_d[l], glu_w=glu_w[l], glu_b=glu_b[l], w_out=w_out[l], norm2_g=norm2_g[l],
                   router_w1=router_w1[l], router_b1=router_b1[l], router_w2=router_w2[l], router_b2=router_b2[l],
                   exp_w_gate=exp_w_gate[l], exp_w_up=exp_w_up[l], exp_w_down=exp_w_down[l])
        zbuf = jnp.zeros((nb, POOL_BUF, POOL_WIDTH), x_prompt.dtype)
        zh = jnp.zeros((nb, N_SSM_GROUPS, SSM_STATE), jnp.float32)
        xp, bp, hrp, hip = run_layer(xp, c_prompt, zbuf, zh, zh, 0, prm)
        xs, bs, hrs, his = run_layer(xs, c_sample, state_pool[l], state_ssm_re[l], state_ssm_im[l], PAST_LEN, prm)
        pool_p.append(bp.astype(x_prompt.dtype))
        pool_s.append(bs.astype(state_pool.dtype))
        re_p.append(hrp.astype(state_ssm_re.dtype))
        im_p.append(hip.astype(state_ssm_im.dtype))
        re_s.append(hrs.astype(state_ssm_re.dtype))
        im_s.append(his.astype(state_ssm_im.dtype))
    y_prompt = rmsnorm(xp, final_norm_g)
    y_sample = rmsnorm(xs, final_norm_g)
    new_pool_prompt = jnp.stack(pool_p, axis=0)
    new_pool_sample = jnp.stack(pool_s, axis=0)
    new_ssm_re_prompt = jnp.stack(re_p, axis=0)
    new_ssm_im_prompt = jnp.stack(im_p, axis=0)
    new_ssm_re_sample = jnp.stack(re_s, axis=0)
    new_ssm_im_sample = jnp.stack(im_s, axis=0)
    return (y_prompt, y_sample, new_pool_prompt, new_pool_sample, new_ssm_re_prompt, new_ssm_im_prompt, new_ssm_re_sample, new_ssm_im_sample)
```

```python
import functools

import jax
import jax.numpy as jnp
from jax import lax
from jax.experimental import pallas as pl
from jax.experimental.pallas import tpu as pltpu

D_MODEL = 1024
POOL_WIDTH = 512
SSM_WIDTH = 512
POOL_WINDOWS = (2, 4, 8, 16)
POOL_GROUP = 128
POOL_BUF = 15
HIST = 16
SSM_GROUP = 16
N_SSM_GROUPS = 32
SSM_STATE = 64
N_STATE = N_SSM_GROUPS * SSM_STATE
N_EXPERT_GROUPS = 4
EXPERTS_PER_GROUP = 4
N_EXPERTS = 16
EXPERT_HIDDEN = 256
N_MOD = 6
EPS = 1e-6
PAST_LEN = 16384

MXU_TILE = 256
LANES = 128
SUBLANES = 8
ROUTER_LANES = 128
VMEM_LIMIT = 56 * 1024 * 1024

F32 = jnp.float32
BF16 = jnp.bfloat16


def _cparams(sem):
    return pltpu.CompilerParams(dimension_semantics=sem, vmem_limit_bytes=VMEM_LIMIT)


def _const_spec(shape):
    nd = len(shape)
    return pl.BlockSpec(shape, lambda *_: (0,) * nd, pipeline_mode=pl.Buffered(1))


def _rms(x, g):
    return x * lax.rsqrt(jnp.mean(x * x, axis=-1, keepdims=True) + EPS) * g


def _ssm_prep_body(are_ref, aim_ref, ldt_ref, bre_ref, bim_ref, lbre_ref, lbim_ref, bbre_ref, bbim_ref):
    a_re = are_ref[...]
    a_im = aim_ref[...]
    dt = jnp.exp(ldt_ref[...])
    mag = jnp.exp(a_re * dt)
    lb_re = mag * jnp.cos(a_im * dt)
    lb_im = mag * jnp.sin(a_im * dt)
    den = a_re * a_re + a_im * a_im
    nr = lb_re - 1.0
    ni = lb_im
    k_re = (nr * a_re + ni * a_im) / den
    k_im = (ni * a_re - nr * a_im) / den
    lbre_ref[...] = lb_re
    lbim_ref[...] = lb_im
    b_re = bre_ref[...]
    b_im = bim_ref[...]
    bbre_ref[...] = k_re[:, None, :] * b_re - k_im[:, None, :] * b_im
    bbim_ref[...] = k_re[:, None, :] * b_im + k_im[:, None, :] * b_re


def _ssm_prep(a_re, a_im, log_dt, b_re_t, b_im_t):
    G, P, H = N_SSM_GROUPS, SSM_STATE, SSM_GROUP
    return pl.pallas_call(
        _ssm_prep_body,
        out_shape=(jax.ShapeDtypeStruct((G, P), F32), jax.ShapeDtypeStruct((G, P), F32),
                   jax.ShapeDtypeStruct((G, H, P), F32), jax.ShapeDtypeStruct((G, H, P), F32)),
        name="ssm_prep",
    )(a_re, a_im, log_dt.reshape(G, 1), b_re_t, b_im_t)


def _adaln_body(c_ref, w_ref, b_ref, o_ref):
    s = jax.nn.silu(c_ref[...])
    o_ref[...] = jnp.dot(s.astype(BF16), w_ref[...].astype(BF16), preferred_element_type=F32) + b_ref[...]


def _adaln(c_all, w_ada, b_ada):
    n = c_all.shape[0]
    tn = 512
    return pl.pallas_call(
        _adaln_body,
        grid=(N_MOD * D_MODEL // tn,),
        in_specs=[pl.BlockSpec((n, D_MODEL), lambda j: (0, 0)),
                  pl.BlockSpec((D_MODEL, tn), lambda j: (0, j)),
                  pl.BlockSpec((1, tn), lambda j: (0, j))],
        out_specs=pl.BlockSpec((n, tn), lambda j: (0, j)),
        out_shape=jax.ShapeDtypeStruct((n, N_MOD * D_MODEL), F32),
        compiler_params=_cparams(("arbitrary",)),
        name="adaln",
    )(c_all, w_ada, b_ada.reshape(1, -1))


def _stage1_body(x_ref, sh_ref, sc_ref, g_ref, w_ref, o_ref, *, lead):
    x = x_ref[0] if lead else x_ref[...]
    sh = sh_ref[0] if lead else sh_ref[...]
    sc = sc_ref[0] if lead else sc_ref[...]
    h = _rms(x, g_ref[...]) * (1.0 + sc) + sh
    o_ref[...] = jnp.dot(h.astype(BF16), w_ref[...], preferred_element_type=F32)


def _stage1_prompt(x, mod3, g1, w_in_bf, tc):
    nb, L, D = x.shape
    return pl.pallas_call(
        functools.partial(_stage1_body, lead=True),
        grid=(nb, L // tc),
        in_specs=[pl.BlockSpec((1, tc, D), lambda n, c: (n, c, 0)),
                  pl.BlockSpec((1, 1, D), lambda n, c: (n, 0, 0)),
                  pl.BlockSpec((1, 1, D), lambda n, c: (n, 0, 1)),
                  _const_spec((1, D)),
                  _const_spec((D, D))],
        out_specs=pl.BlockSpec((tc, D), lambda n, c: (c, n)),
        out_shape=jax.ShapeDtypeStruct((L, nb * D), F32),
        compiler_params=_cparams(("arbitrary", "arbitrary")),
        name="stage1_prompt",
    )(x, mod3, mod3, g1, w_in_bf)


def _stage1_sample(x2d, mod2, g1, w_in_bf):
    ns, LD = x2d.shape
    D = D_MODEL
    L = LD // D
    return pl.pallas_call(
        functools.partial(_stage1_body, lead=False),
        grid=(L,),
        in_specs=[pl.BlockSpec((ns, D), lambda l: (0, l)),
                  pl.BlockSpec((ns, D), lambda l: (0, 0)),
                  pl.BlockSpec((ns, D), lambda l: (0, 1)),
                  _const_spec((1, D)),
                  _const_spec((D, D))],
        out_specs=pl.BlockSpec((ns, D), lambda l: (l, 0)),
        out_shape=jax.ShapeDtypeStruct((L * ns, D), F32),
        compiler_params=_cparams(("arbitrary",)),
        name="stage1_sample",
    )(x2d, mod2, mod2, g1, w_in_bf)


def _stage2_body(u_ref, buf0_ref, hre0_ref, him0_ref, poolw_ref, pscale_ref, lbre_ref, lbim_ref,
                 wbre_ref, wbim_ref, wcre_ref, wcim_ref, d_ref, gluw_ref, glub_ref,
                 mix_ref, newbuf_ref, hre_out_ref, him_out_ref,
                 z_scr, sre_scr, sim_scr, hre_scr, him_scr, *, S, Tt, start_pos):
    i = pl.program_id(0)
    R = Tt * S
    HR = HIST * S

    @pl.when(i == 0)
    def _init():
        z_scr[0:S, :] = jnp.zeros((S, POOL_WIDTH), F32)
        for j in range(POOL_BUF):
            z_scr[(j + 1) * S:(j + 2) * S, :] = buf0_ref[:, j * POOL_WIDTH:(j + 1) * POOL_WIDTH]
        hre_scr[...] = hre0_ref[...]
        him_scr[...] = him0_ref[...]

    up = u_ref[:, 0:POOL_WIDTH]
    us = u_ref[:, POOL_WIDTH:D_MODEL]
    z_scr[HR:HR + R, :] = up

    row = lax.broadcasted_iota(jnp.int32, (R, 1), 0)
    pos = start_pos + i * Tt + lax.shift_right_logical(row, S.bit_length() - 1)
    for k, w in enumerate(POOL_WINDOWS):
        lo, hi = k * POOL_GROUP, (k + 1) * POOL_GROUP
        cur = z_scr[:, lo:hi]
        step = 1
        while step < w:
            cur = cur[step * S:, :] + cur[:cur.shape[0] - step * S, :]
            step *= 2
        s = cur[cur.shape[0] - R:, :]
        cnt = jnp.minimum(w, pos + 1).astype(F32)
        pooled = s / cnt - up[:, lo:hi]
        mixed = jnp.dot(pooled.astype(BF16), poolw_ref[k], preferred_element_type=F32)
        mix_ref[:, lo:hi] = (mixed * pscale_ref[:, lo:hi]).astype(mix_ref.dtype)

    for j in range(POOL_BUF):
        r0 = (Tt + 1 + j) * S
        newbuf_ref[:, j * POOL_WIDTH:(j + 1) * POOL_WIDTH] = z_scr[r0:r0 + S, :]
    hist = z_scr[R:R + HR, :]
    z_scr[0:HR, :] = hist

    usb = us.astype(BF16)
    n_bt = N_STATE // MXU_TILE
    for n in range(n_bt):
        kb = (n * MXU_TILE // SSM_STATE * SSM_GROUP) // MXU_TILE
        lhs = usb[:, kb * MXU_TILE:(kb + 1) * MXU_TILE]
        sre_scr[:, n * MXU_TILE:(n + 1) * MXU_TILE] = jnp.dot(lhs, wbre_ref[n], preferred_element_type=F32)
        sim_scr[:, n * MXU_TILE:(n + 1) * MXU_TILE] = jnp.dot(lhs, wbim_ref[n], preferred_element_type=F32)

    CB = 512
    n_tiles = S // SUBLANES
    for cb in range(N_STATE // CB):
        c0 = cb * CB
        lr = jnp.broadcast_to(lbre_ref[:, c0:c0 + CB], (SUBLANES, CB))
        li = jnp.broadcast_to(lbim_ref[:, c0:c0 + CB], (SUBLANES, CB))

        def tile_body(j, carry, c0=c0, lr=lr, li=li):
            s0 = pl.multiple_of(j * SUBLANES, SUBLANES)
            hr = hre_scr[pl.ds(s0, SUBLANES), c0:c0 + CB]
            hi_ = him_scr[pl.ds(s0, SUBLANES), c0:c0 + CB]

            def step_body(t, hc):
                hr, hi_ = hc
                r0 = pl.multiple_of(t * S + s0, SUBLANES)
                br = sre_scr[pl.ds(r0, SUBLANES), c0:c0 + CB]
                bi = sim_scr[pl.ds(r0, SUBLANES), c0:c0 + CB]
                nr = lr * hr - li * hi_ + br
                ni = lr * hi_ + li * hr + bi
                sre_scr[pl.ds(r0, SUBLANES), c0:c0 + CB] = nr
                sim_scr[pl.ds(r0, SUBLANES), c0:c0 + CB] = ni
                return nr, ni

            hr, hi_ = lax.fori_loop(0, Tt, step_body, (hr, hi_), unroll=8)
            hre_scr[pl.ds(s0, SUBLANES), c0:c0 + CB] = hr
            him_scr[pl.ds(s0, SUBLANES), c0:c0 + CB] = hi_
            return carry

        lax.fori_loop(0, n_tiles, tile_body, 0)

    hre_out_ref[...] = hre_scr[...]
    him_out_ref[...] = him_scr[...]

    n_ct = SSM_WIDTH // MXU_TILE
    k_per = (N_STATE // n_ct) // MXU_TILE
    ys = []
    for m in range(n_ct):
        acc = d_ref[:, m * MXU_TILE:(m + 1) * MXU_TILE] * us[:, m * MXU_TILE:(m + 1) * MXU_TILE]
        for k in range(k_per):
            kt = m * k_per + k
            acc = acc + jnp.dot(sre_scr[:, kt * MXU_TILE:(kt + 1) * MXU_TILE].astype(BF16), wcre_ref[kt],
                                preferred_element_type=F32)
            acc = acc + jnp.dot(sim_scr[:, kt * MXU_TILE:(kt + 1) * MXU_TILE].astype(BF16), wcim_ref[kt],
                                preferred_element_type=F32)
        ys.append(acc)
    y = jnp.concatenate(ys, axis=-1)
    g = jax.nn.gelu(y)
    gate = jax.nn.sigmoid(jnp.dot(g.astype(BF16), gluw_ref[...], preferred_element_type=F32) + glub_ref[...])
    mix_ref[:, POOL_WIDTH:D_MODEL] = (g * gate).astype(mix_ref.dtype)


def _stage2(u_tm, buf0, hre0, him0, wts, *, S, Tt, start_pos):
    rows = u_tm.shape[0]
    L = rows // S
    R = Tt * S
    (poolw, pscale, lbre, lbim, wbre, wbim, wcre, wcim, dflat, gluw, glub) = wts
    body = functools.partial(_stage2_body, S=S, Tt=Tt, start_pos=start_pos)
    const_in = [buf0, hre0, him0, poolw, pscale, lbre, lbim, wbre, wbim, wcre, wcim, dflat, gluw, glub]
    return pl.pallas_call(
        body,
        grid=(L // Tt,),
        in_specs=[pl.BlockSpec((R, D_MODEL), lambda i: (i, 0))] + [_const_spec(a.shape) for a in const_in],
        out_specs=(pl.BlockSpec((R, D_MODEL), lambda i: (i, 0)),
                   pl.BlockSpec((S, POOL_BUF * POOL_WIDTH), lambda i: (0, 0)),
                   pl.BlockSpec((S, N_STATE), lambda i: (0, 0)),
                   pl.BlockSpec((S, N_STATE), lambda i: (0, 0))),
        out_shape=(jax.ShapeDtypeStruct((rows, D_MODEL), BF16),
                   jax.ShapeDtypeStruct((S, POOL_BUF * POOL_WIDTH), F32),
                   jax.ShapeDtypeStruct((S, N_STATE), F32),
                   jax.ShapeDtypeStruct((S, N_STATE), F32)),
        scratch_shapes=[pltpu.VMEM(((HIST + Tt) * S, POOL_WIDTH), F32),
                        pltpu.VMEM((R, N_STATE), F32),
                        pltpu.VMEM((R, N_STATE), F32),
                        pltpu.VMEM((S, N_STATE), F32),
                        pltpu.VMEM((S, N_STATE), F32)],
        compiler_params=_cparams(("arbitrary",)),
        name="stage2_S%d" % S,
    )(u_tm, *const_in)


def _split_bf16(v):
    hi = v.astype(BF16)
    lo = (v - hi.astype(F32)).astype(BF16)
    return hi, lo


def _stage3_body(x_ref, mix_ref, g1_ref, sh2_ref, sc2_ref, g2_ref, n2g_ref, fng_ref, wout_ref, rw_ref, rb_ref,
                 wg_ref, wu_ref, wd_ref, y_ref, *, lead):
    x = x_ref[0] if lead else x_ref[...]
    g1, sh2, sc2, g2 = [(r[0] if lead else r[...]) for r in (g1_ref, sh2_ref, sc2_ref, g2_ref)]
    R = x.shape[0]
    mix = jnp.dot(mix_ref[...], wout_ref[...], preferred_element_type=F32)
    x1 = x + g1 * mix
    h2 = _rms(x1, n2g_ref[...]) * (1.0 + sc2) + sh2
    h2_hi, h2_lo = _split_bf16(h2)

    rw_hi, rw_lo = _split_bf16(rw_ref[...])
    logits = (jnp.dot(h2_hi, rw_hi, preferred_element_type=F32)
              + jnp.dot(h2_lo, rw_hi, preferred_element_type=F32)
              + jnp.dot(h2_hi, rw_lo, preferred_element_type=F32)) + rb_ref[...]
    lane = lax.broadcasted_iota(jnp.int32, (R, ROUTER_LANES), 1).astype(F32)
    ninf = jnp.float32(-jnp.inf)
    none = jnp.float32(ROUTER_LANES)
    is_g = lane < N_EXPERT_GROUPS
    l1 = jnp.where(is_g, logits, ninf)
    m1 = jnp.max(l1, axis=-1, keepdims=True)
    gidx = jnp.min(jnp.where(l1 == m1, lane, none), axis=-1, keepdims=True)
    p_top = 1.0 / jnp.sum(jnp.where(is_g, jnp.exp(logits - m1), 0.0), axis=-1, keepdims=True)
    e_lo = N_EXPERT_GROUPS + gidx * EXPERTS_PER_GROUP
    sel = (lane >= e_lo) & (lane < e_lo + EXPERTS_PER_GROUP)
    l2 = jnp.where(sel, logits, ninf)
    va = jnp.max(l2, axis=-1, keepdims=True)
    ia = jnp.min(jnp.where(l2 == va, lane, none), axis=-1, keepdims=True)
    l2b = jnp.where(lane == ia, ninf, l2)
    vb = jnp.max(l2b, axis=-1, keepdims=True)
    ib = jnp.min(jnp.where(l2b == vb, lane, none), axis=-1, keepdims=True)
    eb = jnp.exp(vb - va)
    den = 1.0 + eb
    gates = jnp.where(lane == ia, (1.0 / den) * p_top, 0.0) + jnp.where(lane == ib, (eb / den) * p_top, 0.0)

    acc = jnp.zeros((R, D_MODEL), F32)
    for e in range(N_EXPERTS):
        a = jnp.dot(h2_hi, wg_ref[e], preferred_element_type=F32)
        b = jnp.dot(h2_hi, wu_ref[e], preferred_element_type=F32)
        ge = gates[:, N_EXPERT_GROUPS + e:N_EXPERT_GROUPS + e + 1]
        hid = jax.nn.silu(a) * b * ge
        acc = acc + jnp.dot(hid.astype(BF16), wd_ref[e], preferred_element_type=F32)
    x2 = x1 + g2 * acc
    y = _rms(x2, fng_ref[...])
    if lead:
        y_ref[0] = y
    else:
        y_ref[...] = y


def _stage3_prompt(x, mix_tm2d, mod3, wts, tc):
    nb, L, D = x.shape
    consts = list(wts)

    def mspec(k):
        return pl.BlockSpec((1, 1, D), lambda n, c, k=k: (n, 0, k))

    return pl.pallas_call(
        functools.partial(_stage3_body, lead=True),
        grid=(nb, L // tc),
        in_specs=[pl.BlockSpec((1, tc, D), lambda n, c: (n, c, 0)),
                  pl.BlockSpec((tc, D), lambda n, c: (c, n)),
                  mspec(2), mspec(3), mspec(4), mspec(5)] + [_const_spec(a.shape) for a in consts],
        out_specs=pl.BlockSpec((1, tc, D), lambda n, c: (n, c, 0)),
        out_shape=jax.ShapeDtypeStruct((nb, L, D), F32),
        compiler_params=_cparams(("arbitrary", "arbitrary")),
        name="stage3_prompt",
    )(x, mix_tm2d, mod3, mod3, mod3, mod3, *consts)


def _stage3_sample(x2d, mix_tm, mod2, wts):
    ns, LD = x2d.shape
    D = D_MODEL
    L = LD // D
    consts = list(wts)

    def mspec(k):
        return pl.BlockSpec((ns, D), lambda l, k=k: (0, k))

    return pl.pallas_call(
        functools.partial(_stage3_body, lead=False),
        grid=(L,),
        in_specs=[pl.BlockSpec((ns, D), lambda l: (0, l)),
                  pl.BlockSpec((ns, D), lambda l: (l, 0)),
                  mspec(2), mspec(3), mspec(4), mspec(5)] + [_const_spec(a.shape) for a in consts],
        out_specs=pl.BlockSpec((ns, D), lambda l: (0, l)),
        out_shape=jax.ShapeDtypeStruct((ns, LD), F32),
        compiler_params=_cparams(("arbitrary",)),
        name="stage3_sample",
    )(x2d, mix_tm, mod2, mod2, mod2, mod2, *consts)


def _block_diag_tiles_b(bb_t):
    G, H, P = bb_t.shape
    eye = jnp.eye(G, dtype=bb_t.dtype)
    full = (bb_t[:, :, None, :] * eye[:, None, :, None]).reshape(G * H, G * P)
    n_t = (G * P) // MXU_TILE
    fr = full.reshape((G * H) // MXU_TILE, MXU_TILE, n_t, MXU_TILE)
    per_kb = n_t // ((G * H) // MXU_TILE)
    return jnp.stack([fr[n // per_kb, :, n, :] for n in range(n_t)], axis=0).astype(BF16)


def _block_diag_tiles_c(c):
    G, H, P = c.shape
    eye = jnp.eye(G, dtype=c.dtype)
    full = (jnp.transpose(c, (0, 2, 1))[:, :, None, :] * eye[:, None, :, None]).reshape(G * P, G * H)
    n_k = (G * P) // MXU_TILE
    n_m = (G * H) // MXU_TILE
    fr = full.reshape(n_k, MXU_TILE, n_m, MXU_TILE)
    per_m = n_k // n_m
    return jnp.stack([fr[kt, :, kt // per_m, :] for kt in range(n_k)], axis=0).astype(BF16)


def kernel(x_prompt, x_sample, c_prompt, c_sample, state_pool, state_ssm_re, state_ssm_im, w_ada, b_ada, norm1_g, w_in, pool_w, pool_scale, ssm_a_re, ssm_a_im, ssm_log_dt, ssm_b_re, ssm_b_im, ssm_c_re, ssm_c_im, ssm_d, glu_w, glu_b, w_out, norm2_g, router_w1, router_b1, router_w2, router_b2, exp_w_gate, exp_w_up, exp_w_down, final_norm_g):
    depth = w_ada.shape[0]
    assert depth == 1
    l = 0
    nb, L, D = x_prompt.shape
    ns, Ls, _ = x_sample.shape

    lb_re, lb_im, bb_re_t, bb_im_t = _ssm_prep(ssm_a_re[l], ssm_a_im[l], ssm_log_dt[l],
                                               jnp.transpose(ssm_b_re[l], (0, 2, 1)),
                                               jnp.transpose(ssm_b_im[l], (0, 2, 1)))
    wbre = _block_diag_tiles_b(bb_re_t)
    wbim = _block_diag_tiles_b(bb_im_t)
    wcre = _block_diag_tiles_c(ssm_c_re[l])
    wcim = _block_diag_tiles_c(-ssm_c_im[l])
    s2_wts = (pool_w[l].astype(BF16), pool_scale[l].reshape(1, -1), lb_re.reshape(1, -1), lb_im.reshape(1, -1),
              wbre, wbim, wcre, wcim, ssm_d[l].reshape(1, -1), glu_w[l].astype(BF16), glu_b[l].reshape(1, -1))

    rw = jnp.concatenate([router_w1[l], jnp.transpose(router_w2[l], (1, 0, 2)).reshape(D, N_EXPERTS)], axis=1)
    rw = jnp.pad(rw, ((0, 0), (0, ROUTER_LANES - rw.shape[1])))
    rb = jnp.concatenate([router_b1[l], router_b2[l].reshape(-1)])
    rb = jnp.pad(rb, (0, ROUTER_LANES - rb.shape[0])).reshape(1, -1)
    s3_wts = (norm2_g[l].reshape(1, -1), final_norm_g.reshape(1, -1), w_out[l].astype(BF16), rw, rb,
              exp_w_gate[l].astype(BF16), exp_w_up[l].astype(BF16), exp_w_down[l].astype(BF16))

    mod = _adaln(jnp.concatenate([c_prompt, c_sample], axis=0), w_ada[l], b_ada[l])
    mod_p = mod[:nb].reshape(nb, 1, N_MOD * D)
    mod_s = mod[nb:]

    g1 = norm1_g[l].reshape(1, -1)
    w_in_bf = w_in[l].astype(BF16)

    tc = 512
    u_p = _stage1_prompt(x_prompt, mod_p, g1, w_in_bf, tc).reshape(L * nb, D)
    zeros_buf = jnp.zeros((nb, POOL_BUF * POOL_WIDTH), F32)
    zeros_h = jnp.zeros((nb, N_STATE), F32)
    mix_p, pool_p, hre_p, him_p = _stage2(u_p, zeros_buf, zeros_h, zeros_h, s2_wts, S=nb, Tt=64, start_pos=0)
    y_p = _stage3_prompt(x_prompt, mix_p.reshape(L, nb * D), mod_p, s3_wts, tc)

    xs2d = x_sample.reshape(ns, Ls * D)
    u_s = _stage1_sample(xs2d, mod_s, g1, w_in_bf)
    mix_s, pool_s, hre_s, him_s = _stage2(u_s, state_pool[l].reshape(ns, POOL_BUF * POOL_WIDTH),
                                          state_ssm_re[l].reshape(ns, N_STATE), state_ssm_im[l].reshape(ns, N_STATE),
                                          s2_wts, S=ns, Tt=Ls, start_pos=PAST_LEN)
    y_s = _stage3_sample(xs2d, mix_s, mod_s, s3_wts).reshape(ns, Ls, D)

    def st(a, n):
        return a.reshape(1, n, N_SSM_GROUPS, SSM_STATE)

    return (y_p, y_s,
            pool_p.reshape(1, nb, POOL_BUF, POOL_WIDTH), pool_s.reshape(1, ns, POOL_BUF, POOL_WIDTH),
            st(hre_p, nb), st(him_p, nb), st(hre_s, ns), st(him_s, ns))
```

```python
import functools

import jax
import jax.numpy as jnp
from jax import lax
from jax.experimental import pallas as pl
from jax.experimental.pallas import tpu as pltpu

D_MODEL = 1024
POOL_WIDTH = 512
SSM_WIDTH = 512
POOL_WINDOWS = (2, 4, 8, 16)
POOL_GROUP = 128
POOL_BUF = 15
HIST = 16
SSM_GROUP = 16
N_SSM_GROUPS = 32
SSM_STATE = 64
N_STATE = N_SSM_GROUPS * SSM_STATE
N_EXPERT_GROUPS = 4
EXPERTS_PER_GROUP = 4
N_EXPERTS = 16
EXPERT_HIDDEN = 256
N_MOD = 6
EPS = 1e-6
PAST_LEN = 16384

MXU_TILE = 256
LANES = 128
SUBLANES = 8
ROUTER_LANES = 128
VMEM_LIMIT = 56 * 1024 * 1024

F32 = jnp.float32
BF16 = jnp.bfloat16


def _cparams(sem):
    return pltpu.CompilerParams(dimension_semantics=sem, vmem_limit_bytes=VMEM_LIMIT)


def _const_spec(shape):
    nd = len(shape)
    return pl.BlockSpec(shape, lambda *_: (0,) * nd, pipeline_mode=pl.Buffered(1))


def _rms(x, g):
    return x * lax.rsqrt(jnp.mean(x * x, axis=-1, keepdims=True) + EPS) * g


def _ssm_prep_body(are_ref, aim_ref, ldt_ref, bre_ref, bim_ref, lbre_ref, lbim_ref, bbre_ref, bbim_ref):
    a_re = are_ref[...]
    a_im = aim_ref[...]
    dt = jnp.exp(ldt_ref[...])
    mag = jnp.exp(a_re * dt)
    lb_re = mag * jnp.cos(a_im * dt)
    lb_im = mag * jnp.sin(a_im * dt)
    den = a_re * a_re + a_im * a_im
    nr = lb_re - 1.0
    ni = lb_im
    k_re = (nr * a_re + ni * a_im) / den
    k_im = (ni * a_re - nr * a_im) / den
    lbre_ref[...] = lb_re
    lbim_ref[...] = lb_im
    b_re = bre_ref[...]
    b_im = bim_ref[...]
    bbre_ref[...] = k_re[:, None, :] * b_re - k_im[:, None, :] * b_im
    bbim_ref[...] = k_re[:, None, :] * b_im + k_im[:, None, :] * b_re


def _ssm_prep(a_re, a_im, log_dt, b_re_t, b_im_t):
    G, P, H = N_SSM_GROUPS, SSM_STATE, SSM_GROUP
    return pl.pallas_call(
        _ssm_prep_body,
        out_shape=(jax.ShapeDtypeStruct((G, P), F32), jax.ShapeDtypeStruct((G, P), F32),
                   jax.ShapeDtypeStruct((G, H, P), F32), jax.ShapeDtypeStruct((G, H, P), F32)),
        name="ssm_prep",
    )(a_re, a_im, log_dt.reshape(G, 1), b_re_t, b_im_t)


def _adaln_body(c_ref, w_ref, b_ref, o_ref):
    s = jax.nn.silu(c_ref[...])
    o_ref[...] = jnp.dot(s.astype(BF16), w_ref[...].astype(BF16), preferred_element_type=F32) + b_ref[...]


def _adaln(c_all, w_ada, b_ada):
    n = c_all.shape[0]
    tn = 512
    return pl.pallas_call(
        _adaln_body,
        grid=(N_MOD * D_MODEL // tn,),
        in_specs=[pl.BlockSpec((n, D_MODEL), lambda j: (0, 0)),
                  pl.BlockSpec((D_MODEL, tn), lambda j: (0, j)),
                  pl.BlockSpec((1, tn), lambda j: (0, j))],
        out_specs=pl.BlockSpec((n, tn), lambda j: (0, j)),
        out_shape=jax.ShapeDtypeStruct((n, N_MOD * D_MODEL), F32),
        compiler_params=_cparams(("arbitrary",)),
        name="adaln",
    )(c_all, w_ada, b_ada.reshape(1, -1))


def _mixer_body(x_ref, sh_ref, sc_ref, g_ref, win_ref, buf0_ref, hre0_ref, him0_ref, poolw_ref, pscale_ref,
                lbre_ref, lbim_ref, wbre_ref, wbim_ref, wcre_ref, wcim_ref, d_ref, gluw_ref, glub_ref,
                mix_ref, newbuf_ref, hre_out_ref, him_out_ref,
                z_scr, sre_scr, sim_scr, hre_scr, him_scr, *, S, Tt, start_pos, seq_major):
    i = pl.program_id(0)
    R = Tt * S
    HR = HIST * S
    D = D_MODEL

    @pl.when(i == 0)
    def _init():
        z_scr[0:S, :] = jnp.zeros((S, POOL_WIDTH), F32)
        for j in range(POOL_BUF):
            z_scr[(j + 1) * S:(j + 2) * S, :] = buf0_ref[:, j * POOL_WIDTH:(j + 1) * POOL_WIDTH]
        hre_scr[...] = hre0_ref[...]
        him_scr[...] = him0_ref[...]

    g = g_ref[...]
    if seq_major:
        x3 = x_ref[...]
        h3 = _rms(x3, g) * (1.0 + sc_ref[...]) + sh_ref[...]
        u_nm = jnp.dot(h3.reshape(R, D).astype(BF16), win_ref[...], preferred_element_type=F32)
        u = jnp.swapaxes(u_nm.reshape(S, Tt, D), 0, 1).reshape(R, D)
    else:
        x_tm = jnp.concatenate([x_ref[:, t * D:(t + 1) * D] for t in range(Tt)], axis=0)
        h3 = _rms(x_tm, g).reshape(Tt, S, D) * (1.0 + sc_ref[...]) + sh_ref[...]
        u = jnp.dot(h3.reshape(R, D).astype(BF16), win_ref[...], preferred_element_type=F32)

    up = u[:, 0:POOL_WIDTH]
    us = u[:, POOL_WIDTH:D]
    z_scr[HR:HR + R, :] = up

    row = lax.broadcasted_iota(jnp.int32, (R, 1), 0)
    pos = start_pos + i * Tt + lax.shift_right_logical(row, S.bit_length() - 1)
    outs = []
    for k, w in enumerate(POOL_WINDOWS):
        lo, hi = k * POOL_GROUP, (k + 1) * POOL_GROUP
        cur = z_scr[:, lo:hi]
        step = 1
        while step < w:
            cur = cur[step * S:, :] + cur[:cur.shape[0] - step * S, :]
            step *= 2
        s = cur[cur.shape[0] - R:, :]
        cnt = jnp.minimum(w, pos + 1).astype(F32)
        pooled = s / cnt - up[:, lo:hi]
        mixed = jnp.dot(pooled.astype(BF16), poolw_ref[k], preferred_element_type=F32)
        outs.append(mixed * pscale_ref[:, lo:hi])

    for j in range(POOL_BUF):
        r0 = (Tt + 1 + j) * S
        newbuf_ref[:, j * POOL_WIDTH:(j + 1) * POOL_WIDTH] = z_scr[r0:r0 + S, :]
    hist = z_scr[R:R + HR, :]
    z_scr[0:HR, :] = hist

    usb = us.astype(BF16)
    n_bt = N_STATE // MXU_TILE
    for n in range(n_bt):
        kb = (n * MXU_TILE // SSM_STATE * SSM_GROUP) // MXU_TILE
        lhs = usb[:, kb * MXU_TILE:(kb + 1) * MXU_TILE]
        sre_scr[:, n * MXU_TILE:(n + 1) * MXU_TILE] = jnp.dot(lhs, wbre_ref[n], preferred_element_type=F32)
        sim_scr[:, n * MXU_TILE:(n + 1) * MXU_TILE] = jnp.dot(lhs, wbim_ref[n], preferred_element_type=F32)

    CB = 512
    n_tiles = S // SUBLANES
    for cb in range(N_STATE // CB):
        c0 = cb * CB
        lr = jnp.broadcast_to(lbre_ref[:, c0:c0 + CB], (SUBLANES, CB))
        li = jnp.broadcast_to(lbim_ref[:, c0:c0 + CB], (SUBLANES, CB))

        def tile_body(j, carry, c0=c0, lr=lr, li=li):
            s0 = pl.multiple_of(j * SUBLANES, SUBLANES)
            hr = hre_scr[pl.ds(s0, SUBLANES), c0:c0 + CB]
            hi_ = him_scr[pl.ds(s0, SUBLANES), c0:c0 + CB]

            def step_body(t, hc):
                hr, hi_ = hc
                r0 = pl.multiple_of(t * S + s0, SUBLANES)
                br = sre_scr[pl.ds(r0, SUBLANES), c0:c0 + CB]
                bi = sim_scr[pl.ds(r0, SUBLANES), c0:c0 + CB]
                nr = lr * hr - li * hi_ + br
                ni = lr * hi_ + li * hr + bi
                sre_scr[pl.ds(r0, SUBLANES), c0:c0 + CB] = nr
                sim_scr[pl.ds(r0, SUBLANES), c0:c0 + CB] = ni
                return nr, ni

            hr, hi_ = lax.fori_loop(0, Tt, step_body, (hr, hi_), unroll=8)
            hre_scr[pl.ds(s0, SUBLANES), c0:c0 + CB] = hr
            him_scr[pl.ds(s0, SUBLANES), c0:c0 + CB] = hi_
            return carry

        lax.fori_loop(0, n_tiles, tile_body, 0)

    hre_out_ref[...] = hre_scr[...]
    him_out_ref[...] = him_scr[...]

    n_ct = SSM_WIDTH // MXU_TILE
    k_per = (N_STATE // n_ct) // MXU_TILE
    ys = []
    for m in range(n_ct):
        acc = d_ref[:, m * MXU_TILE:(m + 1) * MXU_TILE] * us[:, m * MXU_TILE:(m + 1) * MXU_TILE]
        for k in range(k_per):
            kt = m * k_per + k
            acc = acc + jnp.dot(sre_scr[:, kt * MXU_TILE:(kt + 1) * MXU_TILE].astype(BF16), wcre_ref[kt],
                                preferred_element_type=F32)
            acc = acc + jnp.dot(sim_scr[:, kt * MXU_TILE:(kt + 1) * MXU_TILE].astype(BF16), wcim_ref[kt],
                                preferred_element_type=F32)
        ys.append(acc)
    y = jnp.concatenate(ys, axis=-1)
    gl = jax.nn.gelu(y)
    gate = jax.nn.sigmoid(jnp.dot(gl.astype(BF16), gluw_ref[...], preferred_element_type=F32) + glub_ref[...])
    outs.append(gl * gate)

    mix_tm = jnp.concatenate(outs, axis=-1)
    if seq_major:
        mix_ref[...] = jnp.swapaxes(mix_tm.reshape(Tt, S, D), 0, 1).astype(mix_ref.dtype)
    else:
        mix_ref[...] = mix_tm.astype(mix_ref.dtype)


def _mixer(x, sh, sc, g1, w_in_bf, buf0, hre0, him0, wts, *, S, L, Tt, start_pos, seq_major):
    D = D_MODEL
    R = Tt * S
    consts = [g1, w_in_bf, buf0, hre0, him0] + list(wts)
    if seq_major:
        x_spec = pl.BlockSpec((S, Tt, D), lambda i: (0, i, 0))
        mod_specs = [pl.BlockSpec((S, 1, D), lambda i: (0, 0, 0)), pl.BlockSpec((S, 1, D), lambda i: (0, 0, 1))]
        mix_spec = pl.BlockSpec((S, Tt, D), lambda i: (0, i, 0))
        mix_shape = jax.ShapeDtypeStruct((S, L, D), BF16)
    else:
        assert Tt == L
        x_spec = pl.BlockSpec((S, L * D), lambda i: (0, 0))
        mod_specs = [pl.BlockSpec((S, D), lambda i: (0, 0)), pl.BlockSpec((S, D), lambda i: (0, 1))]
        mix_spec = pl.BlockSpec((R, D), lambda i: (0, 0))
        mix_shape = jax.ShapeDtypeStruct((L * S, D), BF16)
    body = functools.partial(_mixer_body, S=S, Tt=Tt, start_pos=start_pos, seq_major=seq_major)
    return pl.pallas_call(
        body,
        grid=(L // Tt,),
        in_specs=[x_spec] + mod_specs + [_const_spec(a.shape) for a in consts],
        out_specs=(mix_spec,
                   pl.BlockSpec((S, POOL_BUF * POOL_WIDTH), lambda i: (0, 0)),
                   pl.BlockSpec((S, N_STATE), lambda i: (0, 0)),
                   pl.BlockSpec((S, N_STATE), lambda i: (0, 0))),
        out_shape=(mix_shape,
                   jax.ShapeDtypeStruct((S, POOL_BUF * POOL_WIDTH), F32),
                   jax.ShapeDtypeStruct((S, N_STATE), F32),
                   jax.ShapeDtypeStruct((S, N_STATE), F32)),
        scratch_shapes=[pltpu.VMEM(((HIST + Tt) * S, POOL_WIDTH), F32),
                        pltpu.VMEM((R, N_STATE), F32),
                        pltpu.VMEM((R, N_STATE), F32),
                        pltpu.VMEM((S, N_STATE), F32),
                        pltpu.VMEM((S, N_STATE), F32)],
        compiler_params=_cparams(("arbitrary",)),
        name="mixer_S%d" % S,
    )(x, sh, sc, *consts)


def _split_bf16(v):
    hi = v.astype(BF16)
    lo = (v - hi.astype(F32)).astype(BF16)
    return hi, lo


def _stage3_rows(x, mix, g1, sh2, sc2, g2, n2g_ref, fng_ref, wout_ref, rw_ref, rb_ref, wg_ref, wu_ref, wd_ref):
    R = x.shape[0]
    mixo = jnp.dot(mix, wout_ref[...], preferred_element_type=F32)
    x1 = x + g1 * mixo
    h2 = _rms(x1, n2g_ref[...]) * (1.0 + sc2) + sh2
    h2_hi, h2_lo = _split_bf16(h2)

    rw_hi, rw_lo = _split_bf16(rw_ref[...])
    logits = (jnp.dot(h2_hi, rw_hi, preferred_element_type=F32)
              + jnp.dot(h2_lo, rw_hi, preferred_element_type=F32)
              + jnp.dot(h2_hi, rw_lo, preferred_element_type=F32)) + rb_ref[...]
    lane = lax.broadcasted_iota(jnp.int32, (R, ROUTER_LANES), 1).astype(F32)
    ninf = jnp.float32(-jnp.inf)
    none = jnp.float32(ROUTER_LANES)
    is_g = lane < N_EXPERT_GROUPS
    l1 = jnp.where(is_g, logits, ninf)
    m1 = jnp.max(l1, axis=-1, keepdims=True)
    gidx = jnp.min(jnp.where(l1 == m1, lane, none), axis=-1, keepdims=True)
    p_top = 1.0 / jnp.sum(jnp.where(is_g, jnp.exp(logits - m1), 0.0), axis=-1, keepdims=True)
    e_lo = N_EXPERT_GROUPS + gidx * EXPERTS_PER_GROUP
    sel = (lane >= e_lo) & (lane < e_lo + EXPERTS_PER_GROUP)
    l2 = jnp.where(sel, logits, ninf)
    va = jnp.max(l2, axis=-1, keepdims=True)
    ia = jnp.min(jnp.where(l2 == va, lane, none), axis=-1, keepdims=True)
    l2b = jnp.where(lane == ia, ninf, l2)
    vb = jnp.max(l2b, axis=-1, keepdims=True)
    ib = jnp.min(jnp.where(l2b == vb, lane, none), axis=-1, keepdims=True)
    eb = jnp.exp(vb - va)
    den = 1.0 + eb
    gates = jnp.where(lane == ia, (1.0 / den) * p_top, 0.0) + jnp.where(lane == ib, (eb / den) * p_top, 0.0)

    acc = jnp.zeros((R, D_MODEL), F32)
    for e in range(N_EXPERTS):
        a = jnp.dot(h2_hi, wg_ref[e], preferred_element_type=F32)
        b = jnp.dot(h2_hi, wu_ref[e], preferred_element_type=F32)
        ge = gates[:, N_EXPERT_GROUPS + e:N_EXPERT_GROUPS + e + 1]
        hid = jax.nn.silu(a) * b * ge
        acc = acc + jnp.dot(hid.astype(BF16), wd_ref[e], preferred_element_type=F32)
    x2 = x1 + g2 * acc
    return _rms(x2, fng_ref[...])


def _stage3_prompt_body(x_ref, mix_ref, g1_ref, sh2_ref, sc2_ref, g2_ref, *rest):
    wrefs, y_ref = rest[:-1], rest[-1]
    y_ref[0] = _stage3_rows(x_ref[0], mix_ref[0], g1_ref[0], sh2_ref[0], sc2_ref[0], g2_ref[0], *wrefs)


def _stage3_sample_body(x_ref, mix_ref, g1_ref, sh2_ref, sc2_ref, g2_ref, *rest, S, L):
    wrefs, y_ref = rest[:-1], rest[-1]
    D = D_MODEL
    x_tm = jnp.concatenate([x_ref[:, t * D:(t + 1) * D] for t in range(L)], axis=0)

    def rows(r):
        return jnp.concatenate([r[...]] * L, axis=0)

    y = _stage3_rows(x_tm, mix_ref[...], rows(g1_ref), rows(sh2_ref), rows(sc2_ref), rows(g2_ref), *wrefs)
    for t in range(L):
        y_ref[:, t * D:(t + 1) * D] = y[t * S:(t + 1) * S, :]


def _stage3_prompt(x, mix, mod3, wts, tc):
    nb, L, D = x.shape
    consts = list(wts)

    def mspec(k):
        return pl.BlockSpec((1, 1, D), lambda n, c, k=k: (n, 0, k))

    return pl.pallas_call(
        _stage3_prompt_body,
        grid=(nb, L // tc),
        in_specs=[pl.BlockSpec((1, tc, D), lambda n, c: (n, c, 0)),
                  pl.BlockSpec((1, tc, D), lambda n, c: (n, c, 0)),
                  mspec(2), mspec(3), mspec(4), mspec(5)] + [_const_spec(a.shape) for a in consts],
        out_specs=pl.BlockSpec((1, tc, D), lambda n, c: (n, c, 0)),
        out_shape=jax.ShapeDtypeStruct((nb, L, D), F32),
        compiler_params=_cparams(("arbitrary", "arbitrary")),
        name="stage3_prompt",
    )(x, mix, mod3, mod3, mod3, mod3, *consts)


def _stage3_sample(x2d, mix_tm, mod2, wts):
    ns, LD = x2d.shape
    D = D_MODEL
    L = LD // D
    consts = list(wts)

    def mspec(k):
        return pl.BlockSpec((ns, D), lambda i, k=k: (0, k))

    return pl.pallas_call(
        functools.partial(_stage3_sample_body, S=ns, L=L),
        grid=(1,),
        in_specs=[pl.BlockSpec((ns, LD), lambda i: (0, 0)),
                  pl.BlockSpec((L * ns, D), lambda i: (0, 0)),
                  mspec(2), mspec(3), mspec(4), mspec(5)] + [_const_spec(a.shape) for a in consts],
        out_specs=pl.BlockSpec((ns, LD), lambda i: (0, 0)),
        out_shape=jax.ShapeDtypeStruct((ns, LD), F32),
        compiler_params=_cparams(("arbitrary",)),
        name="stage3_sample",
    )(x2d, mix_tm, mod2, mod2, mod2, mod2, *consts)


def _block_diag_tiles_b(bb_t):
    G, H, P = bb_t.shape
    eye = jnp.eye(G, dtype=bb_t.dtype)
    full = (bb_t[:, :, None, :] * eye[:, None, :, None]).reshape(G * H, G * P)
    n_t = (G * P) // MXU_TILE
    fr = full.reshape((G * H) // MXU_TILE, MXU_TILE, n_t, MXU_TILE)
    per_kb = n_t // ((G * H) // MXU_TILE)
    return jnp.stack([fr[n // per_kb, :, n, :] for n in range(n_t)], axis=0).astype(BF16)


def _block_diag_tiles_c(c):
    G, H, P = c.shape
    eye = jnp.eye(G, dtype=c.dtype)
    full = (jnp.transpose(c, (0, 2, 1))[:, :, None, :] * eye[:, None, :, None]).reshape(G * P, G * H)
    n_k = (G * P) // MXU_TILE
    n_m = (G * H) // MXU_TILE
    fr = full.reshape(n_k, MXU_TILE, n_m, MXU_TILE)
    per_m = n_k // n_m
    return jnp.stack([fr[kt, :, kt // per_m, :] for kt in range(n_k)], axis=0).astype(BF16)


def kernel(x_prompt, x_sample, c_prompt, c_sample, state_pool, state_ssm_re, state_ssm_im, w_ada, b_ada, norm1_g, w_in, pool_w, pool_scale, ssm_a_re, ssm_a_im, ssm_log_dt, ssm_b_re, ssm_b_im, ssm_c_re, ssm_c_im, ssm_d, glu_w, glu_b, w_out, norm2_g, router_w1, router_b1, router_w2, router_b2, exp_w_gate, exp_w_up, exp_w_down, final_norm_g):
    depth = w_ada.shape[0]
    assert depth == 1
    l = 0
    nb, L, D = x_prompt.shape
    ns, Ls, _ = x_sample.shape

    lb_re, lb_im, bb_re_t, bb_im_t = _ssm_prep(ssm_a_re[l], ssm_a_im[l], ssm_log_dt[l],
                                               jnp.transpose(ssm_b_re[l], (0, 2, 1)),
                                               jnp.transpose(ssm_b_im[l], (0, 2, 1)))
    wbre = _block_diag_tiles_b(bb_re_t)
    wbim = _block_diag_tiles_b(bb_im_t)
    wcre = _block_diag_tiles_c(ssm_c_re[l])
    wcim = _block_diag_tiles_c(-ssm_c_im[l])
    mix_wts = (pool_w[l].astype(BF16), pool_scale[l].reshape(1, -1), lb_re.reshape(1, -1), lb_im.reshape(1, -1),
               wbre, wbim, wcre, wcim, ssm_d[l].reshape(1, -1), glu_w[l].astype(BF16), glu_b[l].reshape(1, -1))

    rw = jnp.concatenate([router_w1[l], jnp.transpose(router_w2[l], (1, 0, 2)).reshape(D, N_EXPERTS)], axis=1)
    rw = jnp.pad(rw, ((0, 0), (0, ROUTER_LANES - rw.shape[1])))
    rb = jnp.concatenate([router_b1[l], router_b2[l].reshape(-1)])
    rb = jnp.pad(rb, (0, ROUTER_LANES - rb.shape[0])).reshape(1, -1)
    s3_wts = (norm2_g[l].reshape(1, -1), final_norm_g.reshape(1, -1), w_out[l].astype(BF16), rw, rb,
              exp_w_gate[l].astype(BF16), exp_w_up[l].astype(BF16), exp_w_down[l].astype(BF16))

    mod = _adaln(jnp.concatenate([c_prompt, c_sample], axis=0), w_ada[l], b_ada[l])
    mod_p = mod[:nb].reshape(nb, 1, N_MOD * D)
    mod_s = mod[nb:]

    g1 = norm1_g[l].reshape(1, -1)
    w_in_bf = w_in[l].astype(BF16)

    zeros_buf = jnp.zeros((nb, POOL_BUF * POOL_WIDTH), F32)
    zeros_h = jnp.zeros((nb, N_STATE), F32)
    mix_p, pool_p, hre_p, him_p = _mixer(x_prompt, mod_p, mod_p, g1, w_in_bf, zeros_buf, zeros_h, zeros_h, mix_wts,
                                         S=nb, L=L, Tt=64, start_pos=0, seq_major=True)
    y_p = _stage3_prompt(x_prompt, mix_p, mod_p, s3_wts, 512)

    xs2d = x_sample.reshape(ns, Ls * D)
    mix_s, pool_s, hre_s, him_s = _mixer(xs2d, mod_s, mod_s, g1, w_in_bf,
                                         state_pool[l].reshape(ns, POOL_BUF * POOL_WIDTH),
                                         state_ssm_re[l].reshape(ns, N_STATE), state_ssm_im[l].reshape(ns, N_STATE),
                                         mix_wts, S=ns, L=Ls, Tt=Ls, start_pos=PAST_LEN, seq_major=False)
    y_s = _stage3_sample(xs2d, mix_s, mod_s, s3_wts).reshape(ns, Ls, D)

    def st(a, n):
        return a.reshape(1, n, N_SSM_GROUPS, SSM_STATE)

    return (y_p, y_s,
            pool_p.reshape(1, nb, POOL_BUF, POOL_WIDTH), pool_s.reshape(1, ns, POOL_BUF, POOL_WIDTH),
            st(hre_p, nb), st(him_p, nb), st(hre_s, ns), st(him_s, ns))
```

```python
import functools

import jax
import jax.numpy as jnp
from jax import lax
from jax.experimental import pallas as pl
from jax.experimental.pallas import tpu as pltpu

D_MODEL = 1024
POOL_WIDTH = 512
SSM_WIDTH = 512
POOL_WINDOWS = (2, 4, 8, 16)
POOL_GROUP = 128
POOL_BUF = 15
HIST = 16
SSM_GROUP = 16
N_SSM_GROUPS = 32
SSM_STATE = 64
N_STATE = N_SSM_GROUPS * SSM_STATE
N_EXPERT_GROUPS = 4
EXPERTS_PER_GROUP = 4
N_EXPERTS = 16
EXPERT_HIDDEN = 256
N_MOD = 6
EPS = 1e-6
PAST_LEN = 16384

MXU_TILE = 256
LANES = 128
SUBLANES = 8
ROUTER_LANES = 128
VMEM_LIMIT = 56 * 1024 * 1024
N_STATE_TILES = N_STATE // MXU_TILE

F32 = jnp.float32
BF16 = jnp.bfloat16


def _cparams(sem):
    return pltpu.CompilerParams(dimension_semantics=sem, vmem_limit_bytes=VMEM_LIMIT)


def _const_spec(shape):
    nd = len(shape)
    return pl.BlockSpec(shape, lambda *_: (0,) * nd, pipeline_mode=pl.Buffered(1))


def _rms(x, g):
    return x * lax.rsqrt(jnp.mean(x * x, axis=-1, keepdims=True) + EPS) * g


def _ssm_prep_body(are_ref, aim_ref, ldt_ref, bre_ref, bim_ref, cre_ref, cim_ref, d_ref,
                   lbre_ref, lbim_ref, dflat_ref, wbre_ref, wbim_ref, wcre_ref, wcim_ref):
    G, H, P = N_SSM_GROUPS, SSM_GROUP, SSM_STATE
    a_re = are_ref[...]
    a_im = aim_ref[...]
    dt = jnp.exp(ldt_ref[...])
    mag = jnp.exp(a_re * dt)
    lb_re = mag * jnp.cos(a_im * dt)
    lb_im = mag * jnp.sin(a_im * dt)
    den = a_re * a_re + a_im * a_im
    nr = lb_re - 1.0
    ni = lb_im
    k_re = (nr * a_re + ni * a_im) / den
    k_im = (ni * a_re - nr * a_im) / den
    b_re = bre_ref[...]
    b_im = bim_ref[...]
    bb_re = k_re[:, None, :] * b_re - k_im[:, None, :] * b_im
    bb_im = k_re[:, None, :] * b_im + k_im[:, None, :] * b_re
    c_re = cre_ref[...]
    c_im = cim_ref[...]
    d = d_ref[...]

    zeros = jnp.zeros((N_STATE_TILES, MXU_TILE, MXU_TILE), BF16)
    wbre_ref[...] = zeros
    wbim_ref[...] = zeros
    wcre_ref[...] = zeros
    wcim_ref[...] = zeros
    g_per_tile = MXU_TILE // P
    g_per_blk = MXU_TILE // H
    for g in range(G):
        lbre_ref[:, g * P:(g + 1) * P] = lb_re[g:g + 1, :]
        lbim_ref[:, g * P:(g + 1) * P] = lb_im[g:g + 1, :]
        dflat_ref[:, g * H:(g + 1) * H] = d[g:g + 1, :]
        n, gi = divmod(g, g_per_tile)
        r0 = (g % g_per_blk) * H
        c0 = gi * P
        wbre_ref[n, r0:r0 + H, c0:c0 + P] = bb_re[g].astype(BF16)
        wbim_ref[n, r0:r0 + H, c0:c0 + P] = bb_im[g].astype(BF16)
        wcre_ref[n, c0:c0 + P, r0:r0 + H] = c_re[g].astype(BF16)
        wcim_ref[n, c0:c0 + P, r0:r0 + H] = (-c_im[g]).astype(BF16)


def _ssm_prep(a_re, a_im, log_dt, b_re_t, b_im_t, c_re_t, c_im_t, d):
    G = N_SSM_GROUPS
    tile = jax.ShapeDtypeStruct((N_STATE_TILES, MXU_TILE, MXU_TILE), BF16)
    return pl.pallas_call(
        _ssm_prep_body,
        out_shape=(jax.ShapeDtypeStruct((1, N_STATE), F32), jax.ShapeDtypeStruct((1, N_STATE), F32),
                   jax.ShapeDtypeStruct((1, SSM_WIDTH), F32), tile, tile, tile, tile),
        name="ssm_prep",
    )(a_re, a_im, log_dt.reshape(G, 1), b_re_t, b_im_t, c_re_t, c_im_t, d)


def _adaln_body(cp_ref, cs_ref, w_ref, b_ref, op_ref, os_ref):
    w = w_ref[...].astype(BF16)
    b = b_ref[...]
    op_ref[...] = jnp.dot(jax.nn.silu(cp_ref[...]).astype(BF16), w, preferred_element_type=F32) + b
    os_ref[...] = jnp.dot(jax.nn.silu(cs_ref[...]).astype(BF16), w, preferred_element_type=F32) + b


def _adaln(c_p, c_s, w_ada, b_ada):
    n_p, n_s = c_p.shape[0], c_s.shape[0]
    tn = D_MODEL
    return pl.pallas_call(
        _adaln_body,
        grid=(N_MOD * D_MODEL // tn,),
        in_specs=[pl.BlockSpec((n_p, D_MODEL), lambda j: (0, 0)),
                  pl.BlockSpec((n_s, D_MODEL), lambda j: (0, 0)),
                  pl.BlockSpec((D_MODEL, tn), lambda j: (0, j)),
                  pl.BlockSpec((1, tn), lambda j: (0, j))],
        out_specs=(pl.BlockSpec((n_p, tn), lambda j: (0, j)), pl.BlockSpec((n_s, tn), lambda j: (0, j))),
        out_shape=(jax.ShapeDtypeStruct((n_p, N_MOD * D_MODEL), F32),
                   jax.ShapeDtypeStruct((n_s, N_MOD * D_MODEL), F32)),
        compiler_params=_cparams(("arbitrary",)),
        name="adaln",
    )(c_p, c_s, w_ada, b_ada.reshape(1, -1))


def _mixer_body(*refs, S, Tt, start_pos, seq_major, has_state):
    refs = list(refs)
    x_ref, sh_ref, sc_ref, g_ref, win_ref = refs[:5]
    k = 5
    if has_state:
        buf0_ref, hre0_ref, him0_ref = refs[k:k + 3]
        k += 3
    (poolw_ref, pscale_ref, lbre_ref, lbim_ref, wbre_ref, wbim_ref, wcre_ref, wcim_ref, d_ref, gluw_ref,
     glub_ref) = refs[k:k + 11]
    k += 11
    mix_ref, newbuf_ref, hre_out_ref, him_out_ref = refs[k:k + 4]
    z_scr, sre_scr, sim_scr, hre_scr, him_scr, winb_scr, glub_scr = refs[k + 4:]

    i = pl.program_id(0)
    R = Tt * S
    HR = HIST * S
    D = D_MODEL

    @pl.when(i == 0)
    def _init():
        z_scr[0:S, :] = jnp.zeros((S, POOL_WIDTH), F32)
        if has_state:
            for j in range(POOL_BUF):
                z_scr[(j + 1) * S:(j + 2) * S, :] = buf0_ref[:, j, :]
            hre_scr[...] = hre0_ref[...]
            him_scr[...] = him0_ref[...]
        else:
            z_scr[S:HR, :] = jnp.zeros((HR - S, POOL_WIDTH), F32)
            hre_scr[...] = jnp.zeros((S, N_STATE), F32)
            him_scr[...] = jnp.zeros((S, N_STATE), F32)
        winb_scr[...] = win_ref[...].astype(BF16)
        glub_scr[...] = gluw_ref[...].astype(BF16)

    g = g_ref[...]
    if seq_major:
        x3 = x_ref[...]
        h3 = _rms(x3, g) * (1.0 + sc_ref[...][:, None, :]) + sh_ref[...][:, None, :]
        u_nm = jnp.dot(h3.reshape(R, D).astype(BF16), winb_scr[...], preferred_element_type=F32)
        u = jnp.swapaxes(u_nm.reshape(S, Tt, D), 0, 1).reshape(R, D)
    else:
        x_tm = jnp.concatenate([x_ref[:, t, :] for t in range(Tt)], axis=0)
        h3 = _rms(x_tm, g).reshape(Tt, S, D) * (1.0 + sc_ref[...]) + sh_ref[...]
        u = jnp.dot(h3.reshape(R, D).astype(BF16), winb_scr[...], preferred_element_type=F32)

    up = u[:, 0:POOL_WIDTH]
    us = u[:, POOL_WIDTH:D]
    z_scr[HR:HR + R, :] = up

    row = lax.broadcasted_iota(jnp.int32, (R, 1), 0)
    pos = start_pos + i * Tt + lax.shift_right_logical(row, S.bit_length() - 1)
    outs = []
    for kk, w in enumerate(POOL_WINDOWS):
        lo, hi = kk * POOL_GROUP, (kk + 1) * POOL_GROUP
        cur = z_scr[:, lo:hi]
        step = 1
        while step < w:
            cur = cur[step * S:, :] + cur[:cur.shape[0] - step * S, :]
            step *= 2
        s = cur[cur.shape[0] - R:, :]
        cnt = jnp.minimum(w, pos + 1).astype(F32)
        pooled = s / cnt - up[:, lo:hi]
        mixed = jnp.dot(pooled.astype(BF16), poolw_ref[kk].astype(BF16), preferred_element_type=F32)
        outs.append(mixed * pscale_ref[:, lo:hi])

    for j in range(POOL_BUF):
        r0 = (Tt + 1 + j) * S
        newbuf_ref[:, j, :] = z_scr[r0:r0 + S, :]
    hist = z_scr[R:R + HR, :]
    z_scr[0:HR, :] = hist

    usb = us.astype(BF16)
    for n in range(N_STATE_TILES):
        kb = (n * MXU_TILE // SSM_STATE * SSM_GROUP) // MXU_TILE
        lhs = usb[:, kb * MXU_TILE:(kb + 1) * MXU_TILE]
        sre_scr[:, n * MXU_TILE:(n + 1) * MXU_TILE] = jnp.dot(lhs, wbre_ref[n], preferred_element_type=F32)
        sim_scr[:, n * MXU_TILE:(n + 1) * MXU_TILE] = jnp.dot(lhs, wbim_ref[n], preferred_element_type=F32)

    CB = 512
    n_tiles = S // SUBLANES
    for cb in range(N_STATE // CB):
        c0 = cb * CB
        lr = jnp.broadcast_to(lbre_ref[:, c0:c0 + CB], (SUBLANES, CB))
        li = jnp.broadcast_to(lbim_ref[:, c0:c0 + CB], (SUBLANES, CB))

        def tile_body(j, carry, c0=c0, lr=lr, li=li):
            s0 = pl.multiple_of(j * SUBLANES, SUBLANES)
            hr = hre_scr[pl.ds(s0, SUBLANES), c0:c0 + CB]
            hi_ = him_scr[pl.ds(s0, SUBLANES), c0:c0 + CB]

            def step_body(t, hc):
                hr, hi_ = hc
                r0 = pl.multiple_of(t * S + s0, SUBLANES)
                br = sre_scr[pl.ds(r0, SUBLANES), c0:c0 + CB]
                bi = sim_scr[pl.ds(r0, SUBLANES), c0:c0 + CB]
                nr = lr * hr - li * hi_ + br
                ni = lr * hi_ + li * hr + bi
                sre_scr[pl.ds(r0, SUBLANES), c0:c0 + CB] = nr
                sim_scr[pl.ds(r0, SUBLANES), c0:c0 + CB] = ni
                return nr, ni

            hr, hi_ = lax.fori_loop(0, Tt, step_body, (hr, hi_), unroll=8)
            hre_scr[pl.ds(s0, SUBLANES), c0:c0 + CB] = hr
            him_scr[pl.ds(s0, SUBLANES), c0:c0 + CB] = hi_
            return carry

        lax.fori_loop(0, n_tiles, tile_body, 0)

    hre_out_ref[...] = hre_scr[...]
    him_out_ref[...] = him_scr[...]

    n_ct = SSM_WIDTH // MXU_TILE
    k_per = N_STATE_TILES // n_ct
    ys = []
    for m in range(n_ct):
        acc = d_ref[:, m * MXU_TILE:(m + 1) * MXU_TILE] * us[:, m * MXU_TILE:(m + 1) * MXU_TILE]
        for kk in range(k_per):
            kt = m * k_per + kk
            acc = acc + jnp.dot(sre_scr[:, kt * MXU_TILE:(kt + 1) * MXU_TILE].astype(BF16), wcre_ref[kt],
                                preferred_element_type=F32)
            acc = acc + jnp.dot(sim_scr[:, kt * MXU_TILE:(kt + 1) * MXU_TILE].astype(BF16), wcim_ref[kt],
                                preferred_element_type=F32)
        ys.append(acc)
    y = jnp.concatenate(ys, axis=-1)
    gl = jax.nn.gelu(y)
    gate = jax.nn.sigmoid(jnp.dot(gl.astype(BF16), glub_scr[...], preferred_element_type=F32) + glub_ref[...])
    outs.append(gl * gate)

    mix_tm = jnp.concatenate(outs, axis=-1)
    if seq_major:
        mix_ref[...] = jnp.swapaxes(mix_tm.reshape(Tt, S, D), 0, 1).astype(mix_ref.dtype)
    else:
        mix_ref[...] = mix_tm.astype(mix_ref.dtype)


def _mixer(x, mod, g1, w_in, state, wts, *, Tt, start_pos, seq_major):
    S, L, D = x.shape
    R = Tt * S
    has_state = state is not None
    consts = [g1, w_in] + (list(state) if has_state else []) + list(wts)
    x_spec = pl.BlockSpec((S, Tt, D), lambda i: (0, i, 0))
    mod_specs = [pl.BlockSpec((S, D), lambda i: (0, 0)), pl.BlockSpec((S, D), lambda i: (0, 1))]
    if seq_major:
        mix_spec = pl.BlockSpec((S, Tt, D), lambda i: (0, i, 0))
        mix_shape = jax.ShapeDtypeStruct((S, L, D), BF16)
    else:
        assert Tt == L
        mix_spec = pl.BlockSpec((R, D), lambda i: (0, 0))
        mix_shape = jax.ShapeDtypeStruct((L * S, D), BF16)
    body = functools.partial(_mixer_body, S=S, Tt=Tt, start_pos=start_pos, seq_major=seq_major, has_state=has_state)
    return pl.pallas_call(
        body,
        grid=(L // Tt,),
        in_specs=[x_spec] + mod_specs + [_const_spec(a.shape) for a in consts],
        out_specs=(mix_spec,
                   pl.BlockSpec((S, POOL_BUF, POOL_WIDTH), lambda i: (0, 0, 0)),
                   pl.BlockSpec((S, N_STATE), lambda i: (0, 0)),
                   pl.BlockSpec((S, N_STATE), lambda i: (0, 0))),
        out_shape=(mix_shape,
                   jax.ShapeDtypeStruct((S, POOL_BUF, POOL_WIDTH), F32),
                   jax.ShapeDtypeStruct((S, N_STATE), F32),
                   jax.ShapeDtypeStruct((S, N_STATE), F32)),
        scratch_shapes=[pltpu.VMEM(((HIST + Tt) * S, POOL_WIDTH), F32),
                        pltpu.VMEM((R, N_STATE), F32),
                        pltpu.VMEM((R, N_STATE), F32),
                        pltpu.VMEM((S, N_STATE), F32),
                        pltpu.VMEM((S, N_STATE), F32),
                        pltpu.VMEM((D, D), BF16),
                        pltpu.VMEM((SSM_WIDTH, SSM_WIDTH), BF16)],
        compiler_params=_cparams(("arbitrary",)),
        name="mixer_S%d" % S,
    )(x, mod, mod, *consts)


def _split_bf16(v):
    hi = v.astype(BF16)
    lo = (v - hi.astype(F32)).astype(BF16)
    return hi, lo


def _stage3_rows(x, mix, g1, sh2, sc2, g2, n2g_ref, fng_ref, wout_ref, rw_ref, rb_ref, wg_ref, wu_ref, wd_ref):
    R = x.shape[0]
    mixo = jnp.dot(mix, wout_ref[...], preferred_element_type=F32)
    x1 = x + g1 * mixo
    h2 = _rms(x1, n2g_ref[...]) * (1.0 + sc2) + sh2
    h2_hi, h2_lo = _split_bf16(h2)

    rw_hi, rw_lo = _split_bf16(rw_ref[...])
    logits = (jnp.dot(h2_hi, rw_hi, preferred_element_type=F32)
              + jnp.dot(h2_lo, rw_hi, preferred_element_type=F32)
              + jnp.dot(h2_hi, rw_lo, preferred_element_type=F32)) + rb_ref[...]
    lane = lax.broadcasted_iota(jnp.int32, (R, ROUTER_LANES), 1).astype(F32)
    ninf = jnp.float32(-jnp.inf)
    none = jnp.float32(ROUTER_LANES)
    is_g = lane < N_EXPERT_GROUPS
    l1 = jnp.where(is_g, logits, ninf)
    m1 = jnp.max(l1, axis=-1, keepdims=True)
    gidx = jnp.min(jnp.where(l1 == m1, lane, none), axis=-1, keepdims=True)
    p_top = 1.0 / jnp.sum(jnp.where(is_g, jnp.exp(logits - m1), 0.0), axis=-1, keepdims=True)
    e_lo = N_EXPERT_GROUPS + gidx * EXPERTS_PER_GROUP
    sel = (lane >= e_lo) & (lane < e_lo + EXPERTS_PER_GROUP)
    l2 = jnp.where(sel, logits, ninf)
    va = jnp.max(l2, axis=-1, keepdims=True)
    ia = jnp.min(jnp.where(l2 == va, lane, none), axis=-1, keepdims=True)
    l2b = jnp.where(lane == ia, ninf, l2)
    vb = jnp.max(l2b, axis=-1, keepdims=True)
    ib = jnp.min(jnp.where(l2b == vb, lane, none), axis=-1, keepdims=True)
    eb = jnp.exp(vb - va)
    den = 1.0 + eb
    gates = jnp.where(lane == ia, (1.0 / den) * p_top, 0.0) + jnp.where(lane == ib, (eb / den) * p_top, 0.0)

    acc = jnp.zeros((R, D_MODEL), F32)
    for e in range(N_EXPERTS):
        a = jnp.dot(h2_hi, wg_ref[e], preferred_element_type=F32)
        b = jnp.dot(h2_hi, wu_ref[e], preferred_element_type=F32)
        ge = gates[:, N_EXPERT_GROUPS + e:N_EXPERT_GROUPS + e + 1]
        hid = jax.nn.silu(a) * b * ge
        acc = acc + jnp.dot(hid.astype(BF16), wd_ref[e], preferred_element_type=F32)
    x2 = x1 + g2 * acc
    return _rms(x2, fng_ref[...])


def _stage3_prompt_body(x_ref, mix_ref, g1_ref, sh2_ref, sc2_ref, g2_ref, *rest):
    wrefs, y_ref = rest[:-1], rest[-1]
    n = pl.program_id(0)
    mods = [r[pl.ds(n, 1), :] for r in (g1_ref, sh2_ref, sc2_ref, g2_ref)]
    y_ref[0] = _stage3_rows(x_ref[0], mix_ref[0], *mods, *wrefs)


def _stage3_sample_body(x_ref, mix_ref, g1_ref, sh2_ref, sc2_ref, g2_ref, *rest, S, L):
    wrefs, y_ref = rest[:-1], rest[-1]
    x_tm = jnp.concatenate([x_ref[:, t, :] for t in range(L)], axis=0)

    def rows(r):
        return jnp.concatenate([r[...]] * L, axis=0)

    y = _stage3_rows(x_tm, mix_ref[...], rows(g1_ref), rows(sh2_ref), rows(sc2_ref), rows(g2_ref), *wrefs)
    for t in range(L):
        y_ref[:, t, :] = y[t * S:(t + 1) * S, :]


def _stage3_prompt(x, mix, mod, wts, tc):
    nb, L, D = x.shape
    consts = list(wts)

    def mspec(k):
        return pl.BlockSpec((nb, D), lambda n, c, k=k: (0, k))

    return pl.pallas_call(
        _stage3_prompt_body,
        grid=(nb, L // tc),
        in_specs=[pl.BlockSpec((1, tc, D), lambda n, c: (n, c, 0)),
                  pl.BlockSpec((1, tc, D), lambda n, c: (n, c, 0)),
                  mspec(2), mspec(3), mspec(4), mspec(5)] + [_const_spec(a.shape) for a in consts],
        out_specs=pl.BlockSpec((1, tc, D), lambda n, c: (n, c, 0)),
        out_shape=jax.ShapeDtypeStruct((nb, L, D), F32),
        compiler_params=_cparams(("arbitrary", "arbitrary")),
        name="stage3_prompt",
    )(x, mix, mod, mod, mod, mod, *consts)


def _stage3_sample(x, mix_tm, mod, wts):
    ns, L, D = x.shape
    consts = list(wts)

    def mspec(k):
        return pl.BlockSpec((ns, D), lambda i, k=k: (0, k))

    return pl.pallas_call(
        functools.partial(_stage3_sample_body, S=ns, L=L),
        grid=(1,),
        in_specs=[pl.BlockSpec((ns, L, D), lambda i: (0, 0, 0)),
                  pl.BlockSpec((L * ns, D), lambda i: (0, 0)),
                  mspec(2), mspec(3), mspec(4), mspec(5)] + [_const_spec(a.shape) for a in consts],
        out_specs=pl.BlockSpec((ns, L, D), lambda i: (0, 0, 0)),
        out_shape=jax.ShapeDtypeStruct((ns, L, D), F32),
        compiler_params=_cparams(("arbitrary",)),
        name="stage3_sample",
    )(x, mix_tm, mod, mod, mod, mod, *consts)


def kernel(x_prompt, x_sample, c_prompt, c_sample, state_pool, state_ssm_re, state_ssm_im, w_ada, b_ada, norm1_g, w_in, pool_w, pool_scale, ssm_a_re, ssm_a_im, ssm_log_dt, ssm_b_re, ssm_b_im, ssm_c_re, ssm_c_im, ssm_d, glu_w, glu_b, w_out, norm2_g, router_w1, router_b1, router_w2, router_b2, exp_w_gate, exp_w_up, exp_w_down, final_norm_g):
    depth = w_ada.shape[0]
    assert depth == 1
    l = 0
    nb, L, D = x_prompt.shape
    ns, Ls, _ = x_sample.shape

    lb_re, lb_im, d_flat, wbre, wbim, wcre, wcim = _ssm_prep(
        ssm_a_re[l], ssm_a_im[l], ssm_log_dt[l],
        jnp.transpose(ssm_b_re[l], (0, 2, 1)), jnp.transpose(ssm_b_im[l], (0, 2, 1)),
        jnp.transpose(ssm_c_re[l], (0, 2, 1)), jnp.transpose(ssm_c_im[l], (0, 2, 1)), ssm_d[l])
    mix_wts = (pool_w[l], pool_scale[l].reshape(1, -1), lb_re, lb_im, wbre, wbim, wcre, wcim, d_flat,
               glu_w[l], glu_b[l].reshape(1, -1))

    rw = jnp.concatenate([router_w1[l], jnp.transpose(router_w2[l], (1, 0, 2)).reshape(D, N_EXPERTS)], axis=1)
    rw = jnp.pad(rw, ((0, 0), (0, ROUTER_LANES - rw.shape[1])))
    rb = jnp.concatenate([router_b1[l], router_b2[l].reshape(-1)])
    rb = jnp.pad(rb, (0, ROUTER_LANES - rb.shape[0])).reshape(1, -1)
    s3_wts = (norm2_g[l].reshape(1, -1), final_norm_g.reshape(1, -1), w_out[l].astype(BF16), rw, rb,
              exp_w_gate[l].astype(BF16), exp_w_up[l].astype(BF16), exp_w_down[l].astype(BF16))

    mod_p, mod_s = _adaln(c_prompt, c_sample, w_ada[l], b_ada[l])
    g1 = norm1_g[l].reshape(1, -1)

    mix_p, pool_p, hre_p, him_p = _mixer(x_prompt, mod_p, g1, w_in[l], None, mix_wts,
                                         Tt=64, start_pos=0, seq_major=True)
    y_p = _stage3_prompt(x_prompt, mix_p, mod_p, s3_wts, 512)

    state = (state_pool[l], state_ssm_re[l].reshape(ns, N_STATE), state_ssm_im[l].reshape(ns, N_STATE))
    mix_s, pool_s, hre_s, him_s = _mixer(x_sample, mod_s, g1, w_in[l], state, mix_wts,
                                         Tt=Ls, start_pos=PAST_LEN, seq_major=False)
    y_s = _stage3_sample(x_sample, mix_s, mod_s, s3_wts)

    def st(a, n):
        return a.reshape(1, n, N_SSM_GROUPS, SSM_STATE)

    return (y_p, y_s, pool_p[None], pool_s[None], st(hre_p, nb), st(him_p, nb), st(hre_s, ns), st(him_s, ns))
```

```python
import functools

import jax
import jax.numpy as jnp
from jax import lax
from jax.experimental import pallas as pl
from jax.experimental.pallas import tpu as pltpu

D_MODEL = 1024
POOL_WIDTH = 512
SSM_WIDTH = 512
POOL_WINDOWS = (2, 4, 8, 16)
POOL_GROUP = 128
POOL_BUF = 15
HIST = 16
SSM_GROUP = 16
N_SSM_GROUPS = 32
SSM_STATE = 64
N_STATE = N_SSM_GROUPS * SSM_STATE
N_EXPERT_GROUPS = 4
EXPERTS_PER_GROUP = 4
N_EXPERTS = 16
EXPERT_HIDDEN = 256
N_MOD = 6
EPS = 1e-6
PAST_LEN = 16384

MXU_TILE = 256
LANES = 128
SUBLANES = 8
ROUTER_LANES = 128
VMEM_LIMIT = 56 * 1024 * 1024
N_STATE_TILES = N_STATE // MXU_TILE

F32 = jnp.float32
BF16 = jnp.bfloat16


def _cparams(sem):
    return pltpu.CompilerParams(dimension_semantics=sem, vmem_limit_bytes=VMEM_LIMIT)


def _const_spec(shape):
    nd = len(shape)
    return pl.BlockSpec(shape, lambda *_: (0,) * nd, pipeline_mode=pl.Buffered(1))


def _rms(x, g):
    return x * lax.rsqrt(jnp.mean(x * x, axis=-1, keepdims=True) + EPS) * g


def _ssm_prep_body(are_ref, aim_ref, ldt_ref, bre_ref, bim_ref, cre_ref, cim_ref, d_ref,
                   lbre_ref, lbim_ref, dflat_ref, wbre_ref, wbim_ref, wcre_ref, wcim_ref):
    G, H, P = N_SSM_GROUPS, SSM_GROUP, SSM_STATE
    a_re = are_ref[...]
    a_im = aim_ref[...]
    dt = jnp.exp(ldt_ref[...])
    mag = jnp.exp(a_re * dt)
    lb_re = mag * jnp.cos(a_im * dt)
    lb_im = mag * jnp.sin(a_im * dt)
    den = a_re * a_re + a_im * a_im
    nr = lb_re - 1.0
    ni = lb_im
    k_re = (nr * a_re + ni * a_im) / den
    k_im = (ni * a_re - nr * a_im) / den
    b_re = bre_ref[...]
    b_im = bim_ref[...]
    bb_re = k_re[:, None, :] * b_re - k_im[:, None, :] * b_im
    bb_im = k_re[:, None, :] * b_im + k_im[:, None, :] * b_re
    c_re = cre_ref[...]
    c_im = cim_ref[...]
    d = d_ref[...]

    zeros = jnp.zeros((N_STATE_TILES, MXU_TILE, MXU_TILE), BF16)
    wbre_ref[...] = zeros
    wbim_ref[...] = zeros
    wcre_ref[...] = zeros
    wcim_ref[...] = zeros
    g_per_tile = MXU_TILE // P
    g_per_blk = MXU_TILE // H
    for g in range(G):
        lbre_ref[:, g * P:(g + 1) * P] = lb_re[g:g + 1, :]
        lbim_ref[:, g * P:(g + 1) * P] = lb_im[g:g + 1, :]
        dflat_ref[:, g * H:(g + 1) * H] = d[g:g + 1, :]
        n, gi = divmod(g, g_per_tile)
        r0 = (g % g_per_blk) * H
        c0 = gi * P
        wbre_ref[n, r0:r0 + H, c0:c0 + P] = bb_re[g].astype(BF16)
        wbim_ref[n, r0:r0 + H, c0:c0 + P] = bb_im[g].astype(BF16)
        wcre_ref[n, c0:c0 + P, r0:r0 + H] = c_re[g].astype(BF16)
        wcim_ref[n, c0:c0 + P, r0:r0 + H] = (-c_im[g]).astype(BF16)


def _ssm_prep(a_re, a_im, log_dt, b_re_t, b_im_t, c_re_t, c_im_t, d):
    G = N_SSM_GROUPS
    tile = jax.ShapeDtypeStruct((N_STATE_TILES, MXU_TILE, MXU_TILE), BF16)
    return pl.pallas_call(
        _ssm_prep_body,
        out_shape=(jax.ShapeDtypeStruct((1, N_STATE), F32), jax.ShapeDtypeStruct((1, N_STATE), F32),
                   jax.ShapeDtypeStruct((1, SSM_WIDTH), F32), tile, tile, tile, tile),
        name="ssm_prep",
    )(a_re, a_im, log_dt.reshape(G, 1), b_re_t, b_im_t, c_re_t, c_im_t, d)


def _adaln_body(cp_ref, cs_ref, w_ref, b_ref, op_ref, os_ref):
    w = w_ref[...].astype(BF16)
    b = b_ref[...]
    op_ref[...] = jnp.dot(jax.nn.silu(cp_ref[...]).astype(BF16), w, preferred_element_type=F32) + b
    os_ref[...] = jnp.dot(jax.nn.silu(cs_ref[...]).astype(BF16), w, preferred_element_type=F32) + b


def _adaln(c_p, c_s, w_ada, b_ada):
    n_p, n_s = c_p.shape[0], c_s.shape[0]
    tn = D_MODEL
    return pl.pallas_call(
        _adaln_body,
        grid=(N_MOD * D_MODEL // tn,),
        in_specs=[pl.BlockSpec((n_p, D_MODEL), lambda j: (0, 0)),
                  pl.BlockSpec((n_s, D_MODEL), lambda j: (0, 0)),
                  pl.BlockSpec((D_MODEL, tn), lambda j: (0, j)),
                  pl.BlockSpec((1, tn), lambda j: (0, j))],
        out_specs=(pl.BlockSpec((n_p, tn), lambda j: (0, j)), pl.BlockSpec((n_s, tn), lambda j: (0, j))),
        out_shape=(jax.ShapeDtypeStruct((n_p, N_MOD * D_MODEL), F32),
                   jax.ShapeDtypeStruct((n_s, N_MOD * D_MODEL), F32)),
        compiler_params=_cparams(("arbitrary",)),
        name="adaln",
    )(c_p, c_s, w_ada, b_ada.reshape(1, -1))


def _mixer_body(*refs, S, Tt, start_pos, seq_major, has_state):
    refs = list(refs)
    x_ref, sh_ref, sc_ref, g_ref, win_ref = refs[:5]
    k = 5
    if has_state:
        buf0_ref, hre0_ref, him0_ref = refs[k:k + 3]
        k += 3
    (poolw_ref, pscale_ref, lbre_ref, lbim_ref, wbre_ref, wbim_ref, wcre_ref, wcim_ref, d_ref, gluw_ref,
     glub_ref) = refs[k:k + 11]
    k += 11
    mix_ref, newbuf_ref, hre_out_ref, him_out_ref = refs[k:k + 4]
    z_scr, sre_scr, sim_scr, hre_scr, him_scr, winb_scr, glub_scr = refs[k + 4:]

    i = pl.program_id(0)
    R = Tt * S
    HR = HIST * S
    D = D_MODEL

    @pl.when(i == 0)
    def _init():
        z_scr[0:S, :] = jnp.zeros((S, POOL_WIDTH), F32)
        if has_state:
            for j in range(POOL_BUF):
                z_scr[(j + 1) * S:(j + 2) * S, :] = buf0_ref[:, j, :]
            hre_scr[...] = hre0_ref[...]
            him_scr[...] = him0_ref[...]
        else:
            z_scr[S:HR, :] = jnp.zeros((HR - S, POOL_WIDTH), F32)
            hre_scr[...] = jnp.zeros((S, N_STATE), F32)
            him_scr[...] = jnp.zeros((S, N_STATE), F32)
        winb_scr[...] = win_ref[...].astype(BF16)
        glub_scr[...] = gluw_ref[...].astype(BF16)

    g = g_ref[...]
    if seq_major:
        x3 = x_ref[...]
        h3 = _rms(x3, g) * (1.0 + sc_ref[...][:, None, :]) + sh_ref[...][:, None, :]
        u_nm = jnp.dot(h3.reshape(R, D).astype(BF16), winb_scr[...], preferred_element_type=F32)
        u = jnp.swapaxes(u_nm.reshape(S, Tt, D), 0, 1).reshape(R, D)
    else:
        x_tm = jnp.concatenate([x_ref[:, t, :] for t in range(Tt)], axis=0)
        h3 = _rms(x_tm, g).reshape(Tt, S, D) * (1.0 + sc_ref[...]) + sh_ref[...]
        u = jnp.dot(h3.reshape(R, D).astype(BF16), winb_scr[...], preferred_element_type=F32)

    up = u[:, 0:POOL_WIDTH]
    us = u[:, POOL_WIDTH:D]
    z_scr[HR:HR + R, :] = up

    row = lax.broadcasted_iota(jnp.int32, (R, 1), 0)
    pos = start_pos + i * Tt + lax.shift_right_logical(row, S.bit_length() - 1)
    outs = []
    for kk, w in enumerate(POOL_WINDOWS):
        lo, hi = kk * POOL_GROUP, (kk + 1) * POOL_GROUP
        cur = z_scr[:, lo:hi]
        step = 1
        while step < w:
            cur = cur[step * S:, :] + cur[:cur.shape[0] - step * S, :]
            step *= 2
        s = cur[cur.shape[0] - R:, :]
        cnt = jnp.minimum(w, pos + 1).astype(F32)
        pooled = s / cnt - up[:, lo:hi]
        mixed = jnp.dot(pooled.astype(BF16), poolw_ref[kk].astype(BF16), preferred_element_type=F32)
        outs.append(mixed * pscale_ref[:, lo:hi])

    for j in range(POOL_BUF):
        r0 = (Tt + 1 + j) * S
        newbuf_ref[:, j, :] = z_scr[r0:r0 + S, :]
    hist = z_scr[R:R + HR, :]
    z_scr[0:HR, :] = hist

    usb = us.astype(BF16)
    for n in range(N_STATE_TILES):
        kb = (n * MXU_TILE // SSM_STATE * SSM_GROUP) // MXU_TILE
        lhs = usb[:, kb * MXU_TILE:(kb + 1) * MXU_TILE]
        sre_scr[:, n * MXU_TILE:(n + 1) * MXU_TILE] = jnp.dot(lhs, wbre_ref[n], preferred_element_type=F32)
        sim_scr[:, n * MXU_TILE:(n + 1) * MXU_TILE] = jnp.dot(lhs, wbim_ref[n], preferred_element_type=F32)

    CB = 512
    n_tiles = S // SUBLANES
    for cb in range(N_STATE // CB):
        c0 = cb * CB
        lr = jnp.broadcast_to(lbre_ref[:, c0:c0 + CB], (SUBLANES, CB))
        li = jnp.broadcast_to(lbim_ref[:, c0:c0 + CB], (SUBLANES, CB))

        def tile_body(j, carry, c0=c0, lr=lr, li=li):
            s0 = pl.multiple_of(j * SUBLANES, SUBLANES)
            hr = hre_scr[pl.ds(s0, SUBLANES), c0:c0 + CB]
            hi_ = him_scr[pl.ds(s0, SUBLANES), c0:c0 + CB]

            def step_body(t, hc):
                hr, hi_ = hc
                r0 = pl.multiple_of(t * S + s0, SUBLANES)
                br = sre_scr[pl.ds(r0, SUBLANES), c0:c0 + CB]
                bi = sim_scr[pl.ds(r0, SUBLANES), c0:c0 + CB]
                nr = lr * hr - li * hi_ + br
                ni = lr * hi_ + li * hr + bi
                sre_scr[pl.ds(r0, SUBLANES), c0:c0 + CB] = nr
                sim_scr[pl.ds(r0, SUBLANES), c0:c0 + CB] = ni
                return nr, ni

            hr, hi_ = lax.fori_loop(0, Tt, step_body, (hr, hi_), unroll=8)
            hre_scr[pl.ds(s0, SUBLANES), c0:c0 + CB] = hr
            him_scr[pl.ds(s0, SUBLANES), c0:c0 + CB] = hi_
            return carry

        lax.fori_loop(0, n_tiles, tile_body, 0)

    hre_out_ref[...] = hre_scr[...]
    him_out_ref[...] = him_scr[...]

    n_ct = SSM_WIDTH // MXU_TILE
    k_per = N_STATE_TILES // n_ct
    ys = []
    for m in range(n_ct):
        acc = d_ref[:, m * MXU_TILE:(m + 1) * MXU_TILE] * us[:, m * MXU_TILE:(m + 1) * MXU_TILE]
        for kk in range(k_per):
            kt = m * k_per + kk
            acc = acc + jnp.dot(sre_scr[:, kt * MXU_TILE:(kt + 1) * MXU_TILE].astype(BF16), wcre_ref[kt],
                                preferred_element_type=F32)
            acc = acc + jnp.dot(sim_scr[:, kt * MXU_TILE:(kt + 1) * MXU_TILE].astype(BF16), wcim_ref[kt],
                                preferred_element_type=F32)
        ys.append(acc)
    y = jnp.concatenate(ys, axis=-1)
    gl = jax.nn.gelu(y)
    gate = jax.nn.sigmoid(jnp.dot(gl.astype(BF16), glub_scr[...], preferred_element_type=F32) + glub_ref[...])
    outs.append(gl * gate)

    mix_tm = jnp.concatenate(outs, axis=-1)
    if seq_major:
        mix_ref[...] = jnp.swapaxes(mix_tm.reshape(Tt, S, D), 0, 1).astype(mix_ref.dtype)
    else:
        mix_ref[...] = mix_tm.astype(mix_ref.dtype)


def _mixer(x, mod, g1, w_in, state, wts, *, Tt, start_pos, seq_major):
    S, L, D = x.shape
    R = Tt * S
    has_state = state is not None
    consts = [g1, w_in] + (list(state) if has_state else []) + list(wts)
    x_spec = pl.BlockSpec((S, Tt, D), lambda i: (0, i, 0))
    mod_specs = [pl.BlockSpec((S, D), lambda i: (0, 0)), pl.BlockSpec((S, D), lambda i: (0, 1))]
    if seq_major:
        mix_spec = pl.BlockSpec((S, Tt, D), lambda i: (0, i, 0))
        mix_shape = jax.ShapeDtypeStruct((S, L, D), BF16)
    else:
        assert Tt == L
        mix_spec = pl.BlockSpec((R, D), lambda i: (0, 0))
        mix_shape = jax.ShapeDtypeStruct((L * S, D), BF16)
    body = functools.partial(_mixer_body, S=S, Tt=Tt, start_pos=start_pos, seq_major=seq_major, has_state=has_state)
    return pl.pallas_call(
        body,
        grid=(L // Tt,),
        in_specs=[x_spec] + mod_specs + [_const_spec(a.shape) for a in consts],
        out_specs=(mix_spec,
                   pl.BlockSpec((S, POOL_BUF, POOL_WIDTH), lambda i: (0, 0, 0)),
                   pl.BlockSpec((S, N_STATE), lambda i: (0, 0)),
                   pl.BlockSpec((S, N_STATE), lambda i: (0, 0))),
        out_shape=(mix_shape,
                   jax.ShapeDtypeStruct((S, POOL_BUF, POOL_WIDTH), F32),
                   jax.ShapeDtypeStruct((S, N_STATE), F32),
                   jax.ShapeDtypeStruct((S, N_STATE), F32)),
        scratch_shapes=[pltpu.VMEM(((HIST + Tt) * S, POOL_WIDTH), F32),
                        pltpu.VMEM((R, N_STATE), F32),
                        pltpu.VMEM((R, N_STATE), F32),
                        pltpu.VMEM((S, N_STATE), F32),
                        pltpu.VMEM((S, N_STATE), F32),
                        pltpu.VMEM((D, D), BF16),
                        pltpu.VMEM((SSM_WIDTH, SSM_WIDTH), BF16)],
        compiler_params=_cparams(("arbitrary",)),
        name="mixer_S%d" % S,
    )(x, mod, mod, *consts)


def _split_bf16(v):
    hi = v.astype(BF16)
    lo = (v - hi.astype(F32)).astype(BF16)
    return hi, lo


def _route(h2, rw_ref, rb_ref):
    R = h2.shape[0]
    h2_hi, h2_lo = _split_bf16(h2)
    rw_hi, rw_lo = _split_bf16(rw_ref[...])
    logits = (jnp.dot(h2_hi, rw_hi, preferred_element_type=F32)
              + jnp.dot(h2_lo, rw_hi, preferred_element_type=F32)
              + jnp.dot(h2_hi, rw_lo, preferred_element_type=F32)) + rb_ref[...]
    lane = lax.broadcasted_iota(jnp.int32, (R, ROUTER_LANES), 1).astype(F32)
    ninf = jnp.float32(-jnp.inf)
    none = jnp.float32(ROUTER_LANES)
    is_g = lane < N_EXPERT_GROUPS
    l1 = jnp.where(is_g, logits, ninf)
    m1 = jnp.max(l1, axis=-1, keepdims=True)
    gidx = jnp.min(jnp.where(l1 == m1, lane, none), axis=-1, keepdims=True)
    p_top = 1.0 / jnp.sum(jnp.where(is_g, jnp.exp(logits - m1), 0.0), axis=-1, keepdims=True)
    e_lo = N_EXPERT_GROUPS + gidx * EXPERTS_PER_GROUP
    sel = (lane >= e_lo) & (lane < e_lo + EXPERTS_PER_GROUP)
    l2 = jnp.where(sel, logits, ninf)
    va = jnp.max(l2, axis=-1, keepdims=True)
    ia = jnp.min(jnp.where(l2 == va, lane, none), axis=-1, keepdims=True)
    l2b = jnp.where(lane == ia, ninf, l2)
    vb = jnp.max(l2b, axis=-1, keepdims=True)
    ib = jnp.min(jnp.where(l2b == vb, lane, none), axis=-1, keepdims=True)
    eb = jnp.exp(vb - va)
    den = 1.0 + eb
    gates = jnp.where(lane == ia, (1.0 / den) * p_top, 0.0) + jnp.where(lane == ib, (eb / den) * p_top, 0.0)
    return h2_hi, gidx, e_lo, gates, lane


def _stage3_rows(x, mix, g1, sh2, sc2, g2, n2g_ref, fng_ref, wout_ref, rw_ref, rb_ref, wg_ref, wu_ref, wd_ref):
    R = x.shape[0]
    mixo = jnp.dot(mix, wout_ref[...], preferred_element_type=F32)
    x1 = x + g1 * mixo
    h2 = _rms(x1, n2g_ref[...]) * (1.0 + sc2) + sh2
    h2_hi, _, _, gates, _ = _route(h2, rw_ref, rb_ref)

    acc = jnp.zeros((R, D_MODEL), F32)
    for e in range(N_EXPERTS):
        a = jnp.dot(h2_hi, wg_ref[e], preferred_element_type=F32)
        b = jnp.dot(h2_hi, wu_ref[e], preferred_element_type=F32)
        ge = gates[:, N_EXPERT_GROUPS + e:N_EXPERT_GROUPS + e + 1]
        hid = jax.nn.silu(a) * b * ge
        acc = acc + jnp.dot(hid.astype(BF16), wd_ref[e], preferred_element_type=F32)
    x2 = x1 + g2 * acc
    return _rms(x2, fng_ref[...])


MOE_CHUNK = 256
ROW_TILES = D_MODEL // LANES
PAY_ROWS = 2 * ROW_TILES
META_ROW = ROW_TILES
GROUP_LANE = EXPERTS_PER_GROUP
SLOT_LANE = EXPERTS_PER_GROUP + 1


def _route_body(x_ref, mix_ref, g1_ref, sh2_ref, sc2_ref, n2g_ref, wout_ref, rw_ref, rb_ref,
                x1_ref, pay_ref, meta_ref, cnt_ref, woutb_scr, tri_scr, run_scr, *, tc):
    n, c = pl.program_id(0), pl.program_id(1)
    step = n * pl.num_programs(1) + c

    @pl.when(step == 0)
    def _init():
        woutb_scr[...] = wout_ref[...].astype(BF16)
        r = lax.broadcasted_iota(jnp.int32, (tc, tc), 0)
        q = lax.broadcasted_iota(jnp.int32, (tc, tc), 1)
        tri_scr[...] = jnp.where(q < r, 1.0, 0.0).astype(BF16)
        run_scr[...] = jnp.zeros((1, ROUTER_LANES), F32)

    g1, sh2, sc2 = [r[pl.ds(n, 1), :] for r in (g1_ref, sh2_ref, sc2_ref)]
    mixo = jnp.dot(mix_ref[0], woutb_scr[...], preferred_element_type=F32)
    x1 = x_ref[0] + g1 * mixo
    x1_ref[0] = x1
    h2 = _rms(x1, n2g_ref[...]) * (1.0 + sc2) + sh2
    _, gidx, e_lo, gates, lane = _route(h2, rw_ref, rb_ref)

    onehot = jnp.where(lane == gidx, 1.0, 0.0)
    before = jnp.dot(tri_scr[...], onehot.astype(BF16), preferred_element_type=F32)
    run = run_scr[...]
    slot = jnp.sum(onehot * (before + run), axis=-1, keepdims=True)
    run = run + jnp.sum(onehot, axis=0, keepdims=True)
    run_scr[...] = run
    cnt_ref[...] = run.astype(jnp.int32)

    meta = jnp.where(lane == GROUP_LANE, gidx, 0.0) + jnp.where(lane == SLOT_LANE, slot, 0.0)
    for e in range(EXPERTS_PER_GROUP):
        ge = jnp.sum(jnp.where(lane == e_lo + e, gates, 0.0), axis=-1, keepdims=True)
        meta = meta + jnp.where(lane == e, ge, 0.0)
    meta_ref[...] = meta
    pay_ref[:, 0:ROW_TILES, :] = h2.reshape(tc, ROW_TILES, LANES)
    pay_ref[:, META_ROW, :] = meta
    pay_ref[:, META_ROW + 1:PAY_ROWS, :] = jnp.zeros((tc, PAY_ROWS - META_ROW - 1, LANES), F32)


def _route_call(x, mix, mod, n2g, w_out, rw, rb, tc):
    nb, L, D = x.shape
    n_tok = nb * L
    n_c = L // tc

    def mspec(k):
        return pl.BlockSpec((nb, D), lambda n, c, k=k: (0, k))

    consts = [n2g, w_out, rw, rb]
    return pl.pallas_call(
        functools.partial(_route_body, tc=tc),
        grid=(nb, n_c),
        in_specs=[pl.BlockSpec((1, tc, D), lambda n, c: (n, c, 0)),
                  pl.BlockSpec((1, tc, D), lambda n, c: (n, c, 0)),
                  mspec(2), mspec(3), mspec(4)] + [_const_spec(a.shape) for a in consts],
        out_specs=(pl.BlockSpec((1, tc, D), lambda n, c: (n, c, 0)),
                   pl.BlockSpec((tc, PAY_ROWS, LANES), lambda n, c: (n * n_c + c, 0, 0)),
                   pl.BlockSpec((tc, ROUTER_LANES), lambda n, c: (n * n_c + c, 0)),
                   pl.BlockSpec((1, ROUTER_LANES), lambda n, c: (0, 0))),
        out_shape=(jax.ShapeDtypeStruct((nb, L, D), F32),
                   jax.ShapeDtypeStruct((n_tok, PAY_ROWS, LANES), F32),
                   jax.ShapeDtypeStruct((n_tok, ROUTER_LANES), F32),
                   jax.ShapeDtypeStruct((1, ROUTER_LANES), jnp.int32)),
        scratch_shapes=[pltpu.VMEM((D, D), BF16),
                        pltpu.VMEM((tc, tc), BF16),
                        pltpu.VMEM((1, ROUTER_LANES), F32)],
        compiler_params=_cparams(("arbitrary", "arbitrary")),
        name="moe_route",
    )(x, mix, mod, mod, mod, *consts)


def _experts_body(src_ref, grp_ref, nch_ref, pay_hbm, wg_ref, wu_ref, wd_ref, moe_hbm,
                  wgb_scr, wub_scr, wdb_scr, xin_scr, out_scr, gsem, ssem, *, n_tok):
    s = pl.program_id(0)
    nch = nch_ref[0]
    slot_b = lax.rem(s, 2)

    def gather(chunk, b, i):
        t = jnp.minimum(src_ref[chunk * MOE_CHUNK + i], n_tok - 1)
        return pltpu.make_async_copy(pay_hbm.at[t], xin_scr.at[b, i], gsem.at[b])

    def scatter(chunk, b, i):
        return pltpu.make_async_copy(out_scr.at[b, i], moe_hbm.at[src_ref[chunk * MOE_CHUNK + i]], ssem.at[b])

    @pl.when(jnp.logical_and(s == 0, nch > 0))
    def _first_gather():
        for i in range(MOE_CHUNK):
            gather(0, 0, i).start()
        out_scr[1] = jnp.zeros((MOE_CHUNK, ROW_TILES, LANES), F32)
        spare = pltpu.make_async_copy(out_scr.at[1], moe_hbm.at[pl.ds(n_tok, MOE_CHUNK)], ssem.at[1])
        spare.start()
        spare.wait()

    @pl.when(s + 1 < nch)
    def _prefetch():
        for i in range(MOE_CHUNK):
            gather(s + 1, 1 - slot_b, i).start()

    fresh = jnp.logical_or(s == 0, grp_ref[s] != grp_ref[jnp.maximum(s - 1, 0)])

    @pl.when(jnp.logical_and(s < nch, fresh))
    def _load_group():
        wgb_scr[...] = wg_ref[...].astype(BF16)
        wub_scr[...] = wu_ref[...].astype(BF16)
        wdb_scr[...] = wd_ref[...].astype(BF16)

    @pl.when(s < nch)
    def _chunk():
        for i in range(MOE_CHUNK):
            gather(s, slot_b, i).wait()
        xin = xin_scr.at[slot_b]
        xb = xin[:, 0:ROW_TILES, :].reshape(MOE_CHUNK, D_MODEL).astype(BF16)
        meta = xin[:, META_ROW, :]
        acc = jnp.zeros((MOE_CHUNK, D_MODEL), F32)
        for e in range(EXPERTS_PER_GROUP):
            a = jnp.dot(xb, wgb_scr[e], preferred_element_type=F32)
            b = jnp.dot(xb, wub_scr[e], preferred_element_type=F32)
            hid = jax.nn.silu(a) * b * meta[:, e:e + 1]
            acc = acc + jnp.dot(hid.astype(BF16), wdb_scr[e], preferred_element_type=F32)

        @pl.when(s >= 2)
        def _reuse():
            for i in range(MOE_CHUNK):
                scatter(s, slot_b, i).wait()

        out_scr[slot_b] = acc.reshape(MOE_CHUNK, ROW_TILES, LANES)
        for i in range(MOE_CHUNK):
            scatter(s, slot_b, i).start()

    @pl.when(s == nch - 1)
    def _drain():
        for i in range(MOE_CHUNK):
            scatter(s, slot_b, i).wait()

        @pl.when(s >= 1)
        def _drain_prev():
            for i in range(MOE_CHUNK):
                scatter(s, 1 - slot_b, i).wait()


def _experts_call(src, grp, nch, pay, w_gate, w_up, w_down, n_tok):
    n_steps = grp.shape[0]
    E, D, F = w_gate.shape
    gsz = EXPERTS_PER_GROUP
    return pl.pallas_call(
        functools.partial(_experts_body, n_tok=n_tok),
        grid_spec=pltpu.PrefetchScalarGridSpec(
            num_scalar_prefetch=3,
            grid=(n_steps,),
            in_specs=[pl.BlockSpec(memory_space=pl.ANY),
                      pl.BlockSpec((gsz, D, F), lambda s, src, grp, nch: (grp[s], 0, 0)),
                      pl.BlockSpec((gsz, D, F), lambda s, src, grp, nch: (grp[s], 0, 0)),
                      pl.BlockSpec((gsz, F, D), lambda s, src, grp, nch: (grp[s], 0, 0))],
            out_specs=pl.BlockSpec(memory_space=pl.ANY),
            scratch_shapes=[pltpu.VMEM((gsz, D, F), BF16), pltpu.VMEM((gsz, D, F), BF16),
                            pltpu.VMEM((gsz, F, D), BF16),
                            pltpu.VMEM((2, MOE_CHUNK, PAY_ROWS, LANES), F32),
                            pltpu.VMEM((2, MOE_CHUNK, ROW_TILES, LANES), F32),
                            pltpu.SemaphoreType.DMA((2,)), pltpu.SemaphoreType.DMA((2,))]),
        out_shape=jax.ShapeDtypeStruct((n_tok + MOE_CHUNK, ROW_TILES, LANES), F32),
        compiler_params=_cparams(("arbitrary",)),
        name="moe_experts",
    )(src, grp, nch, pay, w_gate, w_up, w_down)


def _finish_body(x1_ref, moe_ref, g2_ref, fng_ref, y_ref):
    n = pl.program_id(0)
    moe = moe_ref[...].reshape(x1_ref.shape[1], D_MODEL)
    x2 = x1_ref[0] + g2_ref[pl.ds(n, 1), :] * moe
    y_ref[0] = _rms(x2, fng_ref[...])


def _finish_call(x1, moe, mod, fng, tc):
    nb, L, D = x1.shape
    n_c = L // tc
    return pl.pallas_call(
        _finish_body,
        grid=(nb, n_c),
        in_specs=[pl.BlockSpec((1, tc, D), lambda n, c: (n, c, 0)),
                  pl.BlockSpec((tc, ROW_TILES, LANES), lambda n, c: (n * n_c + c, 0, 0)),
                  pl.BlockSpec((nb, D), lambda n, c: (0, 5)),
                  _const_spec(fng.shape)],
        out_specs=pl.BlockSpec((1, tc, D), lambda n, c: (n, c, 0)),
        out_shape=jax.ShapeDtypeStruct((nb, L, D), F32),
        compiler_params=_cparams(("arbitrary", "arbitrary")),
        name="moe_finish",
    )(x1, moe, mod, fng)


def _sorted_order(meta, counts, n_tok):
    n_steps = n_tok // MOE_CHUNK + N_EXPERT_GROUPS
    n_rows = n_steps * MOE_CHUNK
    g = meta[:, GROUP_LANE].astype(jnp.int32)
    slot = meta[:, SLOT_LANE].astype(jnp.int32)
    nch_g = (counts + MOE_CHUNK - 1) // MOE_CHUNK
    ends = jnp.cumsum(nch_g)
    first_row = (ends - nch_g) * MOE_CHUNK
    pos = first_row[g] + slot
    filler = n_tok + jnp.arange(n_rows, dtype=jnp.int32) % MOE_CHUNK
    src = filler.at[pos].set(jnp.arange(n_tok, dtype=jnp.int32), mode="drop")
    total = ends[-1]
    s = jnp.clip(jnp.arange(n_steps, dtype=jnp.int32), 0, jnp.maximum(total - 1, 0))
    grp = jnp.minimum(jnp.sum((s[:, None] >= ends[None, :]).astype(jnp.int32), axis=1), N_EXPERT_GROUPS - 1)
    return src, grp.astype(jnp.int32), total.reshape(1).astype(jnp.int32)


def _stage3_prompt_sparse(x, mix, mod, n2g, fng, w_out, rw, rb, w_gate, w_up, w_down, tc):
    nb, L, D = x.shape
    n_tok = nb * L
    x1, pay, meta, cnt = _route_call(x, mix, mod, n2g, w_out, rw, rb, tc)
    src, grp, nch = _sorted_order(meta, cnt[0, :N_EXPERT_GROUPS], n_tok)
    moe = _experts_call(src, grp, nch, pay, w_gate, w_up, w_down, n_tok)
    return _finish_call(x1, moe, mod, fng, tc)


def _stage3_sample_body(x_ref, mix_ref, g1_ref, sh2_ref, sc2_ref, g2_ref, *rest, S, L):
    wrefs, y_ref = rest[:-1], rest[-1]
    x_tm = jnp.concatenate([x_ref[:, t, :] for t in range(L)], axis=0)

    def rows(r):
        return jnp.concatenate([r[...]] * L, axis=0)

    y = _stage3_rows(x_tm, mix_ref[...], rows(g1_ref), rows(sh2_ref), rows(sc2_ref), rows(g2_ref), *wrefs)
    for t in range(L):
        y_ref[:, t, :] = y[t * S:(t + 1) * S, :]


def _stage3_sample(x, mix_tm, mod, wts):
    ns, L, D = x.shape
    consts = list(wts)

    def mspec(k):
        return pl.BlockSpec((ns, D), lambda i, k=k: (0, k))

    return pl.pallas_call(
        functools.partial(_stage3_sample_body, S=ns, L=L),
        grid=(1,),
        in_specs=[pl.BlockSpec((ns, L, D), lambda i: (0, 0, 0)),
                  pl.BlockSpec((L * ns, D), lambda i: (0, 0)),
                  mspec(2), mspec(3), mspec(4), mspec(5)] + [_const_spec(a.shape) for a in consts],
        out_specs=pl.BlockSpec((ns, L, D), lambda i: (0, 0, 0)),
        out_shape=jax.ShapeDtypeStruct((ns, L, D), F32),
        compiler_params=_cparams(("arbitrary",)),
        name="stage3_sample",
    )(x, mix_tm, mod, mod, mod, mod, *consts)


def kernel(x_prompt, x_sample, c_prompt, c_sample, state_pool, state_ssm_re, state_ssm_im, w_ada, b_ada, norm1_g, w_in, pool_w, pool_scale, ssm_a_re, ssm_a_im, ssm_log_dt, ssm_b_re, ssm_b_im, ssm_c_re, ssm_c_im, ssm_d, glu_w, glu_b, w_out, norm2_g, router_w1, router_b1, router_w2, router_b2, exp_w_gate, exp_w_up, exp_w_down, final_norm_g):
    depth = w_ada.shape[0]
    assert depth == 1
    l = 0
    nb, L, D = x_prompt.shape
    ns, Ls, _ = x_sample.shape

    lb_re, lb_im, d_flat, wbre, wbim, wcre, wcim = _ssm_prep(
        ssm_a_re[l], ssm_a_im[l], ssm_log_dt[l],
        jnp.transpose(ssm_b_re[l], (0, 2, 1)), jnp.transpose(ssm_b_im[l], (0, 2, 1)),
        jnp.transpose(ssm_c_re[l], (0, 2, 1)), jnp.transpose(ssm_c_im[l], (0, 2, 1)), ssm_d[l])
    mix_wts = (pool_w[l], pool_scale[l].reshape(1, -1), lb_re, lb_im, wbre, wbim, wcre, wcim, d_flat,
               glu_w[l], glu_b[l].reshape(1, -1))

    rw = jnp.concatenate([router_w1[l], jnp.transpose(router_w2[l], (1, 0, 2)).reshape(D, N_EXPERTS)], axis=1)
    rw = jnp.pad(rw, ((0, 0), (0, ROUTER_LANES - rw.shape[1])))
    rb = jnp.concatenate([router_b1[l], router_b2[l].reshape(-1)])
    rb = jnp.pad(rb, (0, ROUTER_LANES - rb.shape[0])).reshape(1, -1)
    s3_wts = (norm2_g[l].reshape(1, -1), final_norm_g.reshape(1, -1), w_out[l].astype(BF16), rw, rb,
              exp_w_gate[l].astype(BF16), exp_w_up[l].astype(BF16), exp_w_down[l].astype(BF16))

    mod_p, mod_s = _adaln(c_prompt, c_sample, w_ada[l], b_ada[l])
    g1 = norm1_g[l].reshape(1, -1)

    mix_p, pool_p, hre_p, him_p = _mixer(x_prompt, mod_p, g1, w_in[l], None, mix_wts,
                                         Tt=64, start_pos=0, seq_major=True)
    y_p = _stage3_prompt_sparse(x_prompt, mix_p, mod_p, norm2_g[l].reshape(1, -1), final_norm_g.reshape(1, -1),
                                w_out[l], rw, rb, exp_w_gate[l], exp_w_up[l], exp_w_down[l], 512)

    state = (state_pool[l], state_ssm_re[l].reshape(ns, N_STATE), state_ssm_im[l].reshape(ns, N_STATE))
    mix_s, pool_s, hre_s, him_s = _mixer(x_sample, mod_s, g1, w_in[l], state, mix_wts,
                                         Tt=Ls, start_pos=PAST_LEN, seq_major=False)
    y_s = _stage3_sample(x_sample, mix_s, mod_s, s3_wts)

    def st(a, n):
        return a.reshape(1, n, N_SSM_GROUPS, SSM_STATE)

    return (y_p, y_s, pool_p[None], pool_s[None], st(hre_p, nb), st(him_p, nb), st(hre_s, ns), st(him_s, ns))
```

```python
import functools

import jax
import jax.numpy as jnp
from jax import lax
from jax.experimental import pallas as pl
from jax.experimental.pallas import tpu as pltpu

D_MODEL = 1024
POOL_WIDTH = 512
SSM_WIDTH = 512
POOL_WINDOWS = (2, 4, 8, 16)
POOL_GROUP = 128
POOL_BUF = 15
HIST = 16
SSM_GROUP = 16
N_SSM_GROUPS = 32
SSM_STATE = 64
N_STATE = N_SSM_GROUPS * SSM_STATE
N_EXPERT_GROUPS = 4
EXPERTS_PER_GROUP = 4
N_EXPERTS = 16
EXPERT_HIDDEN = 256
N_MOD = 6
EPS = 1e-6
PAST_LEN = 16384

MXU_TILE = 256
LANES = 128
SUBLANES = 8
ROUTER_LANES = 128
VMEM_LIMIT = 56 * 1024 * 1024
N_STATE_TILES = N_STATE // MXU_TILE

F32 = jnp.float32
BF16 = jnp.bfloat16


def _cparams(sem):
    return pltpu.CompilerParams(dimension_semantics=sem, vmem_limit_bytes=VMEM_LIMIT)


def _const_spec(shape):
    nd = len(shape)
    return pl.BlockSpec(shape, lambda *_: (0,) * nd, pipeline_mode=pl.Buffered(1))


def _rms(x, g):
    return x * lax.rsqrt(jnp.mean(x * x, axis=-1, keepdims=True) + EPS) * g


def _ssm_prep_body(are_ref, aim_ref, ldt_ref, bre_ref, bim_ref, cre_ref, cim_ref, d_ref,
                   lbre_ref, lbim_ref, dflat_ref, wbre_ref, wbim_ref, wcre_ref, wcim_ref):
    G, H, P = N_SSM_GROUPS, SSM_GROUP, SSM_STATE
    a_re = are_ref[...]
    a_im = aim_ref[...]
    dt = jnp.exp(ldt_ref[...])
    mag = jnp.exp(a_re * dt)
    lb_re = mag * jnp.cos(a_im * dt)
    lb_im = mag * jnp.sin(a_im * dt)
    den = a_re * a_re + a_im * a_im
    nr = lb_re - 1.0
    ni = lb_im
    k_re = (nr * a_re + ni * a_im) / den
    k_im = (ni * a_re - nr * a_im) / den
    b_re = bre_ref[...]
    b_im = bim_ref[...]
    bb_re = k_re[:, None, :] * b_re - k_im[:, None, :] * b_im
    bb_im = k_re[:, None, :] * b_im + k_im[:, None, :] * b_re
    c_re = cre_ref[...]
    c_im = cim_ref[...]
    d = d_ref[...]

    zeros = jnp.zeros((N_STATE_TILES, MXU_TILE, MXU_TILE), BF16)
    wbre_ref[...] = zeros
    wbim_ref[...] = zeros
    wcre_ref[...] = zeros
    wcim_ref[...] = zeros
    g_per_tile = MXU_TILE // P
    g_per_blk = MXU_TILE // H
    for g in range(G):
        lbre_ref[:, g * P:(g + 1) * P] = lb_re[g:g + 1, :]
        lbim_ref[:, g * P:(g + 1) * P] = lb_im[g:g + 1, :]
        dflat_ref[:, g * H:(g + 1) * H] = d[g:g + 1, :]
        n, gi = divmod(g, g_per_tile)
        r0 = (g % g_per_blk) * H
        c0 = gi * P
        wbre_ref[n, r0:r0 + H, c0:c0 + P] = bb_re[g].astype(BF16)
        wbim_ref[n, r0:r0 + H, c0:c0 + P] = bb_im[g].astype(BF16)
        wcre_ref[n, c0:c0 + P, r0:r0 + H] = c_re[g].astype(BF16)
        wcim_ref[n, c0:c0 + P, r0:r0 + H] = (-c_im[g]).astype(BF16)


def _ssm_prep(a_re, a_im, log_dt, b_re_t, b_im_t, c_re_t, c_im_t, d):
    G = N_SSM_GROUPS
    tile = jax.ShapeDtypeStruct((N_STATE_TILES, MXU_TILE, MXU_TILE), BF16)
    return pl.pallas_call(
        _ssm_prep_body,
        out_shape=(jax.ShapeDtypeStruct((1, N_STATE), F32), jax.ShapeDtypeStruct((1, N_STATE), F32),
                   jax.ShapeDtypeStruct((1, SSM_WIDTH), F32), tile, tile, tile, tile),
        name="ssm_prep",
    )(a_re, a_im, log_dt.reshape(G, 1), b_re_t, b_im_t, c_re_t, c_im_t, d)


def _adaln_body(cp_ref, cs_ref, w_ref, b_ref, op_ref, os_ref):
    w = w_ref[...].astype(BF16)
    b = b_ref[...]
    op_ref[...] = jnp.dot(jax.nn.silu(cp_ref[...]).astype(BF16), w, preferred_element_type=F32) + b
    os_ref[...] = jnp.dot(jax.nn.silu(cs_ref[...]).astype(BF16), w, preferred_element_type=F32) + b


def _adaln(c_p, c_s, w_ada, b_ada):
    n_p, n_s = c_p.shape[0], c_s.shape[0]
    tn = D_MODEL
    return pl.pallas_call(
        _adaln_body,
        grid=(N_MOD * D_MODEL // tn,),
        in_specs=[pl.BlockSpec((n_p, D_MODEL), lambda j: (0, 0)),
                  pl.BlockSpec((n_s, D_MODEL), lambda j: (0, 0)),
                  pl.BlockSpec((D_MODEL, tn), lambda j: (0, j)),
                  pl.BlockSpec((1, tn), lambda j: (0, j))],
        out_specs=(pl.BlockSpec((n_p, tn), lambda j: (0, j)), pl.BlockSpec((n_s, tn), lambda j: (0, j))),
        out_shape=(jax.ShapeDtypeStruct((n_p, N_MOD * D_MODEL), F32),
                   jax.ShapeDtypeStruct((n_s, N_MOD * D_MODEL), F32)),
        compiler_params=_cparams(("arbitrary",)),
        name="adaln",
    )(c_p, c_s, w_ada, b_ada.reshape(1, -1))


def _mixer_body(*refs, S, Tt, start_pos, seq_major, has_state):
    refs = list(refs)
    x_ref, sh_ref, sc_ref, g_ref, win_ref = refs[:5]
    k = 5
    if has_state:
        buf0_ref, hre0_ref, him0_ref = refs[k:k + 3]
        k += 3
    (poolw_ref, pscale_ref, lbre_ref, lbim_ref, wbre_ref, wbim_ref, wcre_ref, wcim_ref, d_ref, gluw_ref,
     glub_ref) = refs[k:k + 11]
    k += 11
    mix_ref, newbuf_ref, hre_out_ref, him_out_ref = refs[k:k + 4]
    z_scr, sre_scr, sim_scr, hre_scr, him_scr, winb_scr, glub_scr = refs[k + 4:]

    i = pl.program_id(0)
    R = Tt * S
    HR = HIST * S
    D = D_MODEL

    @pl.when(i == 0)
    def _init():
        z_scr[0:S, :] = jnp.zeros((S, POOL_WIDTH), F32)
        if has_state:
            for j in range(POOL_BUF):
                z_scr[(j + 1) * S:(j + 2) * S, :] = buf0_ref[:, j, :]
            hre_scr[...] = hre0_ref[...]
            him_scr[...] = him0_ref[...]
        else:
            z_scr[S:HR, :] = jnp.zeros((HR - S, POOL_WIDTH), F32)
            hre_scr[...] = jnp.zeros((S, N_STATE), F32)
            him_scr[...] = jnp.zeros((S, N_STATE), F32)
        winb_scr[...] = win_ref[...].astype(BF16)
        glub_scr[...] = gluw_ref[...].astype(BF16)

    g = g_ref[...]
    if seq_major:
        x3 = x_ref[...]
        h3 = _rms(x3, g) * (1.0 + sc_ref[...][:, None, :]) + sh_ref[...][:, None, :]
        u_nm = jnp.dot(h3.reshape(R, D).astype(BF16), winb_scr[...], preferred_element_type=F32)
        u = jnp.swapaxes(u_nm.reshape(S, Tt, D), 0, 1).reshape(R, D)
    else:
        x_tm = jnp.concatenate([x_ref[:, t, :] for t in range(Tt)], axis=0)
        h3 = _rms(x_tm, g).reshape(Tt, S, D) * (1.0 + sc_ref[...]) + sh_ref[...]
        u = jnp.dot(h3.reshape(R, D).astype(BF16), winb_scr[...], preferred_element_type=F32)

    up = u[:, 0:POOL_WIDTH]
    us = u[:, POOL_WIDTH:D]
    z_scr[HR:HR + R, :] = up

    row = lax.broadcasted_iota(jnp.int32, (R, 1), 0)
    pos = start_pos + i * Tt + lax.shift_right_logical(row, S.bit_length() - 1)
    outs = []
    for kk, w in enumerate(POOL_WINDOWS):
        lo, hi = kk * POOL_GROUP, (kk + 1) * POOL_GROUP
        cur = z_scr[:, lo:hi]
        step = 1
        while step < w:
            cur = cur[step * S:, :] + cur[:cur.shape[0] - step * S, :]
            step *= 2
        s = cur[cur.shape[0] - R:, :]
        cnt = jnp.minimum(w, pos + 1).astype(F32)
        pooled = s / cnt - up[:, lo:hi]
        mixed = jnp.dot(pooled.astype(BF16), poolw_ref[kk].astype(BF16), preferred_element_type=F32)
        outs.append(mixed * pscale_ref[:, lo:hi])

    for j in range(POOL_BUF):
        r0 = (Tt + 1 + j) * S
        newbuf_ref[:, j, :] = z_scr[r0:r0 + S, :]
    hist = z_scr[R:R + HR, :]
    z_scr[0:HR, :] = hist

    usb = us.astype(BF16)
    for n in range(N_STATE_TILES):
        kb = (n * MXU_TILE // SSM_STATE * SSM_GROUP) // MXU_TILE
        lhs = usb[:, kb * MXU_TILE:(kb + 1) * MXU_TILE]
        sre_scr[:, n * MXU_TILE:(n + 1) * MXU_TILE] = jnp.dot(lhs, wbre_ref[n], preferred_element_type=F32)
        sim_scr[:, n * MXU_TILE:(n + 1) * MXU_TILE] = jnp.dot(lhs, wbim_ref[n], preferred_element_type=F32)

    CB = 512
    n_tiles = S // SUBLANES
    for cb in range(N_STATE // CB):
        c0 = cb * CB
        lr = jnp.broadcast_to(lbre_ref[:, c0:c0 + CB], (SUBLANES, CB))
        li = jnp.broadcast_to(lbim_ref[:, c0:c0 + CB], (SUBLANES, CB))

        def tile_body(j, carry, c0=c0, lr=lr, li=li):
            s0 = pl.multiple_of(j * SUBLANES, SUBLANES)
            hr = hre_scr[pl.ds(s0, SUBLANES), c0:c0 + CB]
            hi_ = him_scr[pl.ds(s0, SUBLANES), c0:c0 + CB]

            def step_body(t, hc):
                hr, hi_ = hc
                r0 = pl.multiple_of(t * S + s0, SUBLANES)
                br = sre_scr[pl.ds(r0, SUBLANES), c0:c0 + CB]
                bi = sim_scr[pl.ds(r0, SUBLANES), c0:c0 + CB]
                nr = lr * hr - li * hi_ + br
                ni = lr * hi_ + li * hr + bi
                sre_scr[pl.ds(r0, SUBLANES), c0:c0 + CB] = nr
                sim_scr[pl.ds(r0, SUBLANES), c0:c0 + CB] = ni
                return nr, ni

            hr, hi_ = lax.fori_loop(0, Tt, step_body, (hr, hi_), unroll=8)
            hre_scr[pl.ds(s0, SUBLANES), c0:c0 + CB] = hr
            him_scr[pl.ds(s0, SUBLANES), c0:c0 + CB] = hi_
            return carry

        lax.fori_loop(0, n_tiles, tile_body, 0)

    hre_out_ref[...] = hre_scr[...]
    him_out_ref[...] = him_scr[...]

    n_ct = SSM_WIDTH // MXU_TILE
    k_per = N_STATE_TILES // n_ct
    ys = []
    for m in range(n_ct):
        acc = d_ref[:, m * MXU_TILE:(m + 1) * MXU_TILE] * us[:, m * MXU_TILE:(m + 1) * MXU_TILE]
        for kk in range(k_per):
            kt = m * k_per + kk
            acc = acc + jnp.dot(sre_scr[:, kt * MXU_TILE:(kt + 1) * MXU_TILE].astype(BF16), wcre_ref[kt],
                                preferred_element_type=F32)
            acc = acc + jnp.dot(sim_scr[:, kt * MXU_TILE:(kt + 1) * MXU_TILE].astype(BF16), wcim_ref[kt],
                                preferred_element_type=F32)
        ys.append(acc)
    y = jnp.concatenate(ys, axis=-1)
    gl = jax.nn.gelu(y)
    gate = jax.nn.sigmoid(jnp.dot(gl.astype(BF16), glub_scr[...], preferred_element_type=F32) + glub_ref[...])
    outs.append(gl * gate)

    mix_tm = jnp.concatenate(outs, axis=-1)
    if seq_major:
        mix_ref[...] = jnp.swapaxes(mix_tm.reshape(Tt, S, D), 0, 1).astype(mix_ref.dtype)
    else:
        mix_ref[...] = mix_tm.astype(mix_ref.dtype)


def _mixer(x, mod, g1, w_in, state, wts, *, Tt, start_pos, seq_major):
    S, L, D = x.shape
    R = Tt * S
    has_state = state is not None
    consts = [g1, w_in] + (list(state) if has_state else []) + list(wts)
    x_spec = pl.BlockSpec((S, Tt, D), lambda i: (0, i, 0))
    mod_specs = [pl.BlockSpec((S, D), lambda i: (0, 0)), pl.BlockSpec((S, D), lambda i: (0, 1))]
    if seq_major:
        mix_spec = pl.BlockSpec((S, Tt, D), lambda i: (0, i, 0))
        mix_shape = jax.ShapeDtypeStruct((S, L, D), BF16)
    else:
        assert Tt == L
        mix_spec = pl.BlockSpec((R, D), lambda i: (0, 0))
        mix_shape = jax.ShapeDtypeStruct((L * S, D), BF16)
    body = functools.partial(_mixer_body, S=S, Tt=Tt, start_pos=start_pos, seq_major=seq_major, has_state=has_state)
    return pl.pallas_call(
        body,
        grid=(L // Tt,),
        in_specs=[x_spec] + mod_specs + [_const_spec(a.shape) for a in consts],
        out_specs=(mix_spec,
                   pl.BlockSpec((S, POOL_BUF, POOL_WIDTH), lambda i: (0, 0, 0)),
                   pl.BlockSpec((S, N_STATE), lambda i: (0, 0)),
                   pl.BlockSpec((S, N_STATE), lambda i: (0, 0))),
        out_shape=(mix_shape,
                   jax.ShapeDtypeStruct((S, POOL_BUF, POOL_WIDTH), F32),
                   jax.ShapeDtypeStruct((S, N_STATE), F32),
                   jax.ShapeDtypeStruct((S, N_STATE), F32)),
        scratch_shapes=[pltpu.VMEM(((HIST + Tt) * S, POOL_WIDTH), F32),
                        pltpu.VMEM((R, N_STATE), F32),
                        pltpu.VMEM((R, N_STATE), F32),
                        pltpu.VMEM((S, N_STATE), F32),
                        pltpu.VMEM((S, N_STATE), F32),
                        pltpu.VMEM((D, D), BF16),
                        pltpu.VMEM((SSM_WIDTH, SSM_WIDTH), BF16)],
        compiler_params=_cparams(("arbitrary",)),
        name="mixer_S%d" % S,
    )(x, mod, mod, *consts)


def _split_bf16(v):
    hi = v.astype(BF16)
    lo = (v - hi.astype(F32)).astype(BF16)
    return hi, lo


def _route(h2, rw_ref, rb_ref):
    R = h2.shape[0]
    h2_hi, h2_lo = _split_bf16(h2)
    rw_hi, rw_lo = _split_bf16(rw_ref[...])
    logits = (jnp.dot(h2_hi, rw_hi, preferred_element_type=F32)
              + jnp.dot(h2_lo, rw_hi, preferred_element_type=F32)
              + jnp.dot(h2_hi, rw_lo, preferred_element_type=F32)) + rb_ref[...]
    lane = lax.broadcasted_iota(jnp.int32, (R, ROUTER_LANES), 1).astype(F32)
    ninf = jnp.float32(-jnp.inf)
    none = jnp.float32(ROUTER_LANES)
    is_g = lane < N_EXPERT_GROUPS
    l1 = jnp.where(is_g, logits, ninf)
    m1 = jnp.max(l1, axis=-1, keepdims=True)
    gidx = jnp.min(jnp.where(l1 == m1, lane, none), axis=-1, keepdims=True)
    p_top = 1.0 / jnp.sum(jnp.where(is_g, jnp.exp(logits - m1), 0.0), axis=-1, keepdims=True)
    e_lo = N_EXPERT_GROUPS + gidx * EXPERTS_PER_GROUP
    sel = (lane >= e_lo) & (lane < e_lo + EXPERTS_PER_GROUP)
    l2 = jnp.where(sel, logits, ninf)
    va = jnp.max(l2, axis=-1, keepdims=True)
    ia = jnp.min(jnp.where(l2 == va, lane, none), axis=-1, keepdims=True)
    l2b = jnp.where(lane == ia, ninf, l2)
    vb = jnp.max(l2b, axis=-1, keepdims=True)
    ib = jnp.min(jnp.where(l2b == vb, lane, none), axis=-1, keepdims=True)
    eb = jnp.exp(vb - va)
    den = 1.0 + eb
    gates = jnp.where(lane == ia, (1.0 / den) * p_top, 0.0) + jnp.where(lane == ib, (eb / den) * p_top, 0.0)
    return h2_hi, gidx, e_lo, gates, lane


def _stage3_rows(x, mix, g1, sh2, sc2, g2, n2g_ref, fng_ref, wout_ref, rw_ref, rb_ref, wg_ref, wu_ref, wd_ref):
    R = x.shape[0]
    mixo = jnp.dot(mix, wout_ref[...], preferred_element_type=F32)
    x1 = x + g1 * mixo
    h2 = _rms(x1, n2g_ref[...]) * (1.0 + sc2) + sh2
    h2_hi, _, _, gates, _ = _route(h2, rw_ref, rb_ref)

    acc = jnp.zeros((R, D_MODEL), F32)
    for e in range(N_EXPERTS):
        a = jnp.dot(h2_hi, wg_ref[e], preferred_element_type=F32)
        b = jnp.dot(h2_hi, wu_ref[e], preferred_element_type=F32)
        ge = gates[:, N_EXPERT_GROUPS + e:N_EXPERT_GROUPS + e + 1]
        hid = jax.nn.silu(a) * b * ge
        acc = acc + jnp.dot(hid.astype(BF16), wd_ref[e], preferred_element_type=F32)
    x2 = x1 + g2 * acc
    return _rms(x2, fng_ref[...])


MOE_CHUNK = 256
ROW_TILES = D_MODEL // LANES
PAY_ROWS = 2 * ROW_TILES
META_ROW = ROW_TILES
GROUP_LANE = EXPERTS_PER_GROUP
SLOT_LANE = EXPERTS_PER_GROUP + 1


def _route_body(x_ref, mix_ref, g1_ref, sh2_ref, sc2_ref, n2g_ref, wout_ref, rw_ref, rb_ref,
                x1_ref, pay_ref, meta_ref, cnt_ref, woutb_scr, tri_scr, run_scr, *, tc):
    n, c = pl.program_id(0), pl.program_id(1)
    step = n * pl.num_programs(1) + c

    @pl.when(step == 0)
    def _init():
        woutb_scr[...] = wout_ref[...].astype(BF16)
        r = lax.broadcasted_iota(jnp.int32, (tc, tc), 0)
        q = lax.broadcasted_iota(jnp.int32, (tc, tc), 1)
        tri_scr[...] = jnp.where(q < r, 1.0, 0.0).astype(BF16)
        run_scr[...] = jnp.zeros((1, ROUTER_LANES), F32)

    g1, sh2, sc2 = [r[pl.ds(n, 1), :] for r in (g1_ref, sh2_ref, sc2_ref)]
    mixo = jnp.dot(mix_ref[0], woutb_scr[...], preferred_element_type=F32)
    x1 = x_ref[0] + g1 * mixo
    x1_ref[0] = x1
    h2 = _rms(x1, n2g_ref[...]) * (1.0 + sc2) + sh2
    _, gidx, e_lo, gates, lane = _route(h2, rw_ref, rb_ref)

    onehot = jnp.where(lane == gidx, 1.0, 0.0)
    before = jnp.dot(tri_scr[...], onehot.astype(BF16), preferred_element_type=F32)
    run = run_scr[...]
    slot = jnp.sum(onehot * (before + run), axis=-1, keepdims=True)
    run = run + jnp.sum(onehot, axis=0, keepdims=True)
    run_scr[...] = run
    cnt_ref[...] = run.astype(jnp.int32)

    meta = jnp.where(lane == GROUP_LANE, gidx, 0.0) + jnp.where(lane == SLOT_LANE, slot, 0.0)
    for e in range(EXPERTS_PER_GROUP):
        ge = jnp.sum(jnp.where(lane == e_lo + e, gates, 0.0), axis=-1, keepdims=True)
        meta = meta + jnp.where(lane == e, ge, 0.0)
    meta_ref[...] = meta
    pay_ref[:, 0:ROW_TILES, :] = h2.reshape(tc, ROW_TILES, LANES)
    pay_ref[:, META_ROW, :] = meta
    pay_ref[:, META_ROW + 1:PAY_ROWS, :] = jnp.zeros((tc, PAY_ROWS - META_ROW - 1, LANES), F32)


def _route_call(x, mix, mod, n2g, w_out, rw, rb, tc):
    nb, L, D = x.shape
    n_tok = nb * L
    n_c = L // tc

    def mspec(k):
        return pl.BlockSpec((nb, D), lambda n, c, k=k: (0, k))

    consts = [n2g, w_out, rw, rb]
    return pl.pallas_call(
        functools.partial(_route_body, tc=tc),
        grid=(nb, n_c),
        in_specs=[pl.BlockSpec((1, tc, D), lambda n, c: (n, c, 0)),
                  pl.BlockSpec((1, tc, D), lambda n, c: (n, c, 0)),
                  mspec(2), mspec(3), mspec(4)] + [_const_spec(a.shape) for a in consts],
        out_specs=(pl.BlockSpec((1, tc, D), lambda n, c: (n, c, 0)),
                   pl.BlockSpec((tc, PAY_ROWS, LANES), lambda n, c: (n * n_c + c, 0, 0)),
                   pl.BlockSpec((tc, ROUTER_LANES), lambda n, c: (n * n_c + c, 0)),
                   pl.BlockSpec((1, ROUTER_LANES), lambda n, c: (0, 0))),
        out_shape=(jax.ShapeDtypeStruct((nb, L, D), F32),
                   jax.ShapeDtypeStruct((n_tok, PAY_ROWS, LANES), F32),
                   jax.ShapeDtypeStruct((n_tok, ROUTER_LANES), F32),
                   jax.ShapeDtypeStruct((1, ROUTER_LANES), jnp.int32)),
        scratch_shapes=[pltpu.VMEM((D, D), BF16),
                        pltpu.VMEM((tc, tc), BF16),
                        pltpu.VMEM((1, ROUTER_LANES), F32)],
        compiler_params=_cparams(("arbitrary", "arbitrary")),
        name="moe_route",
    )(x, mix, mod, mod, mod, *consts)


def _experts_body(src_ref, grp_ref, nch_ref, pay_hbm, wg_ref, wu_ref, wd_ref, moe_hbm,
                  wgb_scr, wub_scr, wdb_scr, xin_scr, out_scr, gsem, ssem, *, n_tok):
    s = pl.program_id(0)
    nch = nch_ref[0]
    slot_b = lax.rem(s, 2)

    def gather(chunk, b, i):
        t = jnp.minimum(src_ref[chunk * MOE_CHUNK + i], n_tok - 1)
        return pltpu.make_async_copy(pay_hbm.at[t], xin_scr.at[b, i], gsem.at[b])

    def scatter(chunk, b, i):
        return pltpu.make_async_copy(out_scr.at[b, i], moe_hbm.at[src_ref[chunk * MOE_CHUNK + i]], ssem.at[b])

    @pl.when(jnp.logical_and(s == 0, nch > 0))
    def _first_gather():
        def start_row(i, carry):
            gather(0, 0, i).start()
            return carry

        lax.fori_loop(0, MOE_CHUNK, start_row, 0, unroll=8)
        out_scr[1] = jnp.zeros((MOE_CHUNK, ROW_TILES, LANES), F32)
        spare = pltpu.make_async_copy(out_scr.at[1], moe_hbm.at[pl.ds(n_tok, MOE_CHUNK)], ssem.at[1])
        spare.start()
        spare.wait()

    fresh = jnp.logical_or(s == 0, grp_ref[s] != grp_ref[jnp.maximum(s - 1, 0)])

    @pl.when(jnp.logical_and(s < nch, fresh))
    def _load_group():
        wgb_scr[...] = wg_ref[...].astype(BF16)
        wub_scr[...] = wu_ref[...].astype(BF16)
        wdb_scr[...] = wd_ref[...].astype(BF16)

    @pl.when(s < nch)
    def _chunk():
        for i in range(MOE_CHUNK):
            gather(s, slot_b, i).wait()
        nxt = jnp.minimum(s + 1, nch - 1)
        for i in range(MOE_CHUNK):
            gather(nxt, 1 - slot_b, i).start()
        xin = xin_scr.at[slot_b]
        xb = xin[:, 0:ROW_TILES, :].reshape(MOE_CHUNK, D_MODEL).astype(BF16)
        meta = xin[:, META_ROW, :]
        acc = jnp.zeros((MOE_CHUNK, D_MODEL), F32)
        for e in range(EXPERTS_PER_GROUP):
            a = jnp.dot(xb, wgb_scr[e], preferred_element_type=F32)
            b = jnp.dot(xb, wub_scr[e], preferred_element_type=F32)
            hid = jax.nn.silu(a) * b * meta[:, e:e + 1]
            acc = acc + jnp.dot(hid.astype(BF16), wdb_scr[e], preferred_element_type=F32)

        @pl.when(s >= 2)
        def _reuse():
            for i in range(MOE_CHUNK):
                scatter(s, slot_b, i).wait()

        out_scr[slot_b] = acc.reshape(MOE_CHUNK, ROW_TILES, LANES)
        for i in range(MOE_CHUNK):
            scatter(s, slot_b, i).start()

    @pl.when(s == nch - 1)
    def _drain():
        for i in range(MOE_CHUNK):
            gather(s, 1 - slot_b, i).wait()
        for i in range(MOE_CHUNK):
            scatter(s, slot_b, i).wait()

        @pl.when(s >= 1)
        def _drain_prev():
            for i in range(MOE_CHUNK):
                scatter(s, 1 - slot_b, i).wait()


def _experts_call(src, grp, nch, pay, w_gate, w_up, w_down, n_tok):
    n_steps = grp.shape[0]
    E, D, F = w_gate.shape
    gsz = EXPERTS_PER_GROUP
    return pl.pallas_call(
        functools.partial(_experts_body, n_tok=n_tok),
        grid_spec=pltpu.PrefetchScalarGridSpec(
            num_scalar_prefetch=3,
            grid=(n_steps,),
            in_specs=[pl.BlockSpec(memory_space=pl.ANY),
                      pl.BlockSpec((gsz, D, F), lambda s, src, grp, nch: (grp[s], 0, 0)),
                      pl.BlockSpec((gsz, D, F), lambda s, src, grp, nch: (grp[s], 0, 0)),
                      pl.BlockSpec((gsz, F, D), lambda s, src, grp, nch: (grp[s], 0, 0))],
            out_specs=pl.BlockSpec(memory_space=pl.ANY),
            scratch_shapes=[pltpu.VMEM((gsz, D, F), BF16), pltpu.VMEM((gsz, D, F), BF16),
                            pltpu.VMEM((gsz, F, D), BF16),
                            pltpu.VMEM((2, MOE_CHUNK, PAY_ROWS, LANES), F32),
                            pltpu.VMEM((2, MOE_CHUNK, ROW_TILES, LANES), F32),
                            pltpu.SemaphoreType.DMA((2,)), pltpu.SemaphoreType.DMA((2,))]),
        out_shape=jax.ShapeDtypeStruct((n_tok + MOE_CHUNK, ROW_TILES, LANES), F32),
        compiler_params=_cparams(("arbitrary",)),
        name="moe_experts",
    )(src, grp, nch, pay, w_gate, w_up, w_down)


def _finish_body(x1_ref, moe_ref, g2_ref, fng_ref, y_ref):
    n = pl.program_id(0)
    moe = moe_ref[...].reshape(x1_ref.shape[1], D_MODEL)
    x2 = x1_ref[0] + g2_ref[pl.ds(n, 1), :] * moe
    y_ref[0] = _rms(x2, fng_ref[...])


def _finish_call(x1, moe, mod, fng, tc):
    nb, L, D = x1.shape
    n_c = L // tc
    return pl.pallas_call(
        _finish_body,
        grid=(nb, n_c),
        in_specs=[pl.BlockSpec((1, tc, D), lambda n, c: (n, c, 0)),
                  pl.BlockSpec((tc, ROW_TILES, LANES), lambda n, c: (n * n_c + c, 0, 0)),
                  pl.BlockSpec((nb, D), lambda n, c: (0, 5)),
                  _const_spec(fng.shape)],
        out_specs=pl.BlockSpec((1, tc, D), lambda n, c: (n, c, 0)),
        out_shape=jax.ShapeDtypeStruct((nb, L, D), F32),
        compiler_params=_cparams(("arbitrary", "arbitrary")),
        name="moe_finish",
    )(x1, moe, mod, fng)


def _invert_body(pos_ref, src_ref, *, n_tok, n_rows):
    def fill(j, carry):
        src_ref[j] = n_tok + lax.rem(j, MOE_CHUNK)
        return carry

    lax.fori_loop(0, n_rows, fill, 0, unroll=8)

    def put(i, carry):
        src_ref[pos_ref[i]] = i
        return carry

    lax.fori_loop(0, n_tok, put, 0, unroll=8)


def _invert(pos, n_rows):
    n_tok = pos.shape[0]
    return pl.pallas_call(
        functools.partial(_invert_body, n_tok=n_tok, n_rows=n_rows),
        in_specs=[pl.BlockSpec(memory_space=pltpu.SMEM)],
        out_specs=pl.BlockSpec(memory_space=pltpu.SMEM),
        out_shape=jax.ShapeDtypeStruct((n_rows,), jnp.int32),
        name="moe_invert",
    )(pos)


def _sorted_order(meta, counts, n_tok):
    n_steps = n_tok // MOE_CHUNK + N_EXPERT_GROUPS
    n_rows = n_steps * MOE_CHUNK
    g = meta[:, GROUP_LANE].astype(jnp.int32)
    slot = meta[:, SLOT_LANE].astype(jnp.int32)
    nch_g = (counts + MOE_CHUNK - 1) // MOE_CHUNK
    ends = jnp.cumsum(nch_g)
    first_row = (ends - nch_g) * MOE_CHUNK
    base = sum(jnp.where(g == k, first_row[k], 0) for k in range(N_EXPERT_GROUPS))
    pos = jnp.clip(base + slot, 0, n_rows - 1)
    src = _invert(pos, n_rows)
    total = ends[-1]
    s = jnp.clip(jnp.arange(n_steps, dtype=jnp.int32), 0, jnp.maximum(total - 1, 0))
    grp = jnp.minimum(jnp.sum((s[:, None] >= ends[None, :]).astype(jnp.int32), axis=1), N_EXPERT_GROUPS - 1)
    return src, grp.astype(jnp.int32), total.reshape(1).astype(jnp.int32)


def _stage3_prompt_sparse(x, mix, mod, n2g, fng, w_out, rw, rb, w_gate, w_up, w_down, tc):
    nb, L, D = x.shape
    n_tok = nb * L
    x1, pay, meta, cnt = _route_call(x, mix, mod, n2g, w_out, rw, rb, tc)
    src, grp, nch = _sorted_order(meta, cnt[0, :N_EXPERT_GROUPS], n_tok)
    moe = _experts_call(src, grp, nch, pay, w_gate, w_up, w_down, n_tok)
    return _finish_call(x1, moe, mod, fng, tc)


def _stage3_sample_body(x_ref, mix_ref, g1_ref, sh2_ref, sc2_ref, g2_ref, *rest, S, L):
    wrefs, y_ref = rest[:-1], rest[-1]
    x_tm = jnp.concatenate([x_ref[:, t, :] for t in range(L)], axis=0)

    def rows(r):
        return jnp.concatenate([r[...]] * L, axis=0)

    y = _stage3_rows(x_tm, mix_ref[...], rows(g1_ref), rows(sh2_ref), rows(sc2_ref), rows(g2_ref), *wrefs)
    for t in range(L):
        y_ref[:, t, :] = y[t * S:(t + 1) * S, :]


def _stage3_sample(x, mix_tm, mod, wts):
    ns, L, D = x.shape
    consts = list(wts)

    def mspec(k):
        return pl.BlockSpec((ns, D), lambda i, k=k: (0, k))

    return pl.pallas_call(
        functools.partial(_stage3_sample_body, S=ns, L=L),
        grid=(1,),
        in_specs=[pl.BlockSpec((ns, L, D), lambda i: (0, 0, 0)),
                  pl.BlockSpec((L * ns, D), lambda i: (0, 0)),
                  mspec(2), mspec(3), mspec(4), mspec(5)] + [_const_spec(a.shape) for a in consts],
        out_specs=pl.BlockSpec((ns, L, D), lambda i: (0, 0, 0)),
        out_shape=jax.ShapeDtypeStruct((ns, L, D), F32),
        compiler_params=_cparams(("arbitrary",)),
        name="stage3_sample",
    )(x, mix_tm, mod, mod, mod, mod, *consts)


def kernel(x_prompt, x_sample, c_prompt, c_sample, state_pool, state_ssm_re, state_ssm_im, w_ada, b_ada, norm1_g, w_in, pool_w, pool_scale, ssm_a_re, ssm_a_im, ssm_log_dt, ssm_b_re, ssm_b_im, ssm_c_re, ssm_c_im, ssm_d, glu_w, glu_b, w_out, norm2_g, router_w1, router_b1, router_w2, router_b2, exp_w_gate, exp_w_up, exp_w_down, final_norm_g):
    depth = w_ada.shape[0]
    assert depth == 1
    l = 0
    nb, L, D = x_prompt.shape
    ns, Ls, _ = x_sample.shape

    lb_re, lb_im, d_flat, wbre, wbim, wcre, wcim = _ssm_prep(
        ssm_a_re[l], ssm_a_im[l], ssm_log_dt[l],
        jnp.transpose(ssm_b_re[l], (0, 2, 1)), jnp.transpose(ssm_b_im[l], (0, 2, 1)),
        jnp.transpose(ssm_c_re[l], (0, 2, 1)), jnp.transpose(ssm_c_im[l], (0, 2, 1)), ssm_d[l])
    mix_wts = (pool_w[l], pool_scale[l].reshape(1, -1), lb_re, lb_im, wbre, wbim, wcre, wcim, d_flat,
               glu_w[l], glu_b[l].reshape(1, -1))

    rw = jnp.concatenate([router_w1[l], jnp.transpose(router_w2[l], (1, 0, 2)).reshape(D, N_EXPERTS)], axis=1)
    rw = jnp.pad(rw, ((0, 0), (0, ROUTER_LANES - rw.shape[1])))
    rb = jnp.concatenate([router_b1[l], router_b2[l].reshape(-1)])
    rb = jnp.pad(rb, (0, ROUTER_LANES - rb.shape[0])).reshape(1, -1)
    s3_wts = (norm2_g[l].reshape(1, -1), final_norm_g.reshape(1, -1), w_out[l].astype(BF16), rw, rb,
              exp_w_gate[l].astype(BF16), exp_w_up[l].astype(BF16), exp_w_down[l].astype(BF16))

    mod_p, mod_s = _adaln(c_prompt, c_sample, w_ada[l], b_ada[l])
    g1 = norm1_g[l].reshape(1, -1)

    mix_p, pool_p, hre_p, him_p = _mixer(x_prompt, mod_p, g1, w_in[l], None, mix_wts,
                                         Tt=64, start_pos=0, seq_major=True)
    y_p = _stage3_prompt_sparse(x_prompt, mix_p, mod_p, norm2_g[l].reshape(1, -1), final_norm_g.reshape(1, -1),
                                w_out[l], rw, rb, exp_w_gate[l], exp_w_up[l], exp_w_down[l], 512)

    state = (state_pool[l], state_ssm_re[l].reshape(ns, N_STATE), state_ssm_im[l].reshape(ns, N_STATE))
    mix_s, pool_s, hre_s, him_s = _mixer(x_sample, mod_s, g1, w_in[l], state, mix_wts,
                                         Tt=Ls, start_pos=PAST_LEN, seq_major=False)
    y_s = _stage3_sample(x_sample, mix_s, mod_s, s3_wts)

    def st(a, n):
        return a.reshape(1, n, N_SSM_GROUPS, SSM_STATE)

    return (y_p, y_s, pool_p[None], pool_s[None], st(hre_p, nb), st(him_p, nb), st(hre_s, ns), st(him_s, ns))
```

```python
import functools

import jax
import jax.numpy as jnp
from jax import lax
from jax.experimental import pallas as pl
from jax.experimental.pallas import tpu as pltpu

D_MODEL = 1024
POOL_WIDTH = 512
SSM_WIDTH = 512
POOL_WINDOWS = (2, 4, 8, 16)
POOL_GROUP = 128
POOL_BUF = 15
HIST = 16
SSM_GROUP = 16
N_SSM_GROUPS = 32
SSM_STATE = 64
N_STATE = N_SSM_GROUPS * SSM_STATE
N_EXPERT_GROUPS = 4
EXPERTS_PER_GROUP = 4
N_EXPERTS = 16
EXPERT_HIDDEN = 256
N_MOD = 6
EPS = 1e-6
PAST_LEN = 16384

MXU_TILE = 256
LANES = 128
SUBLANES = 8
ROUTER_LANES = 128
VMEM_LIMIT = 60 * 1024 * 1024
N_STATE_TILES = N_STATE // MXU_TILE

F32 = jnp.float32
BF16 = jnp.bfloat16


def _cparams(sem):
    return pltpu.CompilerParams(dimension_semantics=sem, vmem_limit_bytes=VMEM_LIMIT)


def _const_spec(shape):
    nd = len(shape)
    return pl.BlockSpec(shape, lambda *_: (0,) * nd, pipeline_mode=pl.Buffered(1))


def _rms(x, g):
    return x * lax.rsqrt(jnp.mean(x * x, axis=-1, keepdims=True) + EPS) * g


def _ssm_prep_body(are_ref, aim_ref, ldt_ref, bre_ref, bim_ref, cre_ref, cim_ref, d_ref,
                   lbre_ref, lbim_ref, dflat_ref, wbre_ref, wbim_ref, wcre_ref, wcim_ref):
    G, H, P = N_SSM_GROUPS, SSM_GROUP, SSM_STATE
    a_re = are_ref[...]
    a_im = aim_ref[...]
    dt = jnp.exp(ldt_ref[...])
    mag = jnp.exp(a_re * dt)
    lb_re = mag * jnp.cos(a_im * dt)
    lb_im = mag * jnp.sin(a_im * dt)
    den = a_re * a_re + a_im * a_im
    nr = lb_re - 1.0
    ni = lb_im
    k_re = (nr * a_re + ni * a_im) / den
    k_im = (ni * a_re - nr * a_im) / den
    b_re = bre_ref[...]
    b_im = bim_ref[...]
    bb_re = k_re[:, None, :] * b_re - k_im[:, None, :] * b_im
    bb_im = k_re[:, None, :] * b_im + k_im[:, None, :] * b_re
    c_re = cre_ref[...]
    c_im = cim_ref[...]
    d = d_ref[...]

    zeros = jnp.zeros((N_STATE_TILES, MXU_TILE, MXU_TILE), BF16)
    wbre_ref[...] = zeros
    wbim_ref[...] = zeros
    wcre_ref[...] = zeros
    wcim_ref[...] = zeros
    g_per_tile = MXU_TILE // P
    g_per_blk = MXU_TILE // H
    for g in range(G):
        lbre_ref[:, g * P:(g + 1) * P] = lb_re[g:g + 1, :]
        lbim_ref[:, g * P:(g + 1) * P] = lb_im[g:g + 1, :]
        dflat_ref[:, g * H:(g + 1) * H] = d[g:g + 1, :]
        n, gi = divmod(g, g_per_tile)
        r0 = (g % g_per_blk) * H
        c0 = gi * P
        wbre_ref[n, r0:r0 + H, c0:c0 + P] = bb_re[g].astype(BF16)
        wbim_ref[n, r0:r0 + H, c0:c0 + P] = bb_im[g].astype(BF16)
        wcre_ref[n, c0:c0 + P, r0:r0 + H] = c_re[g].astype(BF16)
        wcim_ref[n, c0:c0 + P, r0:r0 + H] = (-c_im[g]).astype(BF16)


def _ssm_prep(a_re, a_im, log_dt, b_re_t, b_im_t, c_re_t, c_im_t, d):
    G = N_SSM_GROUPS
    tile = jax.ShapeDtypeStruct((N_STATE_TILES, MXU_TILE, MXU_TILE), BF16)
    return pl.pallas_call(
        _ssm_prep_body,
        out_shape=(jax.ShapeDtypeStruct((1, N_STATE), F32), jax.ShapeDtypeStruct((1, N_STATE), F32),
                   jax.ShapeDtypeStruct((1, SSM_WIDTH), F32), tile, tile, tile, tile),
        name="ssm_prep",
    )(a_re, a_im, log_dt.reshape(G, 1), b_re_t, b_im_t, c_re_t, c_im_t, d)


def _adaln_body(cp_ref, cs_ref, w_ref, b_ref, op_ref, os_ref):
    w = w_ref[...].astype(BF16)
    b = b_ref[...]
    op_ref[...] = jnp.dot(jax.nn.silu(cp_ref[...]).astype(BF16), w, preferred_element_type=F32) + b
    os_ref[...] = jnp.dot(jax.nn.silu(cs_ref[...]).astype(BF16), w, preferred_element_type=F32) + b


def _adaln(c_p, c_s, w_ada, b_ada):
    n_p, n_s = c_p.shape[0], c_s.shape[0]
    tn = D_MODEL
    return pl.pallas_call(
        _adaln_body,
        grid=(N_MOD * D_MODEL // tn,),
        in_specs=[pl.BlockSpec((n_p, D_MODEL), lambda j: (0, 0)),
                  pl.BlockSpec((n_s, D_MODEL), lambda j: (0, 0)),
                  pl.BlockSpec((D_MODEL, tn), lambda j: (0, j)),
                  pl.BlockSpec((1, tn), lambda j: (0, j))],
        out_specs=(pl.BlockSpec((n_p, tn), lambda j: (0, j)), pl.BlockSpec((n_s, tn), lambda j: (0, j))),
        out_shape=(jax.ShapeDtypeStruct((n_p, N_MOD * D_MODEL), F32),
                   jax.ShapeDtypeStruct((n_s, N_MOD * D_MODEL), F32)),
        compiler_params=_cparams(("arbitrary",)),
        name="adaln",
    )(c_p, c_s, w_ada, b_ada.reshape(1, -1))


def _mixer_body(*refs, S, Tt, Ts, start_pos, seq_major, has_state):
    refs = list(refs)
    x_ref, sh_ref, sc_ref, g_ref, win_ref = refs[:5]
    k = 5
    if has_state:
        buf0_ref, hre0_ref, him0_ref = refs[k:k + 3]
        k += 3
    (poolw_ref, pscale_ref, lbre_ref, lbim_ref, wbre_ref, wbim_ref, wcre_ref, wcim_ref, d_ref, gluw_ref,
     glub_ref) = refs[k:k + 11]
    k += 11
    mix_ref, newbuf_ref, hre_out_ref, him_out_ref = refs[k:k + 4]
    z_scr, sre_scr, sim_scr, hre_scr, him_scr, winb_scr, glub_scr = refs[k + 4:]

    i = pl.program_id(0)
    R = Tt * S
    HR = HIST * S
    D = D_MODEL

    @pl.when(i == 0)
    def _init():
        z_scr[0:S, :] = jnp.zeros((S, POOL_WIDTH), F32)
        if has_state:
            for j in range(POOL_BUF):
                z_scr[(j + 1) * S:(j + 2) * S, :] = buf0_ref[:, j, :]
            hre_scr[...] = hre0_ref[...]
            him_scr[...] = him0_ref[...]
        else:
            z_scr[S:HR, :] = jnp.zeros((HR - S, POOL_WIDTH), F32)
            hre_scr[...] = jnp.zeros((S, N_STATE), F32)
            him_scr[...] = jnp.zeros((S, N_STATE), F32)
        winb_scr[...] = win_ref[...].astype(BF16)
        glub_scr[...] = gluw_ref[...].astype(BF16)

    g = g_ref[...]
    Rs = Ts * S
    CB = 512
    n_tiles = S // SUBLANES
    n_ct = SSM_WIDTH // MXU_TILE
    k_per = N_STATE_TILES // n_ct
    for sub in range(Tt // Ts):
        t0 = sub * Ts
        r_lo = sub * Rs

        if seq_major:
            x3 = x_ref[:, t0:t0 + Ts, :]
            h3 = _rms(x3, g) * (1.0 + sc_ref[...][:, None, :]) + sh_ref[...][:, None, :]
            u_nm = jnp.dot(h3.reshape(Rs, D).astype(BF16), winb_scr[...], preferred_element_type=F32)
            u = jnp.swapaxes(u_nm.reshape(S, Ts, D), 0, 1).reshape(Rs, D)
        else:
            x_tm = jnp.concatenate([x_ref[:, t, :] for t in range(t0, t0 + Ts)], axis=0)
            h3 = _rms(x_tm, g).reshape(Ts, S, D) * (1.0 + sc_ref[...]) + sh_ref[...]
            u = jnp.dot(h3.reshape(Rs, D).astype(BF16), winb_scr[...], preferred_element_type=F32)

        up = u[:, 0:POOL_WIDTH]
        us = u[:, POOL_WIDTH:D]
        z_scr[HR + r_lo:HR + r_lo + Rs, :] = up

        row = lax.broadcasted_iota(jnp.int32, (Rs, 1), 0)
        pos = start_pos + i * Tt + t0 + lax.shift_right_logical(row, S.bit_length() - 1)
        outs = []
        for kk, w in enumerate(POOL_WINDOWS):
            lo, hi = kk * POOL_GROUP, (kk + 1) * POOL_GROUP
            cur = z_scr[r_lo:r_lo + HR + Rs, lo:hi]
            step = 1
            while step < w:
                cur = cur[step * S:, :] + cur[:cur.shape[0] - step * S, :]
                step *= 2
            s = cur[cur.shape[0] - Rs:, :]
            cnt = jnp.minimum(w, pos + 1).astype(F32)
            pooled = s / cnt - up[:, lo:hi]
            mixed = jnp.dot(pooled.astype(BF16), poolw_ref[kk].astype(BF16), preferred_element_type=F32)
            outs.append(mixed * pscale_ref[:, lo:hi])

        usb = us.astype(BF16)
        for n in range(N_STATE_TILES):
            kb = (n * MXU_TILE // SSM_STATE * SSM_GROUP) // MXU_TILE
            lhs = usb[:, kb * MXU_TILE:(kb + 1) * MXU_TILE]
            cols = slice(n * MXU_TILE, (n + 1) * MXU_TILE)
            sre_scr[r_lo:r_lo + Rs, cols] = jnp.dot(lhs, wbre_ref[n], preferred_element_type=F32)
            sim_scr[r_lo:r_lo + Rs, cols] = jnp.dot(lhs, wbim_ref[n], preferred_element_type=F32)

        for cb in range(N_STATE // CB):
            c0 = cb * CB
            lr = jnp.broadcast_to(lbre_ref[:, c0:c0 + CB], (SUBLANES, CB))
            li = jnp.broadcast_to(lbim_ref[:, c0:c0 + CB], (SUBLANES, CB))

            def scan_tile(s0, c0=c0, lr=lr, li=li, r_lo=r_lo):
                hr = hre_scr[pl.ds(s0, SUBLANES), c0:c0 + CB]
                hi_ = him_scr[pl.ds(s0, SUBLANES), c0:c0 + CB]
                for t in range(Ts):
                    r0 = r_lo + t * S + s0
                    br = sre_scr[pl.ds(r0, SUBLANES), c0:c0 + CB]
                    bi = sim_scr[pl.ds(r0, SUBLANES), c0:c0 + CB]
                    hr, hi_ = lr * hr - li * hi_ + br, lr * hi_ + li * hr + bi
                    sre_scr[pl.ds(r0, SUBLANES), c0:c0 + CB] = hr
                    sim_scr[pl.ds(r0, SUBLANES), c0:c0 + CB] = hi_
                hre_scr[pl.ds(s0, SUBLANES), c0:c0 + CB] = hr
                him_scr[pl.ds(s0, SUBLANES), c0:c0 + CB] = hi_

            if n_tiles == 1:
                scan_tile(0)
            else:
                def tile_body(j, carry, scan_tile=scan_tile):
                    scan_tile(pl.multiple_of(j * SUBLANES, SUBLANES))
                    return carry

                lax.fori_loop(0, n_tiles, tile_body, 0)

        ys = []
        for m in range(n_ct):
            acc = d_ref[:, m * MXU_TILE:(m + 1) * MXU_TILE] * us[:, m * MXU_TILE:(m + 1) * MXU_TILE]
            for kk in range(k_per):
                kt = m * k_per + kk
                cols = slice(kt * MXU_TILE, (kt + 1) * MXU_TILE)
                acc = acc + jnp.dot(sre_scr[r_lo:r_lo + Rs, cols].astype(BF16), wcre_ref[kt],
                                    preferred_element_type=F32)
                acc = acc + jnp.dot(sim_scr[r_lo:r_lo + Rs, cols].astype(BF16), wcim_ref[kt],
                                    preferred_element_type=F32)
            ys.append(acc)
        y = jnp.concatenate(ys, axis=-1)
        gl = jax.nn.gelu(y)
        gate = jax.nn.sigmoid(jnp.dot(gl.astype(BF16), glub_scr[...], preferred_element_type=F32) + glub_ref[...])
        outs.append(gl * gate)

        mix_tm = jnp.concatenate(outs, axis=-1)
        if seq_major:
            mix_ref[:, t0:t0 + Ts, :] = jnp.swapaxes(mix_tm.reshape(Ts, S, D), 0, 1).astype(mix_ref.dtype)
        else:
            mix_ref[r_lo:r_lo + Rs, :] = mix_tm.astype(mix_ref.dtype)

    for j in range(POOL_BUF):
        r0 = (Tt + 1 + j) * S
        newbuf_ref[:, j, :] = z_scr[r0:r0 + S, :]
    hist = z_scr[R:R + HR, :]
    z_scr[0:HR, :] = hist
    hre_out_ref[...] = hre_scr[...]
    him_out_ref[...] = him_scr[...]


def _mixer(x, mod, g1, w_in, state, wts, *, Tt, Ts, start_pos, seq_major):
    S, L, D = x.shape
    R = Tt * S
    has_state = state is not None
    consts = [g1, w_in] + (list(state) if has_state else []) + list(wts)
    x_spec = pl.BlockSpec((S, Tt, D), lambda i: (0, i, 0))
    mod_specs = [pl.BlockSpec((S, D), lambda i: (0, 0)), pl.BlockSpec((S, D), lambda i: (0, 1))]
    if seq_major:
        mix_spec = pl.BlockSpec((S, Tt, D), lambda i: (0, i, 0))
        mix_shape = jax.ShapeDtypeStruct((S, L, D), BF16)
    else:
        assert Tt == L
        mix_spec = pl.BlockSpec((R, D), lambda i: (0, 0))
        mix_shape = jax.ShapeDtypeStruct((L * S, D), BF16)
    body = functools.partial(_mixer_body, S=S, Tt=Tt, Ts=Ts, start_pos=start_pos, seq_major=seq_major,
                             has_state=has_state)
    return pl.pallas_call(
        body,
        grid=(L // Tt,),
        in_specs=[x_spec] + mod_specs + [_const_spec(a.shape) for a in consts],
        out_specs=(mix_spec,
                   pl.BlockSpec((S, POOL_BUF, POOL_WIDTH), lambda i: (0, 0, 0)),
                   pl.BlockSpec((S, N_STATE), lambda i: (0, 0)),
                   pl.BlockSpec((S, N_STATE), lambda i: (0, 0))),
        out_shape=(mix_shape,
                   jax.ShapeDtypeStruct((S, POOL_BUF, POOL_WIDTH), F32),
                   jax.ShapeDtypeStruct((S, N_STATE), F32),
                   jax.ShapeDtypeStruct((S, N_STATE), F32)),
        scratch_shapes=[pltpu.VMEM(((HIST + Tt) * S, POOL_WIDTH), F32),
                        pltpu.VMEM((R, N_STATE), F32),
                        pltpu.VMEM((R, N_STATE), F32),
                        pltpu.VMEM((S, N_STATE), F32),
                        pltpu.VMEM((S, N_STATE), F32),
                        pltpu.VMEM((D, D), BF16),
                        pltpu.VMEM((SSM_WIDTH, SSM_WIDTH), BF16)],
        compiler_params=_cparams(("arbitrary",)),
        name="mixer_S%d" % S,
    )(x, mod, mod, *consts)


def _split_bf16(v):
    hi = v.astype(BF16)
    lo = (v - hi.astype(F32)).astype(BF16)
    return hi, lo


def _route(h2, rw_ref, rb_ref):
    R = h2.shape[0]
    h2_hi, h2_lo = _split_bf16(h2)
    rw_hi, rw_lo = _split_bf16(rw_ref[...])
    logits = (jnp.dot(h2_hi, rw_hi, preferred_element_type=F32)
              + jnp.dot(h2_lo, rw_hi, preferred_element_type=F32)
              + jnp.dot(h2_hi, rw_lo, preferred_element_type=F32)) + rb_ref[...]
    lane = lax.broadcasted_iota(jnp.int32, (R, ROUTER_LANES), 1).astype(F32)
    ninf = jnp.float32(-jnp.inf)
    none = jnp.float32(ROUTER_LANES)
    is_g = lane < N_EXPERT_GROUPS
    l1 = jnp.where(is_g, logits, ninf)
    m1 = jnp.max(l1, axis=-1, keepdims=True)
    gidx = jnp.min(jnp.where(l1 == m1, lane, none), axis=-1, keepdims=True)
    p_top = 1.0 / jnp.sum(jnp.where(is_g, jnp.exp(logits - m1), 0.0), axis=-1, keepdims=True)
    e_lo = N_EXPERT_GROUPS + gidx * EXPERTS_PER_GROUP
    sel = (lane >= e_lo) & (lane < e_lo + EXPERTS_PER_GROUP)
    l2 = jnp.where(sel, logits, ninf)
    va = jnp.max(l2, axis=-1, keepdims=True)
    ia = jnp.min(jnp.where(l2 == va, lane, none), axis=-1, keepdims=True)
    l2b = jnp.where(lane == ia, ninf, l2)
    vb = jnp.max(l2b, axis=-1, keepdims=True)
    ib = jnp.min(jnp.where(l2b == vb, lane, none), axis=-1, keepdims=True)
    eb = jnp.exp(vb - va)
    den = 1.0 + eb
    gates = jnp.where(lane == ia, (1.0 / den) * p_top, 0.0) + jnp.where(lane == ib, (eb / den) * p_top, 0.0)
    return h2_hi, gidx, e_lo, gates, lane


def _stage3_rows(x, mix, g1, sh2, sc2, g2, n2g_ref, fng_ref, wout_ref, rw_ref, rb_ref, wg_ref, wu_ref, wd_ref):
    R = x.shape[0]
    mixo = jnp.dot(mix, wout_ref[...], preferred_element_type=F32)
    x1 = x + g1 * mixo
    h2 = _rms(x1, n2g_ref[...]) * (1.0 + sc2) + sh2
    h2_hi, _, _, gates, _ = _route(h2, rw_ref, rb_ref)

    acc = jnp.zeros((R, D_MODEL), F32)
    for e in range(N_EXPERTS):
        a = jnp.dot(h2_hi, wg_ref[e], preferred_element_type=F32)
        b = jnp.dot(h2_hi, wu_ref[e], preferred_element_type=F32)
        ge = gates[:, N_EXPERT_GROUPS + e:N_EXPERT_GROUPS + e + 1]
        hid = jax.nn.silu(a) * b * ge
        acc = acc + jnp.dot(hid.astype(BF16), wd_ref[e], preferred_element_type=F32)
    x2 = x1 + g2 * acc
    return _rms(x2, fng_ref[...])


MOE_CHUNK = 256
ROW_TILES = D_MODEL // LANES
PAY_ROWS = 2 * ROW_TILES
META_ROW = ROW_TILES
GROUP_LANE = EXPERTS_PER_GROUP
SLOT_LANE = EXPERTS_PER_GROUP + 1


def _route_body(x_ref, mix_ref, g1_ref, sh2_ref, sc2_ref, n2g_ref, wout_ref, rw_ref, rb_ref,
                x1_ref, pay_ref, meta_ref, cnt_ref, woutb_scr, tri_scr, run_scr, *, tc):
    n, c = pl.program_id(0), pl.program_id(1)
    step = n * pl.num_programs(1) + c

    @pl.when(step == 0)
    def _init():
        woutb_scr[...] = wout_ref[...].astype(BF16)
        r = lax.broadcasted_iota(jnp.int32, (tc, tc), 0)
        q = lax.broadcasted_iota(jnp.int32, (tc, tc), 1)
        tri_scr[...] = jnp.where(q < r, 1.0, 0.0).astype(BF16)
        run_scr[...] = jnp.zeros((1, ROUTER_LANES), F32)

    g1, sh2, sc2 = [r[pl.ds(n, 1), :] for r in (g1_ref, sh2_ref, sc2_ref)]
    mixo = jnp.dot(mix_ref[0], woutb_scr[...], preferred_element_type=F32)
    x1 = x_ref[0] + g1 * mixo
    x1_ref[0] = x1
    h2 = _rms(x1, n2g_ref[...]) * (1.0 + sc2) + sh2
    _, gidx, e_lo, gates, lane = _route(h2, rw_ref, rb_ref)

    onehot = jnp.where(lane == gidx, 1.0, 0.0)
    before = jnp.dot(tri_scr[...], onehot.astype(BF16), preferred_element_type=F32)
    run = run_scr[...]
    slot = jnp.sum(onehot * (before + run), axis=-1, keepdims=True)
    run = run + jnp.sum(onehot, axis=0, keepdims=True)
    run_scr[...] = run
    cnt_ref[...] = run.astype(jnp.int32)

    meta = jnp.where(lane == GROUP_LANE, gidx, 0.0) + jnp.where(lane == SLOT_LANE, slot, 0.0)
    for e in range(EXPERTS_PER_GROUP):
        ge = jnp.sum(jnp.where(lane == e_lo + e, gates, 0.0), axis=-1, keepdims=True)
        meta = meta + jnp.where(lane == e, ge, 0.0)
    meta_ref[...] = meta
    pay_ref[:, 0:ROW_TILES, :] = h2.reshape(tc, ROW_TILES, LANES)
    pay_ref[:, META_ROW, :] = meta
    pay_ref[:, META_ROW + 1:PAY_ROWS, :] = jnp.zeros((tc, PAY_ROWS - META_ROW - 1, LANES), F32)


def _route_call(x, mix, mod, n2g, w_out, rw, rb, tc):
    nb, L, D = x.shape
    n_tok = nb * L
    n_c = L // tc

    def mspec(k):
        return pl.BlockSpec((nb, D), lambda n, c, k=k: (0, k))

    consts = [n2g, w_out, rw, rb]
    return pl.pallas_call(
        functools.partial(_route_body, tc=tc),
        grid=(nb, n_c),
        in_specs=[pl.BlockSpec((1, tc, D), lambda n, c: (n, c, 0)),
                  pl.BlockSpec((1, tc, D), lambda n, c: (n, c, 0)),
                  mspec(2), mspec(3), mspec(4)] + [_const_spec(a.shape) for a in consts],
        out_specs=(pl.BlockSpec((1, tc, D), lambda n, c: (n, c, 0)),
                   pl.BlockSpec((tc, PAY_ROWS, LANES), lambda n, c: (n * n_c + c, 0, 0)),
                   pl.BlockSpec((tc, ROUTER_LANES), lambda n, c: (n * n_c + c, 0)),
                   pl.BlockSpec((1, ROUTER_LANES), lambda n, c: (0, 0))),
        out_shape=(jax.ShapeDtypeStruct((nb, L, D), F32),
                   jax.ShapeDtypeStruct((n_tok, PAY_ROWS, LANES), F32),
                   jax.ShapeDtypeStruct((n_tok, ROUTER_LANES), F32),
                   jax.ShapeDtypeStruct((1, ROUTER_LANES), jnp.int32)),
        scratch_shapes=[pltpu.VMEM((D, D), BF16),
                        pltpu.VMEM((tc, tc), BF16),
                        pltpu.VMEM((1, ROUTER_LANES), F32)],
        compiler_params=_cparams(("arbitrary", "arbitrary")),
        name="moe_route",
    )(x, mix, mod, mod, mod, *consts)


def _experts_body(src_ref, grp_ref, nch_ref, pay_hbm, wg_ref, wu_ref, wd_ref, moe_hbm,
                  wgb_scr, wub_scr, wdb_scr, xin_scr, out_scr, gsem, ssem, *, n_tok):
    s = pl.program_id(0)
    nch = nch_ref[0]
    slot_b = lax.rem(s, 2)

    def gather(chunk, b, i):
        t = jnp.minimum(src_ref[chunk * MOE_CHUNK + i], n_tok - 1)
        return pltpu.make_async_copy(pay_hbm.at[t], xin_scr.at[b, i], gsem.at[b])

    def scatter(chunk, b, i):
        return pltpu.make_async_copy(out_scr.at[b, i], moe_hbm.at[src_ref[chunk * MOE_CHUNK + i]], ssem.at[b])

    @pl.when(jnp.logical_and(s == 0, nch > 0))
    def _first_gather():
        def start_row(i, carry):
            gather(0, 0, i).start()
            return carry

        lax.fori_loop(0, MOE_CHUNK, start_row, 0, unroll=8)
        out_scr[1] = jnp.zeros((MOE_CHUNK, ROW_TILES, LANES), F32)
        spare = pltpu.make_async_copy(out_scr.at[1], moe_hbm.at[pl.ds(n_tok, MOE_CHUNK)], ssem.at[1])
        spare.start()
        spare.wait()

    fresh = jnp.logical_or(s == 0, grp_ref[s] != grp_ref[jnp.maximum(s - 1, 0)])

    @pl.when(jnp.logical_and(s < nch, fresh))
    def _load_group():
        wgb_scr[...] = wg_ref[...].astype(BF16)
        wub_scr[...] = wu_ref[...].astype(BF16)
        wdb_scr[...] = wd_ref[...].astype(BF16)

    @pl.when(s < nch)
    def _chunk():
        for i in range(MOE_CHUNK):
            gather(s, slot_b, i).wait()
        nxt = jnp.minimum(s + 1, nch - 1)
        for i in range(MOE_CHUNK):
            gather(nxt, 1 - slot_b, i).start()
        xin = xin_scr.at[slot_b]
        xb = xin[:, 0:ROW_TILES, :].reshape(MOE_CHUNK, D_MODEL).astype(BF16)
        meta = xin[:, META_ROW, :]
        acc = jnp.zeros((MOE_CHUNK, D_MODEL), F32)
        for e in range(EXPERTS_PER_GROUP):
            a = jnp.dot(xb, wgb_scr[e], preferred_element_type=F32)
            b = jnp.dot(xb, wub_scr[e], preferred_element_type=F32)
            hid = jax.nn.silu(a) * b * meta[:, e:e + 1]
            acc = acc + jnp.dot(hid.astype(BF16), wdb_scr[e], preferred_element_type=F32)

        @pl.when(s >= 2)
        def _reuse():
            for i in range(MOE_CHUNK):
                scatter(s, slot_b, i).wait()

        out_scr[slot_b] = acc.reshape(MOE_CHUNK, ROW_TILES, LANES)
        for i in range(MOE_CHUNK):
            scatter(s, slot_b, i).start()

    @pl.when(s == nch - 1)
    def _drain():
        for i in range(MOE_CHUNK):
            gather(s, 1 - slot_b, i).wait()
        for i in range(MOE_CHUNK):
            scatter(s, slot_b, i).wait()

        @pl.when(s >= 1)
        def _drain_prev():
            for i in range(MOE_CHUNK):
                scatter(s, 1 - slot_b, i).wait()


def _experts_call(src, grp, nch, pay, w_gate, w_up, w_down, n_tok):
    n_steps = grp.shape[0]
    E, D, F = w_gate.shape
    gsz = EXPERTS_PER_GROUP
    return pl.pallas_call(
        functools.partial(_experts_body, n_tok=n_tok),
        grid_spec=pltpu.PrefetchScalarGridSpec(
            num_scalar_prefetch=3,
            grid=(n_steps,),
            in_specs=[pl.BlockSpec(memory_space=pl.ANY),
                      pl.BlockSpec((gsz, D, F), lambda s, src, grp, nch: (grp[s], 0, 0)),
                      pl.BlockSpec((gsz, D, F), lambda s, src, grp, nch: (grp[s], 0, 0)),
                      pl.BlockSpec((gsz, F, D), lambda s, src, grp, nch: (grp[s], 0, 0))],
            out_specs=pl.BlockSpec(memory_space=pl.ANY),
            scratch_shapes=[pltpu.VMEM((gsz, D, F), BF16), pltpu.VMEM((gsz, D, F), BF16),
                            pltpu.VMEM((gsz, F, D), BF16),
                            pltpu.VMEM((2, MOE_CHUNK, PAY_ROWS, LANES), F32),
                            pltpu.VMEM((2, MOE_CHUNK, ROW_TILES, LANES), F32),
                            pltpu.SemaphoreType.DMA((2,)), pltpu.SemaphoreType.DMA((2,))]),
        out_shape=jax.ShapeDtypeStruct((n_tok + MOE_CHUNK, ROW_TILES, LANES), F32),
        compiler_params=_cparams(("arbitrary",)),
        name="moe_experts",
    )(src, grp, nch, pay, w_gate, w_up, w_down)


def _finish_body(x1_ref, moe_ref, g2_ref, fng_ref, y_ref):
    n = pl.program_id(0)
    moe = moe_ref[...].reshape(x1_ref.shape[1], D_MODEL)
    x2 = x1_ref[0] + g2_ref[pl.ds(n, 1), :] * moe
    y_ref[0] = _rms(x2, fng_ref[...])


def _finish_call(x1, moe, mod, fng, tc):
    nb, L, D = x1.shape
    n_c = L // tc
    return pl.pallas_call(
        _finish_body,
        grid=(nb, n_c),
        in_specs=[pl.BlockSpec((1, tc, D), lambda n, c: (n, c, 0)),
                  pl.BlockSpec((tc, ROW_TILES, LANES), lambda n, c: (n * n_c + c, 0, 0)),
                  pl.BlockSpec((nb, D), lambda n, c: (0, 5)),
                  _const_spec(fng.shape)],
        out_specs=pl.BlockSpec((1, tc, D), lambda n, c: (n, c, 0)),
        out_shape=jax.ShapeDtypeStruct((nb, L, D), F32),
        compiler_params=_cparams(("arbitrary", "arbitrary")),
        name="moe_finish",
    )(x1, moe, mod, fng)


def _invert_body(pos_ref, src_ref, *, n_tok, n_rows):
    def fill(j, carry):
        src_ref[j] = n_tok + lax.rem(j, MOE_CHUNK)
        return carry

    lax.fori_loop(0, n_rows, fill, 0, unroll=8)

    def put(i, carry):
        src_ref[pos_ref[i]] = i
        return carry

    lax.fori_loop(0, n_tok, put, 0, unroll=8)


def _invert(pos, n_rows):
    n_tok = pos.shape[0]
    return pl.pallas_call(
        functools.partial(_invert_body, n_tok=n_tok, n_rows=n_rows),
        in_specs=[pl.BlockSpec(memory_space=pltpu.SMEM)],
        out_specs=pl.BlockSpec(memory_space=pltpu.SMEM),
        out_shape=jax.ShapeDtypeStruct((n_rows,), jnp.int32),
        name="moe_invert",
    )(pos)


def _sorted_order(meta, counts, n_tok):
    n_steps = n_tok // MOE_CHUNK + N_EXPERT_GROUPS
    n_rows = n_steps * MOE_CHUNK
    g = meta[:, GROUP_LANE].astype(jnp.int32)
    slot = meta[:, SLOT_LANE].astype(jnp.int32)
    nch_g = (counts + MOE_CHUNK - 1) // MOE_CHUNK
    ends = jnp.cumsum(nch_g)
    first_row = (ends - nch_g) * MOE_CHUNK
    base = sum(jnp.where(g == k, first_row[k], 0) for k in range(N_EXPERT_GROUPS))
    pos = jnp.clip(base + slot, 0, n_rows - 1)
    src = _invert(pos, n_rows)
    total = ends[-1]
    s = jnp.clip(jnp.arange(n_steps, dtype=jnp.int32), 0, jnp.maximum(total - 1, 0))
    grp = jnp.minimum(jnp.sum((s[:, None] >= ends[None, :]).astype(jnp.int32), axis=1), N_EXPERT_GROUPS - 1)
    return src, grp.astype(jnp.int32), total.reshape(1).astype(jnp.int32)


def _stage3_prompt_sparse(x, mix, mod, n2g, fng, w_out, rw, rb, w_gate, w_up, w_down, tc):
    nb, L, D = x.shape
    n_tok = nb * L
    x1, pay, meta, cnt = _route_call(x, mix, mod, n2g, w_out, rw, rb, tc)
    src, grp, nch = _sorted_order(meta, cnt[0, :N_EXPERT_GROUPS], n_tok)
    moe = _experts_call(src, grp, nch, pay, w_gate, w_up, w_down, n_tok)
    return _finish_call(x1, moe, mod, fng, tc)


def _stage3_prompt_body(x_ref, mix_ref, g1_ref, sh2_ref, sc2_ref, g2_ref, *rest):
    wrefs, y_ref = rest[:-1], rest[-1]
    n = pl.program_id(0)
    mods = [r[pl.ds(n, 1), :] for r in (g1_ref, sh2_ref, sc2_ref, g2_ref)]
    y_ref[0] = _stage3_rows(x_ref[0], mix_ref[0], *mods, *wrefs)


def _stage3_prompt(x, mix, mod, wts, tc):
    nb, L, D = x.shape
    consts = list(wts)

    def mspec(k):
        return pl.BlockSpec((nb, D), lambda n, c, k=k: (0, k))

    return pl.pallas_call(
        _stage3_prompt_body,
        grid=(nb, L // tc),
        in_specs=[pl.BlockSpec((1, tc, D), lambda n, c: (n, c, 0)),
                  pl.BlockSpec((1, tc, D), lambda n, c: (n, c, 0)),
                  mspec(2), mspec(3), mspec(4), mspec(5)] + [_const_spec(a.shape) for a in consts],
        out_specs=pl.BlockSpec((1, tc, D), lambda n, c: (n, c, 0)),
        out_shape=jax.ShapeDtypeStruct((nb, L, D), F32),
        compiler_params=_cparams(("arbitrary", "arbitrary")),
        name="stage3_prompt",
    )(x, mix, mod, mod, mod, mod, *consts)


def _stage3_sample_body(x_ref, mix_ref, g1_ref, sh2_ref, sc2_ref, g2_ref, *rest, S, L):
    wrefs, y_ref = rest[:-1], rest[-1]
    x_tm = jnp.concatenate([x_ref[:, t, :] for t in range(L)], axis=0)

    def rows(r):
        return jnp.concatenate([r[...]] * L, axis=0)

    y = _stage3_rows(x_tm, mix_ref[...], rows(g1_ref), rows(sh2_ref), rows(sc2_ref), rows(g2_ref), *wrefs)
    for t in range(L):
        y_ref[:, t, :] = y[t * S:(t + 1) * S, :]


def _stage3_sample(x, mix_tm, mod, wts):
    ns, L, D = x.shape
    consts = list(wts)

    def mspec(k):
        return pl.BlockSpec((ns, D), lambda i, k=k: (0, k))

    return pl.pallas_call(
        functools.partial(_stage3_sample_body, S=ns, L=L),
        grid=(1,),
        in_specs=[pl.BlockSpec((ns, L, D), lambda i: (0, 0, 0)),
                  pl.BlockSpec((L * ns, D), lambda i: (0, 0)),
                  mspec(2), mspec(3), mspec(4), mspec(5)] + [_const_spec(a.shape) for a in consts],
        out_specs=pl.BlockSpec((ns, L, D), lambda i: (0, 0, 0)),
        out_shape=jax.ShapeDtypeStruct((ns, L, D), F32),
        compiler_params=_cparams(("arbitrary",)),
        name="stage3_sample",
    )(x, mix_tm, mod, mod, mod, mod, *consts)


def kernel(x_prompt, x_sample, c_prompt, c_sample, state_pool, state_ssm_re, state_ssm_im, w_ada, b_ada, norm1_g, w_in, pool_w, pool_scale, ssm_a_re, ssm_a_im, ssm_log_dt, ssm_b_re, ssm_b_im, ssm_c_re, ssm_c_im, ssm_d, glu_w, glu_b, w_out, norm2_g, router_w1, router_b1, router_w2, router_b2, exp_w_gate, exp_w_up, exp_w_down, final_norm_g):
    depth = w_ada.shape[0]
    assert depth == 1
    l = 0
    nb, L, D = x_prompt.shape
    ns, Ls, _ = x_sample.shape

    lb_re, lb_im, d_flat, wbre, wbim, wcre, wcim = _ssm_prep(
        ssm_a_re[l], ssm_a_im[l], ssm_log_dt[l],
        jnp.transpose(ssm_b_re[l], (0, 2, 1)), jnp.transpose(ssm_b_im[l], (0, 2, 1)),
        jnp.transpose(ssm_c_re[l], (0, 2, 1)), jnp.transpose(ssm_c_im[l], (0, 2, 1)), ssm_d[l])
    mix_wts = (pool_w[l], pool_scale[l].reshape(1, -1), lb_re, lb_im, wbre, wbim, wcre, wcim, d_flat,
               glu_w[l], glu_b[l].reshape(1, -1))

    rw = jnp.concatenate([router_w1[l], jnp.transpose(router_w2[l], (1, 0, 2)).reshape(D, N_EXPERTS)], axis=1)
    rw = jnp.pad(rw, ((0, 0), (0, ROUTER_LANES - rw.shape[1])))
    rb = jnp.concatenate([router_b1[l], router_b2[l].reshape(-1)])
    rb = jnp.pad(rb, (0, ROUTER_LANES - rb.shape[0])).reshape(1, -1)
    s3_wts = (norm2_g[l].reshape(1, -1), final_norm_g.reshape(1, -1), w_out[l].astype(BF16), rw, rb,
              exp_w_gate[l].astype(BF16), exp_w_up[l].astype(BF16), exp_w_down[l].astype(BF16))

    mod_p, mod_s = _adaln(c_prompt, c_sample, w_ada[l], b_ada[l])
    g1 = norm1_g[l].reshape(1, -1)

    mix_p, pool_p, hre_p, him_p = _mixer(x_prompt, mod_p, g1, w_in[l], None, mix_wts,
                                         Tt=128, Ts=64, start_pos=0, seq_major=True)
    y_p = _stage3_prompt(x_prompt, mix_p, mod_p, s3_wts, 512)

    state = (state_pool[l], state_ssm_re[l].reshape(ns, N_STATE), state_ssm_im[l].reshape(ns, N_STATE))
    mix_s, pool_s, hre_s, him_s = _mixer(x_sample, mod_s, g1, w_in[l], state, mix_wts,
                                         Tt=Ls, Ts=Ls // 2, start_pos=PAST_LEN, seq_major=False)
    y_s = _stage3_sample(x_sample, mix_s, mod_s, s3_wts)

    def st(a, n):
        return a.reshape(1, n, N_SSM_GROUPS, SSM_STATE)

    return (y_p, y_s, pool_p[None], pool_s[None], st(hre_p, nb), st(him_p, nb), st(hre_s, ns), st(him_s, ns))
```

```python
import functools

import jax
import jax.numpy as jnp
from jax import lax
from jax.experimental import pallas as pl
from jax.experimental.pallas import tpu as pltpu

D_MODEL = 1024
POOL_WIDTH = 512
SSM_WIDTH = 512
POOL_WINDOWS = (2, 4, 8, 16)
POOL_GROUP = 128
POOL_BUF = 15
HIST = 16
SSM_GROUP = 16
N_SSM_GROUPS = 32
SSM_STATE = 64
N_STATE = N_SSM_GROUPS * SSM_STATE
N_EXPERT_GROUPS = 4
EXPERTS_PER_GROUP = 4
N_EXPERTS = 16
EXPERT_HIDDEN = 256
N_MOD = 6
EPS = 1e-6
PAST_LEN = 16384

MXU_TILE = 256
LANES = 128
SUBLANES = 8
ROUTER_LANES = 128
VMEM_LIMIT = 60 * 1024 * 1024
N_STATE_TILES = N_STATE // MXU_TILE

F32 = jnp.float32
BF16 = jnp.bfloat16


def _cparams(sem):
    return pltpu.CompilerParams(dimension_semantics=sem, vmem_limit_bytes=VMEM_LIMIT)


def _const_spec(shape):
    nd = len(shape)
    return pl.BlockSpec(shape, lambda *_: (0,) * nd, pipeline_mode=pl.Buffered(1))


def _rms(x, g):
    return x * lax.rsqrt(jnp.mean(x * x, axis=-1, keepdims=True) + EPS) * g


def _ssm_prep_body(are_ref, aim_ref, ldt_ref, bre_ref, bim_ref, cre_ref, cim_ref, d_ref,
                   lbre_ref, lbim_ref, dflat_ref, wbre_ref, wbim_ref, wcre_ref, wcim_ref):
    G, H, P = N_SSM_GROUPS, SSM_GROUP, SSM_STATE
    a_re = are_ref[...]
    a_im = aim_ref[...]
    dt = jnp.exp(ldt_ref[...])
    mag = jnp.exp(a_re * dt)
    lb_re = mag * jnp.cos(a_im * dt)
    lb_im = mag * jnp.sin(a_im * dt)
    den = a_re * a_re + a_im * a_im
    nr = lb_re - 1.0
    ni = lb_im
    k_re = (nr * a_re + ni * a_im) / den
    k_im = (ni * a_re - nr * a_im) / den
    b_re = bre_ref[...]
    b_im = bim_ref[...]
    bb_re = k_re[:, None, :] * b_re - k_im[:, None, :] * b_im
    bb_im = k_re[:, None, :] * b_im + k_im[:, None, :] * b_re
    c_re = cre_ref[...]
    c_im = cim_ref[...]
    d = d_ref[...]

    zeros = jnp.zeros((N_STATE_TILES, MXU_TILE, MXU_TILE), BF16)
    wbre_ref[...] = zeros
    wbim_ref[...] = zeros
    wcre_ref[...] = zeros
    wcim_ref[...] = zeros
    g_per_tile = MXU_TILE // P
    g_per_blk = MXU_TILE // H
    for g in range(G):
        lbre_ref[:, g * P:(g + 1) * P] = lb_re[g:g + 1, :]
        lbim_ref[:, g * P:(g + 1) * P] = lb_im[g:g + 1, :]
        dflat_ref[:, g * H:(g + 1) * H] = d[g:g + 1, :]
        n, gi = divmod(g, g_per_tile)
        r0 = (g % g_per_blk) * H
        c0 = gi * P
        wbre_ref[n, r0:r0 + H, c0:c0 + P] = bb_re[g].astype(BF16)
        wbim_ref[n, r0:r0 + H, c0:c0 + P] = bb_im[g].astype(BF16)
        wcre_ref[n, c0:c0 + P, r0:r0 + H] = c_re[g].astype(BF16)
        wcim_ref[n, c0:c0 + P, r0:r0 + H] = (-c_im[g]).astype(BF16)


def _ssm_prep(a_re, a_im, log_dt, b_re_t, b_im_t, c_re_t, c_im_t, d):
    G = N_SSM_GROUPS
    tile = jax.ShapeDtypeStruct((N_STATE_TILES, MXU_TILE, MXU_TILE), BF16)
    return pl.pallas_call(
        _ssm_prep_body,
        out_shape=(jax.ShapeDtypeStruct((1, N_STATE), F32), jax.ShapeDtypeStruct((1, N_STATE), F32),
                   jax.ShapeDtypeStruct((1, SSM_WIDTH), F32), tile, tile, tile, tile),
        name="ssm_prep",
    )(a_re, a_im, log_dt.reshape(G, 1), b_re_t, b_im_t, c_re_t, c_im_t, d)


def _adaln_body(cp_ref, cs_ref, w_ref, b_ref, op_ref, os_ref):
    w = w_ref[...].astype(BF16)
    b = b_ref[...]
    op_ref[...] = jnp.dot(jax.nn.silu(cp_ref[...]).astype(BF16), w, preferred_element_type=F32) + b
    os_ref[...] = jnp.dot(jax.nn.silu(cs_ref[...]).astype(BF16), w, preferred_element_type=F32) + b


def _adaln(c_p, c_s, w_ada, b_ada):
    n_p, n_s = c_p.shape[0], c_s.shape[0]
    tn = D_MODEL
    return pl.pallas_call(
        _adaln_body,
        grid=(N_MOD * D_MODEL // tn,),
        in_specs=[pl.BlockSpec((n_p, D_MODEL), lambda j: (0, 0)),
                  pl.BlockSpec((n_s, D_MODEL), lambda j: (0, 0)),
                  pl.BlockSpec((D_MODEL, tn), lambda j: (0, j)),
                  pl.BlockSpec((1, tn), lambda j: (0, j))],
        out_specs=(pl.BlockSpec((n_p, tn), lambda j: (0, j)), pl.BlockSpec((n_s, tn), lambda j: (0, j))),
        out_shape=(jax.ShapeDtypeStruct((n_p, N_MOD * D_MODEL), F32),
                   jax.ShapeDtypeStruct((n_s, N_MOD * D_MODEL), F32)),
        compiler_params=_cparams(("arbitrary",)),
        name="adaln",
    )(c_p, c_s, w_ada, b_ada.reshape(1, -1))


def _mixer_body(*refs, S, Tt, Ts, start_pos, seq_major, has_state):
    refs = list(refs)
    x_ref, sh_ref, sc_ref, g_ref, win_ref = refs[:5]
    k = 5
    if has_state:
        buf0_ref, hre0_ref, him0_ref = refs[k:k + 3]
        k += 3
    (poolw_ref, pscale_ref, lbre_ref, lbim_ref, wbre_ref, wbim_ref, wcre_ref, wcim_ref, d_ref, gluw_ref,
     glub_ref) = refs[k:k + 11]
    k += 11
    mix_ref, newbuf_ref, hre_out_ref, him_out_ref = refs[k:k + 4]
    z_scr, sre_scr, sim_scr, hre_scr, him_scr, winb_scr, glub_scr = refs[k + 4:]

    i = pl.program_id(0)
    R = Tt * S
    HR = HIST * S
    D = D_MODEL

    @pl.when(i == 0)
    def _init():
        z_scr[0:S, :] = jnp.zeros((S, POOL_WIDTH), F32)
        if has_state:
            for j in range(POOL_BUF):
                z_scr[(j + 1) * S:(j + 2) * S, :] = buf0_ref[:, j, :]
            hre_scr[...] = hre0_ref[...]
            him_scr[...] = him0_ref[...]
        else:
            z_scr[S:HR, :] = jnp.zeros((HR - S, POOL_WIDTH), F32)
            hre_scr[...] = jnp.zeros((S, N_STATE), F32)
            him_scr[...] = jnp.zeros((S, N_STATE), F32)
        winb_scr[...] = win_ref[...].astype(BF16)
        glub_scr[...] = gluw_ref[...].astype(BF16)

    g = g_ref[...]
    Rs = Ts * S
    CB = 512
    n_tiles = S // SUBLANES
    n_ct = SSM_WIDTH // MXU_TILE
    k_per = N_STATE_TILES // n_ct
    for sub in range(Tt // Ts):
        t0 = sub * Ts
        r_lo = sub * Rs

        if seq_major:
            x3 = x_ref[:, t0:t0 + Ts, :]
            h3 = _rms(x3, g) * (1.0 + sc_ref[...][:, None, :]) + sh_ref[...][:, None, :]
            u_nm = jnp.dot(h3.reshape(Rs, D).astype(BF16), winb_scr[...], preferred_element_type=F32)
            u = jnp.swapaxes(u_nm.reshape(S, Ts, D), 0, 1).reshape(Rs, D)
        else:
            x_tm = jnp.concatenate([x_ref[:, t, :] for t in range(t0, t0 + Ts)], axis=0)
            h3 = _rms(x_tm, g).reshape(Ts, S, D) * (1.0 + sc_ref[...]) + sh_ref[...]
            u = jnp.dot(h3.reshape(Rs, D).astype(BF16), winb_scr[...], preferred_element_type=F32)

        up = u[:, 0:POOL_WIDTH]
        us = u[:, POOL_WIDTH:D]
        z_scr[HR + r_lo:HR + r_lo + Rs, :] = up

        row = lax.broadcasted_iota(jnp.int32, (Rs, 1), 0)
        pos = start_pos + i * Tt + t0 + lax.shift_right_logical(row, S.bit_length() - 1)
        outs = []
        for kk, w in enumerate(POOL_WINDOWS):
            lo, hi = kk * POOL_GROUP, (kk + 1) * POOL_GROUP
            cur = z_scr[r_lo:r_lo + HR + Rs, lo:hi]
            step = 1
            while step < w:
                cur = cur[step * S:, :] + cur[:cur.shape[0] - step * S, :]
                step *= 2
            s = cur[cur.shape[0] - Rs:, :]
            cnt = jnp.minimum(w, pos + 1).astype(F32)
            pooled = s / cnt - up[:, lo:hi]
            mixed = jnp.dot(pooled.astype(BF16), poolw_ref[kk].astype(BF16), preferred_element_type=F32)
            outs.append(mixed * pscale_ref[:, lo:hi])

        usb = us.astype(BF16)
        for n in range(N_STATE_TILES):
            kb = (n * MXU_TILE // SSM_STATE * SSM_GROUP) // MXU_TILE
            lhs = usb[:, kb * MXU_TILE:(kb + 1) * MXU_TILE]
            cols = slice(n * MXU_TILE, (n + 1) * MXU_TILE)
            sre_scr[r_lo:r_lo + Rs, cols] = jnp.dot(lhs, wbre_ref[n], preferred_element_type=F32)
            sim_scr[r_lo:r_lo + Rs, cols] = jnp.dot(lhs, wbim_ref[n], preferred_element_type=F32)

        for cb in range(N_STATE // CB):
            c0 = cb * CB
            lr = jnp.broadcast_to(lbre_ref[:, c0:c0 + CB], (SUBLANES, CB))
            li = jnp.broadcast_to(lbim_ref[:, c0:c0 + CB], (SUBLANES, CB))

            def scan_tile(s0, c0=c0, lr=lr, li=li, r_lo=r_lo):
                hr = hre_scr[pl.ds(s0, SUBLANES), c0:c0 + CB]
                hi_ = him_scr[pl.ds(s0, SUBLANES), c0:c0 + CB]
                for t in range(Ts):
                    r0 = r_lo + t * S + s0
                    br = sre_scr[pl.ds(r0, SUBLANES), c0:c0 + CB]
                    bi = sim_scr[pl.ds(r0, SUBLANES), c0:c0 + CB]
                    hr, hi_ = lr * hr - li * hi_ + br, lr * hi_ + li * hr + bi
                    sre_scr[pl.ds(r0, SUBLANES), c0:c0 + CB] = hr
                    sim_scr[pl.ds(r0, SUBLANES), c0:c0 + CB] = hi_
                hre_scr[pl.ds(s0, SUBLANES), c0:c0 + CB] = hr
                him_scr[pl.ds(s0, SUBLANES), c0:c0 + CB] = hi_

            if n_tiles == 1:
                scan_tile(0)
            else:
                def tile_body(j, carry, scan_tile=scan_tile):
                    scan_tile(pl.multiple_of(j * SUBLANES, SUBLANES))
                    return carry

                lax.fori_loop(0, n_tiles, tile_body, 0)

        ys = []
        for m in range(n_ct):
            acc = d_ref[:, m * MXU_TILE:(m + 1) * MXU_TILE] * us[:, m * MXU_TILE:(m + 1) * MXU_TILE]
            for kk in range(k_per):
                kt = m * k_per + kk
                cols = slice(kt * MXU_TILE, (kt + 1) * MXU_TILE)
                acc = acc + jnp.dot(sre_scr[r_lo:r_lo + Rs, cols].astype(BF16), wcre_ref[kt],
                                    preferred_element_type=F32)
                acc = acc + jnp.dot(sim_scr[r_lo:r_lo + Rs, cols].astype(BF16), wcim_ref[kt],
                                    preferred_element_type=F32)
            ys.append(acc)
        y = jnp.concatenate(ys, axis=-1)
        gl = jax.nn.gelu(y)
        gate = jax.nn.sigmoid(jnp.dot(gl.astype(BF16), glub_scr[...], preferred_element_type=F32) + glub_ref[...])
        outs.append(gl * gate)

        mix_tm = jnp.concatenate(outs, axis=-1)
        if seq_major:
            mix_ref[:, t0:t0 + Ts, :] = jnp.swapaxes(mix_tm.reshape(Ts, S, D), 0, 1).astype(mix_ref.dtype)
        else:
            mix_ref[r_lo:r_lo + Rs, :] = mix_tm.astype(mix_ref.dtype)

    for j in range(POOL_BUF):
        r0 = (Tt + 1 + j) * S
        newbuf_ref[:, j, :] = z_scr[r0:r0 + S, :]
    hist = z_scr[R:R + HR, :]
    z_scr[0:HR, :] = hist
    hre_out_ref[...] = hre_scr[...]
    him_out_ref[...] = him_scr[...]


def _mixer(x, mod, g1, w_in, state, wts, *, Tt, Ts, start_pos, seq_major):
    S, L, D = x.shape
    R = Tt * S
    has_state = state is not None
    consts = [g1, w_in] + (list(state) if has_state else []) + list(wts)
    x_spec = pl.BlockSpec((S, Tt, D), lambda i: (0, i, 0))
    mod_specs = [pl.BlockSpec((S, D), lambda i: (0, 0)), pl.BlockSpec((S, D), lambda i: (0, 1))]
    if seq_major:
        mix_spec = pl.BlockSpec((S, Tt, D), lambda i: (0, i, 0))
        mix_shape = jax.ShapeDtypeStruct((S, L, D), BF16)
    else:
        assert Tt == L
        mix_spec = pl.BlockSpec((R, D), lambda i: (0, 0))
        mix_shape = jax.ShapeDtypeStruct((L * S, D), BF16)
    body = functools.partial(_mixer_body, S=S, Tt=Tt, Ts=Ts, start_pos=start_pos, seq_major=seq_major,
                             has_state=has_state)
    return pl.pallas_call(
        body,
        grid=(L // Tt,),
        in_specs=[x_spec] + mod_specs + [_const_spec(a.shape) for a in consts],
        out_specs=(mix_spec,
                   pl.BlockSpec((S, POOL_BUF, POOL_WIDTH), lambda i: (0, 0, 0)),
                   pl.BlockSpec((S, N_STATE), lambda i: (0, 0)),
                   pl.BlockSpec((S, N_STATE), lambda i: (0, 0))),
        out_shape=(mix_shape,
                   jax.ShapeDtypeStruct((S, POOL_BUF, POOL_WIDTH), F32),
                   jax.ShapeDtypeStruct((S, N_STATE), F32),
                   jax.ShapeDtypeStruct((S, N_STATE), F32)),
        scratch_shapes=[pltpu.VMEM(((HIST + Tt) * S, POOL_WIDTH), F32),
                        pltpu.VMEM((R, N_STATE), F32),
                        pltpu.VMEM((R, N_STATE), F32),
                        pltpu.VMEM((S, N_STATE), F32),
                        pltpu.VMEM((S, N_STATE), F32),
                        pltpu.VMEM((D, D), BF16),
                        pltpu.VMEM((SSM_WIDTH, SSM_WIDTH), BF16)],
        compiler_params=_cparams(("arbitrary",)),
        name="mixer_S%d" % S,
    )(x, mod, mod, *consts)


def _split_bf16(v):
    hi = v.astype(BF16)
    lo = (v - hi.astype(F32)).astype(BF16)
    return hi, lo


def _route(h2, rw_ref, rb_ref):
    R = h2.shape[0]
    h2_hi, h2_lo = _split_bf16(h2)
    rw_hi, rw_lo = _split_bf16(rw_ref[...])
    logits = (jnp.dot(h2_hi, rw_hi, preferred_element_type=F32)
              + jnp.dot(h2_lo, rw_hi, preferred_element_type=F32)
              + jnp.dot(h2_hi, rw_lo, preferred_element_type=F32)) + rb_ref[...]
    lane = lax.broadcasted_iota(jnp.int32, (R, ROUTER_LANES), 1).astype(F32)
    ninf = jnp.float32(-jnp.inf)
    none = jnp.float32(ROUTER_LANES)
    is_g = lane < N_EXPERT_GROUPS
    l1 = jnp.where(is_g, logits, ninf)
    m1 = jnp.max(l1, axis=-1, keepdims=True)
    gidx = jnp.min(jnp.where(l1 == m1, lane, none), axis=-1, keepdims=True)
    p_top = 1.0 / jnp.sum(jnp.where(is_g, jnp.exp(logits - m1), 0.0), axis=-1, keepdims=True)
    e_lo = N_EXPERT_GROUPS + gidx * EXPERTS_PER_GROUP
    sel = (lane >= e_lo) & (lane < e_lo + EXPERTS_PER_GROUP)
    l2 = jnp.where(sel, logits, ninf)
    va = jnp.max(l2, axis=-1, keepdims=True)
    ia = jnp.min(jnp.where(l2 == va, lane, none), axis=-1, keepdims=True)
    l2b = jnp.where(lane == ia, ninf, l2)
    vb = jnp.max(l2b, axis=-1, keepdims=True)
    ib = jnp.min(jnp.where(l2b == vb, lane, none), axis=-1, keepdims=True)
    eb = jnp.exp(vb - va)
    den = 1.0 + eb
    gates = jnp.where(lane == ia, (1.0 / den) * p_top, 0.0) + jnp.where(lane == ib, (eb / den) * p_top, 0.0)
    return h2_hi, gidx, e_lo, gates, lane


def _stage3_rows(x, mix, g1, sh2, sc2, g2, n2g_ref, fng_ref, wout_ref, rw_ref, rb_ref, wg_ref, wu_ref, wd_ref):
    R = x.shape[0]
    mixo = jnp.dot(mix, wout_ref[...], preferred_element_type=F32)
    x1 = x + g1 * mixo
    h2 = _rms(x1, n2g_ref[...]) * (1.0 + sc2) + sh2
    h2_hi, _, _, gates, _ = _route(h2, rw_ref, rb_ref)

    acc = jnp.zeros((R, D_MODEL), F32)
    for e in range(N_EXPERTS):
        a = jnp.dot(h2_hi, wg_ref[e], preferred_element_type=F32)
        b = jnp.dot(h2_hi, wu_ref[e], preferred_element_type=F32)
        ge = gates[:, N_EXPERT_GROUPS + e:N_EXPERT_GROUPS + e + 1]
        hid = jax.nn.silu(a) * b * ge
        acc = acc + jnp.dot(hid.astype(BF16), wd_ref[e], preferred_element_type=F32)
    x2 = x1 + g2 * acc
    return _rms(x2, fng_ref[...])


MOE_CHUNK = 256
ROW_TILES = D_MODEL // LANES
PAY_ROWS = 2 * ROW_TILES
META_ROW = ROW_TILES
GROUP_LANE = EXPERTS_PER_GROUP
SLOT_LANE = EXPERTS_PER_GROUP + 1


def _route_body(x_ref, mix_ref, g1_ref, sh2_ref, sc2_ref, n2g_ref, wout_ref, rw_ref, rb_ref,
                x1_ref, pay_ref, meta_ref, cnt_ref, woutb_scr, tri_scr, run_scr, *, tc):
    n, c = pl.program_id(0), pl.program_id(1)
    step = n * pl.num_programs(1) + c

    @pl.when(step == 0)
    def _init():
        woutb_scr[...] = wout_ref[...].astype(BF16)
        r = lax.broadcasted_iota(jnp.int32, (tc, tc), 0)
        q = lax.broadcasted_iota(jnp.int32, (tc, tc), 1)
        tri_scr[...] = jnp.where(q < r, 1.0, 0.0).astype(BF16)
        run_scr[...] = jnp.zeros((1, ROUTER_LANES), F32)

    g1, sh2, sc2 = [r[pl.ds(n, 1), :] for r in (g1_ref, sh2_ref, sc2_ref)]
    mixo = jnp.dot(mix_ref[0], woutb_scr[...], preferred_element_type=F32)
    x1 = x_ref[0] + g1 * mixo
    x1_ref[0] = x1
    h2 = _rms(x1, n2g_ref[...]) * (1.0 + sc2) + sh2
    _, gidx, e_lo, gates, lane = _route(h2, rw_ref, rb_ref)

    onehot = jnp.where(lane == gidx, 1.0, 0.0)
    before = jnp.dot(tri_scr[...], onehot.astype(BF16), preferred_element_type=F32)
    run = run_scr[...]
    slot = jnp.sum(onehot * (before + run), axis=-1, keepdims=True)
    run = run + jnp.sum(onehot, axis=0, keepdims=True)
    run_scr[...] = run
    cnt_ref[...] = run.astype(jnp.int32)

    meta = jnp.where(lane == GROUP_LANE, gidx, 0.0) + jnp.where(lane == SLOT_LANE, slot, 0.0)
    for e in range(EXPERTS_PER_GROUP):
        ge = jnp.sum(jnp.where(lane == e_lo + e, gates, 0.0), axis=-1, keepdims=True)
        meta = meta + jnp.where(lane == e, ge, 0.0)
    meta_ref[...] = meta
    pay_ref[:, 0:ROW_TILES, :] = h2.reshape(tc, ROW_TILES, LANES)
    pay_ref[:, META_ROW, :] = meta
    pay_ref[:, META_ROW + 1:PAY_ROWS, :] = jnp.zeros((tc, PAY_ROWS - META_ROW - 1, LANES), F32)


def _route_call(x, mix, mod, n2g, w_out, rw, rb, tc):
    nb, L, D = x.shape
    n_tok = nb * L
    n_c = L // tc

    def mspec(k):
        return pl.BlockSpec((nb, D), lambda n, c, k=k: (0, k))

    consts = [n2g, w_out, rw, rb]
    return pl.pallas_call(
        functools.partial(_route_body, tc=tc),
        grid=(nb, n_c),
        in_specs=[pl.BlockSpec((1, tc, D), lambda n, c: (n, c, 0)),
                  pl.BlockSpec((1, tc, D), lambda n, c: (n, c, 0)),
                  mspec(2), mspec(3), mspec(4)] + [_const_spec(a.shape) for a in consts],
        out_specs=(pl.BlockSpec((1, tc, D), lambda n, c: (n, c, 0)),
                   pl.BlockSpec((tc, PAY_ROWS, LANES), lambda n, c: (n * n_c + c, 0, 0)),
                   pl.BlockSpec((tc, ROUTER_LANES), lambda n, c: (n * n_c + c, 0)),
                   pl.BlockSpec((1, ROUTER_LANES), lambda n, c: (0, 0))),
        out_shape=(jax.ShapeDtypeStruct((nb, L, D), F32),
                   jax.ShapeDtypeStruct((n_tok, PAY_ROWS, LANES), F32),
                   jax.ShapeDtypeStruct((n_tok, ROUTER_LANES), F32),
                   jax.ShapeDtypeStruct((1, ROUTER_LANES), jnp.int32)),
        scratch_shapes=[pltpu.VMEM((D, D), BF16),
                        pltpu.VMEM((tc, tc), BF16),
                        pltpu.VMEM((1, ROUTER_LANES), F32)],
        compiler_params=_cparams(("arbitrary", "arbitrary")),
        name="moe_route",
    )(x, mix, mod, mod, mod, *consts)


def _experts_body(src_ref, grp_ref, nch_ref, pay_hbm, wg_ref, wu_ref, wd_ref, moe_hbm,
                  wgb_scr, wub_scr, wdb_scr, xin_scr, out_scr, gsem, ssem, *, n_tok):
    s = pl.program_id(0)
    nch = nch_ref[0]
    slot_b = lax.rem(s, 2)

    def gather(chunk, b, i):
        t = jnp.minimum(src_ref[chunk * MOE_CHUNK + i], n_tok - 1)
        return pltpu.make_async_copy(pay_hbm.at[t], xin_scr.at[b, i], gsem.at[b])

    def scatter(chunk, b, i):
        return pltpu.make_async_copy(out_scr.at[b, i], moe_hbm.at[src_ref[chunk * MOE_CHUNK + i]], ssem.at[b])

    @pl.when(jnp.logical_and(s == 0, nch > 0))
    def _first_gather():
        def start_row(i, carry):
            gather(0, 0, i).start()
            return carry

        lax.fori_loop(0, MOE_CHUNK, start_row, 0, unroll=8)
        out_scr[1] = jnp.zeros((MOE_CHUNK, ROW_TILES, LANES), F32)
        spare = pltpu.make_async_copy(out_scr.at[1], moe_hbm.at[pl.ds(n_tok, MOE_CHUNK)], ssem.at[1])
        spare.start()
        spare.wait()

    fresh = jnp.logical_or(s == 0, grp_ref[s] != grp_ref[jnp.maximum(s - 1, 0)])

    @pl.when(jnp.logical_and(s < nch, fresh))
    def _load_group():
        wgb_scr[...] = wg_ref[...].astype(BF16)
        wub_scr[...] = wu_ref[...].astype(BF16)
        wdb_scr[...] = wd_ref[...].astype(BF16)

    @pl.when(s < nch)
    def _chunk():
        for i in range(MOE_CHUNK):
            gather(s, slot_b, i).wait()
        nxt = jnp.minimum(s + 1, nch - 1)
        for i in range(MOE_CHUNK):
            gather(nxt, 1 - slot_b, i).start()
        xin = xin_scr.at[slot_b]
        xb = xin[:, 0:ROW_TILES, :].reshape(MOE_CHUNK, D_MODEL).astype(BF16)
        meta = xin[:, META_ROW, :]
        acc = jnp.zeros((MOE_CHUNK, D_MODEL), F32)
        for e in range(EXPERTS_PER_GROUP):
            a = jnp.dot(xb, wgb_scr[e], preferred_element_type=F32)
            b = jnp.dot(xb, wub_scr[e], preferred_element_type=F32)
            hid = jax.nn.silu(a) * b * meta[:, e:e + 1]
            acc = acc + jnp.dot(hid.astype(BF16), wdb_scr[e], preferred_element_type=F32)

        @pl.when(s >= 2)
        def _reuse():
            for i in range(MOE_CHUNK):
                scatter(s, slot_b, i).wait()

        out_scr[slot_b] = acc.reshape(MOE_CHUNK, ROW_TILES, LANES)
        for i in range(MOE_CHUNK):
            scatter(s, slot_b, i).start()

    @pl.when(s == nch - 1)
    def _drain():
        for i in range(MOE_CHUNK):
            gather(s, 1 - slot_b, i).wait()
        for i in range(MOE_CHUNK):
            scatter(s, slot_b, i).wait()

        @pl.when(s >= 1)
        def _drain_prev():
            for i in range(MOE_CHUNK):
                scatter(s, 1 - slot_b, i).wait()


def _experts_call(src, grp, nch, pay, w_gate, w_up, w_down, n_tok):
    n_steps = grp.shape[0]
    E, D, F = w_gate.shape
    gsz = EXPERTS_PER_GROUP
    return pl.pallas_call(
        functools.partial(_experts_body, n_tok=n_tok),
        grid_spec=pltpu.PrefetchScalarGridSpec(
            num_scalar_prefetch=3,
            grid=(n_steps,),
            in_specs=[pl.BlockSpec(memory_space=pl.ANY),
                      pl.BlockSpec((gsz, D, F), lambda s, src, grp, nch: (grp[s], 0, 0)),
                      pl.BlockSpec((gsz, D, F), lambda s, src, grp, nch: (grp[s], 0, 0)),
                      pl.BlockSpec((gsz, F, D), lambda s, src, grp, nch: (grp[s], 0, 0))],
            out_specs=pl.BlockSpec(memory_space=pl.ANY),
            scratch_shapes=[pltpu.VMEM((gsz, D, F), BF16), pltpu.VMEM((gsz, D, F), BF16),
                            pltpu.VMEM((gsz, F, D), BF16),
                            pltpu.VMEM((2, MOE_CHUNK, PAY_ROWS, LANES), F32),
                            pltpu.VMEM((2, MOE_CHUNK, ROW_TILES, LANES), F32),
                            pltpu.SemaphoreType.DMA((2,)), pltpu.SemaphoreType.DMA((2,))]),
        out_shape=jax.ShapeDtypeStruct((n_tok + MOE_CHUNK, ROW_TILES, LANES), F32),
        compiler_params=_cparams(("arbitrary",)),
        name="moe_experts",
    )(src, grp, nch, pay, w_gate, w_up, w_down)


def _finish_body(x1_ref, moe_ref, g2_ref, fng_ref, y_ref):
    n = pl.program_id(0)
    moe = moe_ref[...].reshape(x1_ref.shape[1], D_MODEL)
    x2 = x1_ref[0] + g2_ref[pl.ds(n, 1), :] * moe
    y_ref[0] = _rms(x2, fng_ref[...])


def _finish_call(x1, moe, mod, fng, tc):
    nb, L, D = x1.shape
    n_c = L // tc
    return pl.pallas_call(
        _finish_body,
        grid=(nb, n_c),
        in_specs=[pl.BlockSpec((1, tc, D), lambda n, c: (n, c, 0)),
                  pl.BlockSpec((tc, ROW_TILES, LANES), lambda n, c: (n * n_c + c, 0, 0)),
                  pl.BlockSpec((nb, D), lambda n, c: (0, 5)),
                  _const_spec(fng.shape)],
        out_specs=pl.BlockSpec((1, tc, D), lambda n, c: (n, c, 0)),
        out_shape=jax.ShapeDtypeStruct((nb, L, D), F32),
        compiler_params=_cparams(("arbitrary", "arbitrary")),
        name="moe_finish",
    )(x1, moe, mod, fng)


def _invert_body(pos_ref, src_ref, *, n_tok, n_rows):
    def fill(j, carry):
        src_ref[j] = n_tok + lax.rem(j, MOE_CHUNK)
        return carry

    lax.fori_loop(0, n_rows, fill, 0, unroll=8)

    def put(i, carry):
        src_ref[pos_ref[i]] = i
        return carry

    lax.fori_loop(0, n_tok, put, 0, unroll=8)


def _invert(pos, n_rows):
    n_tok = pos.shape[0]
    return pl.pallas_call(
        functools.partial(_invert_body, n_tok=n_tok, n_rows=n_rows),
        in_specs=[pl.BlockSpec(memory_space=pltpu.SMEM)],
        out_specs=pl.BlockSpec(memory_space=pltpu.SMEM),
        out_shape=jax.ShapeDtypeStruct((n_rows,), jnp.int32),
        name="moe_invert",
    )(pos)


def _sorted_order(meta, counts, n_tok):
    n_steps = n_tok // MOE_CHUNK + N_EXPERT_GROUPS
    n_rows = n_steps * MOE_CHUNK
    g = meta[:, GROUP_LANE].astype(jnp.int32)
    slot = meta[:, SLOT_LANE].astype(jnp.int32)
    nch_g = (counts + MOE_CHUNK - 1) // MOE_CHUNK
    ends = jnp.cumsum(nch_g)
    first_row = (ends - nch_g) * MOE_CHUNK
    base = sum(jnp.where(g == k, first_row[k], 0) for k in range(N_EXPERT_GROUPS))
    pos = jnp.clip(base + slot, 0, n_rows - 1)
    src = _invert(pos, n_rows)
    total = ends[-1]
    s = jnp.clip(jnp.arange(n_steps, dtype=jnp.int32), 0, jnp.maximum(total - 1, 0))
    grp = jnp.minimum(jnp.sum((s[:, None] >= ends[None, :]).astype(jnp.int32), axis=1), N_EXPERT_GROUPS - 1)
    return src, grp.astype(jnp.int32), total.reshape(1).astype(jnp.int32)


def _stage3_prompt_sparse(x, mix, mod, n2g, fng, w_out, rw, rb, w_gate, w_up, w_down, tc):
    nb, L, D = x.shape
    n_tok = nb * L
    x1, pay, meta, cnt = _route_call(x, mix, mod, n2g, w_out, rw, rb, tc)
    src, grp, nch = _sorted_order(meta, cnt[0, :N_EXPERT_GROUPS], n_tok)
    moe = _experts_call(src, grp, nch, pay, w_gate, w_up, w_down, n_tok)
    return _finish_call(x1, moe, mod, fng, tc)


def _stage3_prompt_body(x_ref, mix_ref, g1_ref, sh2_ref, sc2_ref, g2_ref, *rest):
    wrefs, y_ref = rest[:-1], rest[-1]
    n = pl.program_id(0)
    mods = [r[pl.ds(n, 1), :] for r in (g1_ref, sh2_ref, sc2_ref, g2_ref)]
    y_ref[0] = _stage3_rows(x_ref[0], mix_ref[0], *mods, *wrefs)


def _stage3_prompt(x, mix, mod, wts, tc):
    nb, L, D = x.shape
    consts = list(wts)

    def mspec(k):
        return pl.BlockSpec((nb, D), lambda n, c, k=k: (0, k))

    return pl.pallas_call(
        _stage3_prompt_body,
        grid=(nb, L // tc),
        in_specs=[pl.BlockSpec((1, tc, D), lambda n, c: (n, c, 0)),
                  pl.BlockSpec((1, tc, D), lambda n, c: (n, c, 0)),
                  mspec(2), mspec(3), mspec(4), mspec(5)] + [_const_spec(a.shape) for a in consts],
        out_specs=pl.BlockSpec((1, tc, D), lambda n, c: (n, c, 0)),
        out_shape=jax.ShapeDtypeStruct((nb, L, D), F32),
        compiler_params=_cparams(("arbitrary", "arbitrary")),
        name="stage3_prompt",
    )(x, mix, mod, mod, mod, mod, *consts)


def _stage3_sample_body(x_ref, mix_ref, g1_ref, sh2_ref, sc2_ref, g2_ref, n2g_ref, fng_ref, wout_ref, rw_ref, rb_ref,
                        wg_ref, wu_ref, wd_ref, y_ref, x1_scr, h2b_scr, gates_scr, acc_scr, *, S, L):
    e = pl.program_id(0)
    R = S * L

    def rows(r):
        return jnp.concatenate([r[...]] * L, axis=0)

    @pl.when(e == 0)
    def _prologue():
        x_tm = jnp.concatenate([x_ref[:, t, :] for t in range(L)], axis=0)
        mixo = jnp.dot(mix_ref[...], wout_ref[...].astype(BF16), preferred_element_type=F32)
        x1 = x_tm + rows(g1_ref) * mixo
        h2 = _rms(x1, n2g_ref[...]) * (1.0 + rows(sc2_ref)) + rows(sh2_ref)
        h2_hi, _, _, gates, _ = _route(h2, rw_ref, rb_ref)
        x1_scr[...] = x1
        h2b_scr[...] = h2_hi
        gates_scr[...] = gates
        acc_scr[...] = jnp.zeros((R, D_MODEL), F32)

    lane = lax.broadcasted_iota(jnp.int32, (R, ROUTER_LANES), 1)
    ge = jnp.sum(jnp.where(lane == N_EXPERT_GROUPS + e, gates_scr[...], 0.0), axis=-1, keepdims=True)
    h2b = h2b_scr[...]
    a = jnp.dot(h2b, wg_ref[0].astype(BF16), preferred_element_type=F32)
    b = jnp.dot(h2b, wu_ref[0].astype(BF16), preferred_element_type=F32)
    hid = jax.nn.silu(a) * b * ge
    acc_scr[...] += jnp.dot(hid.astype(BF16), wd_ref[0].astype(BF16), preferred_element_type=F32)

    @pl.when(e == pl.num_programs(0) - 1)
    def _epilogue():
        y = _rms(x1_scr[...] + rows(g2_ref) * acc_scr[...], fng_ref[...])
        for t in range(L):
            y_ref[:, t, :] = y[t * S:(t + 1) * S, :]


def _stage3_sample(x, mix_tm, mod, n2g, fng, w_out, rw, rb, w_gate, w_up, w_down):
    ns, L, D = x.shape
    E, _, F = w_gate.shape
    R = ns * L
    consts = [n2g, fng, w_out, rw, rb]

    def mspec(k):
        return pl.BlockSpec((ns, D), lambda e, k=k: (0, k), pipeline_mode=pl.Buffered(1))

    return pl.pallas_call(
        functools.partial(_stage3_sample_body, S=ns, L=L),
        grid=(E,),
        in_specs=[_const_spec((ns, L, D)), _const_spec((R, D)),
                  mspec(2), mspec(3), mspec(4), mspec(5)] + [_const_spec(a.shape) for a in consts]
                 + [pl.BlockSpec((1, D, F), lambda e: (e, 0, 0)),
                    pl.BlockSpec((1, D, F), lambda e: (e, 0, 0)),
                    pl.BlockSpec((1, F, D), lambda e: (e, 0, 0))],
        out_specs=pl.BlockSpec((ns, L, D), lambda e: (0, 0, 0)),
        out_shape=jax.ShapeDtypeStruct((ns, L, D), F32),
        scratch_shapes=[pltpu.VMEM((R, D), F32), pltpu.VMEM((R, D), BF16),
                        pltpu.VMEM((R, ROUTER_LANES), F32), pltpu.VMEM((R, D), F32)],
        compiler_params=_cparams(("arbitrary",)),
        name="stage3_sample",
    )(x, mix_tm, mod, mod, mod, mod, *consts, w_gate, w_up, w_down)


def kernel(x_prompt, x_sample, c_prompt, c_sample, state_pool, state_ssm_re, state_ssm_im, w_ada, b_ada, norm1_g, w_in, pool_w, pool_scale, ssm_a_re, ssm_a_im, ssm_log_dt, ssm_b_re, ssm_b_im, ssm_c_re, ssm_c_im, ssm_d, glu_w, glu_b, w_out, norm2_g, router_w1, router_b1, router_w2, router_b2, exp_w_gate, exp_w_up, exp_w_down, final_norm_g):
    depth = w_ada.shape[0]
    assert depth == 1
    l = 0
    nb, L, D = x_prompt.shape
    ns, Ls, _ = x_sample.shape

    lb_re, lb_im, d_flat, wbre, wbim, wcre, wcim = _ssm_prep(
        ssm_a_re[l], ssm_a_im[l], ssm_log_dt[l],
        jnp.transpose(ssm_b_re[l], (0, 2, 1)), jnp.transpose(ssm_b_im[l], (0, 2, 1)),
        jnp.transpose(ssm_c_re[l], (0, 2, 1)), jnp.transpose(ssm_c_im[l], (0, 2, 1)), ssm_d[l])
    mix_wts = (pool_w[l], pool_scale[l].reshape(1, -1), lb_re, lb_im, wbre, wbim, wcre, wcim, d_flat,
               glu_w[l], glu_b[l].reshape(1, -1))

    rw = jnp.concatenate([router_w1[l], jnp.transpose(router_w2[l], (1, 0, 2)).reshape(D, N_EXPERTS)], axis=1)
    rw = jnp.pad(rw, ((0, 0), (0, ROUTER_LANES - rw.shape[1])))
    rb = jnp.concatenate([router_b1[l], router_b2[l].reshape(-1)])
    rb = jnp.pad(rb, (0, ROUTER_LANES - rb.shape[0])).reshape(1, -1)
    s3_wts = (norm2_g[l].reshape(1, -1), final_norm_g.reshape(1, -1), w_out[l].astype(BF16), rw, rb,
              exp_w_gate[l].astype(BF16), exp_w_up[l].astype(BF16), exp_w_down[l].astype(BF16))

    mod_p, mod_s = _adaln(c_prompt, c_sample, w_ada[l], b_ada[l])
    g1 = norm1_g[l].reshape(1, -1)

    mix_p, pool_p, hre_p, him_p = _mixer(x_prompt, mod_p, g1, w_in[l], None, mix_wts,
                                         Tt=128, Ts=32, start_pos=0, seq_major=True)
    y_p = _stage3_prompt(x_prompt, mix_p, mod_p, s3_wts, 512)

    state = (state_pool[l], state_ssm_re[l].reshape(ns, N_STATE), state_ssm_im[l].reshape(ns, N_STATE))
    mix_s, pool_s, hre_s, him_s = _mixer(x_sample, mod_s, g1, w_in[l], state, mix_wts,
                                         Tt=Ls, Ts=Ls // 2, start_pos=PAST_LEN, seq_major=False)
    y_s = _stage3_sample(x_sample, mix_s, mod_s, norm2_g[l].reshape(1, -1), final_norm_g.reshape(1, -1), w_out[l],
                         rw, rb, exp_w_gate[l], exp_w_up[l], exp_w_down[l])

    def st(a, n):
        return a.reshape(1, n, N_SSM_GROUPS, SSM_STATE)

    return (y_p, y_s, pool_p[None], pool_s[None], st(hre_p, nb), st(him_p, nb), st(hre_s, ns), st(him_s, ns))
```

```python
import functools

import jax
import jax.numpy as jnp
from jax import lax
from jax.experimental import pallas as pl
from jax.experimental.pallas import tpu as pltpu

D_MODEL = 1024
POOL_WIDTH = 512
SSM_WIDTH = 512
POOL_WINDOWS = (2, 4, 8, 16)
POOL_GROUP = 128
POOL_BUF = 15
HIST = 16
SSM_GROUP = 16
N_SSM_GROUPS = 32
SSM_STATE = 64
N_STATE = N_SSM_GROUPS * SSM_STATE
N_EXPERT_GROUPS = 4
EXPERTS_PER_GROUP = 4
N_EXPERTS = 16
EXPERT_HIDDEN = 256
N_MOD = 6
EPS = 1e-6
PAST_LEN = 16384

MXU_TILE = 256
LANES = 128
SUBLANES = 8
ROUTER_LANES = 128
VMEM_LIMIT = 60 * 1024 * 1024
N_STATE_TILES = N_STATE // MXU_TILE

F32 = jnp.float32
BF16 = jnp.bfloat16


def _cparams(sem):
    return pltpu.CompilerParams(dimension_semantics=sem, vmem_limit_bytes=VMEM_LIMIT)


def _const_spec(shape):
    nd = len(shape)
    return pl.BlockSpec(shape, lambda *_: (0,) * nd, pipeline_mode=pl.Buffered(1))


def _rms(x, g):
    return x * lax.rsqrt(jnp.mean(x * x, axis=-1, keepdims=True) + EPS) * g


def _ssm_prep_body(are_ref, aim_ref, ldt_ref, bre_ref, bim_ref, cre_ref, cim_ref, d_ref,
                   lbre_ref, lbim_ref, dflat_ref, wbre_ref, wbim_ref, wcre_ref, wcim_ref):
    G, H, P = N_SSM_GROUPS, SSM_GROUP, SSM_STATE
    a_re = are_ref[...]
    a_im = aim_ref[...]
    dt = jnp.exp(ldt_ref[...])
    mag = jnp.exp(a_re * dt)
    lb_re = mag * jnp.cos(a_im * dt)
    lb_im = mag * jnp.sin(a_im * dt)
    den = a_re * a_re + a_im * a_im
    nr = lb_re - 1.0
    ni = lb_im
    k_re = (nr * a_re + ni * a_im) / den
    k_im = (ni * a_re - nr * a_im) / den
    b_re = bre_ref[...]
    b_im = bim_ref[...]
    bb_re = k_re[:, None, :] * b_re - k_im[:, None, :] * b_im
    bb_im = k_re[:, None, :] * b_im + k_im[:, None, :] * b_re
    c_re = cre_ref[...]
    c_im = cim_ref[...]
    d = d_ref[...]

    zeros = jnp.zeros((N_STATE_TILES, MXU_TILE, MXU_TILE), BF16)
    wbre_ref[...] = zeros
    wbim_ref[...] = zeros
    wcre_ref[...] = zeros
    wcim_ref[...] = zeros
    g_per_tile = MXU_TILE // P
    g_per_blk = MXU_TILE // H
    for g in range(G):
        lbre_ref[:, g * P:(g + 1) * P] = lb_re[g:g + 1, :]
        lbim_ref[:, g * P:(g + 1) * P] = lb_im[g:g + 1, :]
        dflat_ref[:, g * H:(g + 1) * H] = d[g:g + 1, :]
        n, gi = divmod(g, g_per_tile)
        r0 = (g % g_per_blk) * H
        c0 = gi * P
        wbre_ref[n, r0:r0 + H, c0:c0 + P] = bb_re[g].astype(BF16)
        wbim_ref[n, r0:r0 + H, c0:c0 + P] = bb_im[g].astype(BF16)
        wcre_ref[n, c0:c0 + P, r0:r0 + H] = c_re[g].astype(BF16)
        wcim_ref[n, c0:c0 + P, r0:r0 + H] = (-c_im[g]).astype(BF16)


def _ssm_prep(a_re, a_im, log_dt, b_re_t, b_im_t, c_re_t, c_im_t, d):
    G = N_SSM_GROUPS
    tile = jax.ShapeDtypeStruct((N_STATE_TILES, MXU_TILE, MXU_TILE), BF16)
    return pl.pallas_call(
        _ssm_prep_body,
        out_shape=(jax.ShapeDtypeStruct((1, N_STATE), F32), jax.ShapeDtypeStruct((1, N_STATE), F32),
                   jax.ShapeDtypeStruct((1, SSM_WIDTH), F32), tile, tile, tile, tile),
        name="ssm_prep",
    )(a_re, a_im, log_dt.reshape(G, 1), b_re_t, b_im_t, c_re_t, c_im_t, d)


def _adaln_body(cp_ref, cs_ref, w_ref, b_ref, op_ref, os_ref):
    w = w_ref[...].astype(BF16)
    b = b_ref[...]
    op_ref[...] = jnp.dot(jax.nn.silu(cp_ref[...]).astype(BF16), w, preferred_element_type=F32) + b
    os_ref[...] = jnp.dot(jax.nn.silu(cs_ref[...]).astype(BF16), w, preferred_element_type=F32) + b


def _adaln(c_p, c_s, w_ada, b_ada):
    n_p, n_s = c_p.shape[0], c_s.shape[0]
    tn = 2 * D_MODEL
    return pl.pallas_call(
        _adaln_body,
        grid=(N_MOD * D_MODEL // tn,),
        in_specs=[pl.BlockSpec((n_p, D_MODEL), lambda j: (0, 0)),
                  pl.BlockSpec((n_s, D_MODEL), lambda j: (0, 0)),
                  pl.BlockSpec((D_MODEL, tn), lambda j: (0, j)),
                  pl.BlockSpec((1, tn), lambda j: (0, j))],
        out_specs=(pl.BlockSpec((n_p, tn), lambda j: (0, j)), pl.BlockSpec((n_s, tn), lambda j: (0, j))),
        out_shape=(jax.ShapeDtypeStruct((n_p, N_MOD * D_MODEL), F32),
                   jax.ShapeDtypeStruct((n_s, N_MOD * D_MODEL), F32)),
        compiler_params=_cparams(("arbitrary",)),
        name="adaln",
    )(c_p, c_s, w_ada, b_ada.reshape(1, -1))


def _mixer_body(*refs, S, Tt, Ts, start_pos, seq_major, has_state):
    refs = list(refs)
    x_ref, sh_ref, sc_ref, g_ref, win_ref = refs[:5]
    k = 5
    if has_state:
        buf0_ref, hre0_ref, him0_ref = refs[k:k + 3]
        k += 3
    (poolw_ref, pscale_ref, lbre_ref, lbim_ref, wbre_ref, wbim_ref, wcre_ref, wcim_ref, d_ref, gluw_ref,
     glub_ref) = refs[k:k + 11]
    k += 11
    mix_ref, newbuf_ref, hre_out_ref, him_out_ref = refs[k:k + 4]
    z_scr, sre_scr, sim_scr, hre_scr, him_scr, winb_scr, glub_scr = refs[k + 4:]

    i = pl.program_id(0)
    R = Tt * S
    HR = HIST * S
    D = D_MODEL

    @pl.when(i == 0)
    def _init():
        z_scr[0:S, :] = jnp.zeros((S, POOL_WIDTH), F32)
        if has_state:
            for j in range(POOL_BUF):
                z_scr[(j + 1) * S:(j + 2) * S, :] = buf0_ref[:, j, :]
            hre_scr[...] = hre0_ref[...]
            him_scr[...] = him0_ref[...]
        else:
            z_scr[S:HR, :] = jnp.zeros((HR - S, POOL_WIDTH), F32)
            hre_scr[...] = jnp.zeros((S, N_STATE), F32)
            him_scr[...] = jnp.zeros((S, N_STATE), F32)
        winb_scr[...] = win_ref[...].astype(BF16)
        glub_scr[...] = gluw_ref[...].astype(BF16)

    g = g_ref[...]
    Rs = Ts * S
    CB = 512
    n_tiles = S // SUBLANES
    n_ct = SSM_WIDTH // MXU_TILE
    k_per = N_STATE_TILES // n_ct
    for sub in range(Tt // Ts):
        t0 = sub * Ts
        r_lo = sub * Rs

        if seq_major:
            x3 = x_ref[:, t0:t0 + Ts, :]
            h3 = _rms(x3, g) * (1.0 + sc_ref[...][:, None, :]) + sh_ref[...][:, None, :]
            u_nm = jnp.dot(h3.reshape(Rs, D).astype(BF16), winb_scr[...], preferred_element_type=F32)
            u = jnp.swapaxes(u_nm.reshape(S, Ts, D), 0, 1).reshape(Rs, D)
        else:
            x_tm = jnp.concatenate([x_ref[:, t, :] for t in range(t0, t0 + Ts)], axis=0)
            h3 = _rms(x_tm, g).reshape(Ts, S, D) * (1.0 + sc_ref[...]) + sh_ref[...]
            u = jnp.dot(h3.reshape(Rs, D).astype(BF16), winb_scr[...], preferred_element_type=F32)

        up = u[:, 0:POOL_WIDTH]
        us = u[:, POOL_WIDTH:D]
        z_scr[HR + r_lo:HR + r_lo + Rs, :] = up

        row = lax.broadcasted_iota(jnp.int32, (Rs, 1), 0)
        pos = start_pos + i * Tt + t0 + lax.shift_right_logical(row, S.bit_length() - 1)
        outs = []
        for kk, w in enumerate(POOL_WINDOWS):
            lo, hi = kk * POOL_GROUP, (kk + 1) * POOL_GROUP
            cur = z_scr[r_lo:r_lo + HR + Rs, lo:hi]
            step = 1
            while step < w:
                cur = cur[step * S:, :] + cur[:cur.shape[0] - step * S, :]
                step *= 2
            s = cur[cur.shape[0] - Rs:, :]
            cnt = jnp.minimum(w, pos + 1).astype(F32)
            pooled = s / cnt - up[:, lo:hi]
            mixed = jnp.dot(pooled.astype(BF16), poolw_ref[kk].astype(BF16), preferred_element_type=F32)
            outs.append(mixed * pscale_ref[:, lo:hi])

        usb = us.astype(BF16)
        for n in range(N_STATE_TILES):
            kb = (n * MXU_TILE // SSM_STATE * SSM_GROUP) // MXU_TILE
            lhs = usb[:, kb * MXU_TILE:(kb + 1) * MXU_TILE]
            cols = slice(n * MXU_TILE, (n + 1) * MXU_TILE)
            sre_scr[r_lo:r_lo + Rs, cols] = jnp.dot(lhs, wbre_ref[n], preferred_element_type=F32)
            sim_scr[r_lo:r_lo + Rs, cols] = jnp.dot(lhs, wbim_ref[n], preferred_element_type=F32)

        for cb in range(N_STATE // CB):
            c0 = cb * CB
            lr = jnp.broadcast_to(lbre_ref[:, c0:c0 + CB], (SUBLANES, CB))
            li = jnp.broadcast_to(lbim_ref[:, c0:c0 + CB], (SUBLANES, CB))

            def scan_tile(s0, c0=c0, lr=lr, li=li, r_lo=r_lo):
                hr = hre_scr[pl.ds(s0, SUBLANES), c0:c0 + CB]
                hi_ = him_scr[pl.ds(s0, SUBLANES), c0:c0 + CB]
                for t in range(Ts):
                    r0 = r_lo + t * S + s0
                    br = sre_scr[pl.ds(r0, SUBLANES), c0:c0 + CB]
                    bi = sim_scr[pl.ds(r0, SUBLANES), c0:c0 + CB]
                    hr, hi_ = lr * hr - li * hi_ + br, lr * hi_ + li * hr + bi
                    sre_scr[pl.ds(r0, SUBLANES), c0:c0 + CB] = hr
                    sim_scr[pl.ds(r0, SUBLANES), c0:c0 + CB] = hi_
                hre_scr[pl.ds(s0, SUBLANES), c0:c0 + CB] = hr
                him_scr[pl.ds(s0, SUBLANES), c0:c0 + CB] = hi_

            if n_tiles == 1:
                scan_tile(0)
            else:
                def tile_body(j, carry, scan_tile=scan_tile):
                    scan_tile(pl.multiple_of(j * SUBLANES, SUBLANES))
                    return carry

                lax.fori_loop(0, n_tiles, tile_body, 0)

        ys = []
        for m in range(n_ct):
            acc = d_ref[:, m * MXU_TILE:(m + 1) * MXU_TILE] * us[:, m * MXU_TILE:(m + 1) * MXU_TILE]
            for kk in range(k_per):
                kt = m * k_per + kk
                cols = slice(kt * MXU_TILE, (kt + 1) * MXU_TILE)
                acc = acc + jnp.dot(sre_scr[r_lo:r_lo + Rs, cols].astype(BF16), wcre_ref[kt],
                                    preferred_element_type=F32)
                acc = acc + jnp.dot(sim_scr[r_lo:r_lo + Rs, cols].astype(BF16), wcim_ref[kt],
                                    preferred_element_type=F32)
            ys.append(acc)
        y = jnp.concatenate(ys, axis=-1)
        gl = jax.nn.gelu(y)
        gate = jax.nn.sigmoid(jnp.dot(gl.astype(BF16), glub_scr[...], preferred_element_type=F32) + glub_ref[...])
        outs.append(gl * gate)

        mix_tm = jnp.concatenate(outs, axis=-1)
        if seq_major:
            mix_ref[:, t0:t0 + Ts, :] = jnp.swapaxes(mix_tm.reshape(Ts, S, D), 0, 1).astype(mix_ref.dtype)
        else:
            mix_ref[r_lo:r_lo + Rs, :] = mix_tm.astype(mix_ref.dtype)

    for j in range(POOL_BUF):
        r0 = (Tt + 1 + j) * S
        newbuf_ref[:, j, :] = z_scr[r0:r0 + S, :]
    hist = z_scr[R:R + HR, :]
    z_scr[0:HR, :] = hist
    hre_out_ref[...] = hre_scr[...]
    him_out_ref[...] = him_scr[...]


def _mixer(x, mod, g1, w_in, state, wts, *, Tt, Ts, start_pos, seq_major):
    S, L, D = x.shape
    R = Tt * S
    has_state = state is not None
    consts = [g1, w_in] + (list(state) if has_state else []) + list(wts)
    x_spec = pl.BlockSpec((S, Tt, D), lambda i: (0, i, 0))
    mod_specs = [pl.BlockSpec((S, D), lambda i: (0, 0)), pl.BlockSpec((S, D), lambda i: (0, 1))]
    if seq_major:
        mix_spec = pl.BlockSpec((S, Tt, D), lambda i: (0, i, 0))
        mix_shape = jax.ShapeDtypeStruct((S, L, D), BF16)
    else:
        assert Tt == L
        mix_spec = pl.BlockSpec((R, D), lambda i: (0, 0))
        mix_shape = jax.ShapeDtypeStruct((L * S, D), BF16)
    body = functools.partial(_mixer_body, S=S, Tt=Tt, Ts=Ts, start_pos=start_pos, seq_major=seq_major,
                             has_state=has_state)
    return pl.pallas_call(
        body,
        grid=(L // Tt,),
        in_specs=[x_spec] + mod_specs + [_const_spec(a.shape) for a in consts],
        out_specs=(mix_spec,
                   pl.BlockSpec((S, POOL_BUF, POOL_WIDTH), lambda i: (0, 0, 0)),
                   pl.BlockSpec((S, N_STATE), lambda i: (0, 0)),
                   pl.BlockSpec((S, N_STATE), lambda i: (0, 0))),
        out_shape=(mix_shape,
                   jax.ShapeDtypeStruct((S, POOL_BUF, POOL_WIDTH), F32),
                   jax.ShapeDtypeStruct((S, N_STATE), F32),
                   jax.ShapeDtypeStruct((S, N_STATE), F32)),
        scratch_shapes=[pltpu.VMEM(((HIST + Tt) * S, POOL_WIDTH), F32),
                        pltpu.VMEM((R, N_STATE), F32),
                        pltpu.VMEM((R, N_STATE), F32),
                        pltpu.VMEM((S, N_STATE), F32),
                        pltpu.VMEM((S, N_STATE), F32),
                        pltpu.VMEM((D, D), BF16),
                        pltpu.VMEM((SSM_WIDTH, SSM_WIDTH), BF16)],
        compiler_params=_cparams(("arbitrary",)),
        name="mixer_S%d" % S,
    )(x, mod, mod, *consts)


def _split_bf16(v):
    hi = v.astype(BF16)
    lo = (v - hi.astype(F32)).astype(BF16)
    return hi, lo


def _route(h2, rw_ref, rb_ref):
    R = h2.shape[0]
    h2_hi, h2_lo = _split_bf16(h2)
    rw_hi, rw_lo = _split_bf16(rw_ref[...])
    logits = (jnp.dot(h2_hi, rw_hi, preferred_element_type=F32)
              + jnp.dot(h2_lo, rw_hi, preferred_element_type=F32)
              + jnp.dot(h2_hi, rw_lo, preferred_element_type=F32)) + rb_ref[...]
    lane = lax.broadcasted_iota(jnp.int32, (R, ROUTER_LANES), 1).astype(F32)
    ninf = jnp.float32(-jnp.inf)
    none = jnp.float32(ROUTER_LANES)
    is_g = lane < N_EXPERT_GROUPS
    l1 = jnp.where(is_g, logits, ninf)
    m1 = jnp.max(l1, axis=-1, keepdims=True)
    gidx = jnp.min(jnp.where(l1 == m1, lane, none), axis=-1, keepdims=True)
    p_top = 1.0 / jnp.sum(jnp.where(is_g, jnp.exp(logits - m1), 0.0), axis=-1, keepdims=True)
    e_lo = N_EXPERT_GROUPS + gidx * EXPERTS_PER_GROUP
    sel = (lane >= e_lo) & (lane < e_lo + EXPERTS_PER_GROUP)
    l2 = jnp.where(sel, logits, ninf)
    va = jnp.max(l2, axis=-1, keepdims=True)
    ia = jnp.min(jnp.where(l2 == va, lane, none), axis=-1, keepdims=True)
    l2b = jnp.where(lane == ia, ninf, l2)
    vb = jnp.max(l2b, axis=-1, keepdims=True)
    ib = jnp.min(jnp.where(l2b == vb, lane, none), axis=-1, keepdims=True)
    eb = jnp.exp(vb - va)
    den = 1.0 + eb
    gates = jnp.where(lane == ia, (1.0 / den) * p_top, 0.0) + jnp.where(lane == ib, (eb / den) * p_top, 0.0)
    return h2_hi, gates


def _stage3_rows(x, mix, g1, sh2, sc2, g2, n2g_ref, fng_ref, wout_ref, rw_ref, rb_ref, wg_ref, wu_ref, wd_ref):
    R = x.shape[0]
    mixo = jnp.dot(mix, wout_ref[...], preferred_element_type=F32)
    x1 = x + g1 * mixo
    h2 = _rms(x1, n2g_ref[...]) * (1.0 + sc2) + sh2
    h2_hi, gates = _route(h2, rw_ref, rb_ref)

    acc = jnp.zeros((R, D_MODEL), F32)
    for e in range(N_EXPERTS):
        a = jnp.dot(h2_hi, wg_ref[e], preferred_element_type=F32)
        b = jnp.dot(h2_hi, wu_ref[e], preferred_element_type=F32)
        ge = gates[:, N_EXPERT_GROUPS + e:N_EXPERT_GROUPS + e + 1]
        hid = jax.nn.silu(a) * b * ge
        acc = acc + jnp.dot(hid.astype(BF16), wd_ref[e], preferred_element_type=F32)
    x2 = x1 + g2 * acc
    return _rms(x2, fng_ref[...])


def _stage3_prompt_body(x_ref, mix_ref, g1_ref, sh2_ref, sc2_ref, g2_ref, n2g_ref, fng_ref, wout_ref, rw_ref, rb_ref,
                        wg_hbm, wu_hbm, wd_hbm, y_ref, woutb_scr, wgb_scr, wub_scr, wdb_scr, stg_scr, std_scr, sem):
    n, c = pl.program_id(0), pl.program_id(1)

    @pl.when((n == 0) & (c == 0))
    def _load_weights():
        woutb_scr[...] = wout_ref[...].astype(BF16)
        for src, stage, dst in ((wg_hbm, stg_scr, wgb_scr), (wu_hbm, stg_scr, wub_scr), (wd_hbm, std_scr, wdb_scr)):
            def copy(e, src=src, stage=stage):
                return pltpu.make_async_copy(src.at[e], stage.at[e % 2], sem.at[e % 2])

            copy(0).start()
            for e in range(N_EXPERTS):
                if e + 1 < N_EXPERTS:
                    copy(e + 1).start()
                copy(e).wait()
                dst[e] = stage[e % 2].astype(BF16)

    mods = [r[pl.ds(n, 1), :] for r in (g1_ref, sh2_ref, sc2_ref, g2_ref)]
    y_ref[0] = _stage3_rows(x_ref[0], mix_ref[0], *mods, n2g_ref, fng_ref, woutb_scr, rw_ref, rb_ref,
                            wgb_scr, wub_scr, wdb_scr)


def _stage3_prompt(x, mix, mod, n2g, fng, w_out, rw, rb, w_gate, w_up, w_down, tc):
    nb, L, D = x.shape
    E, _, F = w_gate.shape
    consts = [n2g, fng, w_out, rw, rb]
    any_spec = pl.BlockSpec(memory_space=pl.ANY)

    def mspec(k):
        return pl.BlockSpec((nb, D), lambda n, c, k=k: (0, k))

    return pl.pallas_call(
        _stage3_prompt_body,
        grid=(nb, L // tc),
        in_specs=[pl.BlockSpec((1, tc, D), lambda n, c: (n, c, 0)),
                  pl.BlockSpec((1, tc, D), lambda n, c: (n, c, 0)),
                  mspec(2), mspec(3), mspec(4), mspec(5)] + [_const_spec(a.shape) for a in consts]
                 + [any_spec, any_spec, any_spec],
        out_specs=pl.BlockSpec((1, tc, D), lambda n, c: (n, c, 0)),
        out_shape=jax.ShapeDtypeStruct((nb, L, D), F32),
        scratch_shapes=[pltpu.VMEM((D, D), BF16),
                        pltpu.VMEM((E, D, F), BF16), pltpu.VMEM((E, D, F), BF16), pltpu.VMEM((E, F, D), BF16),
                        pltpu.VMEM((2, D, F), F32), pltpu.VMEM((2, F, D), F32),
                        pltpu.SemaphoreType.DMA((2,))],
        compiler_params=_cparams(("arbitrary", "arbitrary")),
        name="stage3_prompt",
    )(x, mix, mod, mod, mod, mod, *consts, w_gate, w_up, w_down)


def _stage3_sample_body(x_ref, mix_ref, g1_ref, sh2_ref, sc2_ref, g2_ref, n2g_ref, fng_ref, wout_ref, rw_ref, rb_ref,
                        wg_ref, wu_ref, wd_ref, y_ref, x1_scr, h2b_scr, gates_scr, acc_scr, *, S, L):
    e = pl.program_id(0)
    R = S * L

    def rows(r):
        return jnp.concatenate([r[...]] * L, axis=0)

    @pl.when(e == 0)
    def _prologue():
        x_tm = jnp.concatenate([x_ref[:, t, :] for t in range(L)], axis=0)
        mixo = jnp.dot(mix_ref[...], wout_ref[...].astype(BF16), preferred_element_type=F32)
        x1 = x_tm + rows(g1_ref) * mixo
        h2 = _rms(x1, n2g_ref[...]) * (1.0 + rows(sc2_ref)) + rows(sh2_ref)
        h2_hi, gates = _route(h2, rw_ref, rb_ref)
        x1_scr[...] = x1
        h2b_scr[...] = h2_hi
        gates_scr[...] = gates
        acc_scr[...] = jnp.zeros((R, D_MODEL), F32)

    lane = lax.broadcasted_iota(jnp.int32, (R, ROUTER_LANES), 1)
    ge = jnp.sum(jnp.where(lane == N_EXPERT_GROUPS + e, gates_scr[...], 0.0), axis=-1, keepdims=True)
    h2b = h2b_scr[...]
    a = jnp.dot(h2b, wg_ref[0].astype(BF16), preferred_element_type=F32)
    b = jnp.dot(h2b, wu_ref[0].astype(BF16), preferred_element_type=F32)
    hid = jax.nn.silu(a) * b * ge
    acc_scr[...] += jnp.dot(hid.astype(BF16), wd_ref[0].astype(BF16), preferred_element_type=F32)

    @pl.when(e == pl.num_programs(0) - 1)
    def _epilogue():
        y = _rms(x1_scr[...] + rows(g2_ref) * acc_scr[...], fng_ref[...])
        for t in range(L):
            y_ref[:, t, :] = y[t * S:(t + 1) * S, :]


def _stage3_sample(x, mix_tm, mod, n2g, fng, w_out, rw, rb, w_gate, w_up, w_down):
    ns, L, D = x.shape
    E, _, F = w_gate.shape
    R = ns * L
    consts = [n2g, fng, w_out, rw, rb]

    def mspec(k):
        return pl.BlockSpec((ns, D), lambda e, k=k: (0, k), pipeline_mode=pl.Buffered(1))

    return pl.pallas_call(
        functools.partial(_stage3_sample_body, S=ns, L=L),
        grid=(E,),
        in_specs=[_const_spec((ns, L, D)), _const_spec((R, D)),
                  mspec(2), mspec(3), mspec(4), mspec(5)] + [_const_spec(a.shape) for a in consts]
                 + [pl.BlockSpec((1, D, F), lambda e: (e, 0, 0)),
                    pl.BlockSpec((1, D, F), lambda e: (e, 0, 0)),
                    pl.BlockSpec((1, F, D), lambda e: (e, 0, 0))],
        out_specs=pl.BlockSpec((ns, L, D), lambda e: (0, 0, 0)),
        out_shape=jax.ShapeDtypeStruct((ns, L, D), F32),
        scratch_shapes=[pltpu.VMEM((R, D), F32), pltpu.VMEM((R, D), BF16),
                        pltpu.VMEM((R, ROUTER_LANES), F32), pltpu.VMEM((R, D), F32)],
        compiler_params=_cparams(("arbitrary",)),
        name="stage3_sample",
    )(x, mix_tm, mod, mod, mod, mod, *consts, w_gate, w_up, w_down)


def kernel(x_prompt, x_sample, c_prompt, c_sample, state_pool, state_ssm_re, state_ssm_im, w_ada, b_ada, norm1_g, w_in, pool_w, pool_scale, ssm_a_re, ssm_a_im, ssm_log_dt, ssm_b_re, ssm_b_im, ssm_c_re, ssm_c_im, ssm_d, glu_w, glu_b, w_out, norm2_g, router_w1, router_b1, router_w2, router_b2, exp_w_gate, exp_w_up, exp_w_down, final_norm_g):
    depth = w_ada.shape[0]
    assert depth == 1
    l = 0
    nb, L, D = x_prompt.shape
    ns, Ls, _ = x_sample.shape

    lb_re, lb_im, d_flat, wbre, wbim, wcre, wcim = _ssm_prep(
        ssm_a_re[l], ssm_a_im[l], ssm_log_dt[l],
        jnp.transpose(ssm_b_re[l], (0, 2, 1)), jnp.transpose(ssm_b_im[l], (0, 2, 1)),
        jnp.transpose(ssm_c_re[l], (0, 2, 1)), jnp.transpose(ssm_c_im[l], (0, 2, 1)), ssm_d[l])
    mix_wts = (pool_w[l], pool_scale[l].reshape(1, -1), lb_re, lb_im, wbre, wbim, wcre, wcim, d_flat,
               glu_w[l], glu_b[l].reshape(1, -1))

    rw = jnp.concatenate([router_w1[l], jnp.transpose(router_w2[l], (1, 0, 2)).reshape(D, N_EXPERTS)], axis=1)
    rw = jnp.pad(rw, ((0, 0), (0, ROUTER_LANES - rw.shape[1])))
    rb = jnp.concatenate([router_b1[l], router_b2[l].reshape(-1)])
    rb = jnp.pad(rb, (0, ROUTER_LANES - rb.shape[0])).reshape(1, -1)
    s3_wts = (norm2_g[l].reshape(1, -1), final_norm_g.reshape(1, -1), w_out[l], rw, rb,
              exp_w_gate[l], exp_w_up[l], exp_w_down[l])

    mod_p, mod_s = _adaln(c_prompt, c_sample, w_ada[l], b_ada[l])
    g1 = norm1_g[l].reshape(1, -1)

    mix_p, pool_p, hre_p, him_p = _mixer(x_prompt, mod_p, g1, w_in[l], None, mix_wts,
                                         Tt=128, Ts=32, start_pos=0, seq_major=True)
    y_p = _stage3_prompt(x_prompt, mix_p, mod_p, *s3_wts, 512)

    state = (state_pool[l], state_ssm_re[l].reshape(ns, N_STATE), state_ssm_im[l].reshape(ns, N_STATE))
    mix_s, pool_s, hre_s, him_s = _mixer(x_sample, mod_s, g1, w_in[l], state, mix_wts,
                                         Tt=Ls, Ts=Ls // 2, start_pos=PAST_LEN, seq_major=False)
    y_s = _stage3_sample(x_sample, mix_s, mod_s, *s3_wts)

    def st(a, n):
        return a.reshape(1, n, N_SSM_GROUPS, SSM_STATE)

    return (y_p, y_s, pool_p[None], pool_s[None], st(hre_p, nb), st(him_p, nb), st(hre_s, ns), st(him_s, ns))
```

```python
import functools

import jax
import jax.numpy as jnp
from jax import lax
from jax.experimental import pallas as pl
from jax.experimental.pallas import tpu as pltpu

D_MODEL = 1024
POOL_WIDTH = 512
SSM_WIDTH = 512
POOL_WINDOWS = (2, 4, 8, 16)
POOL_GROUP = 128
POOL_BUF = 15
HIST = 16
SSM_GROUP = 16
N_SSM_GROUPS = 32
SSM_STATE = 64
N_STATE = N_SSM_GROUPS * SSM_STATE
N_EXPERT_GROUPS = 4
EXPERTS_PER_GROUP = 4
N_EXPERTS = 16
EXPERT_HIDDEN = 256
N_MOD = 6
EPS = 1e-6
PAST_LEN = 16384

MXU_TILE = 256
LANES = 128
SUBLANES = 8
ROUTER_LANES = 128
VMEM_LIMIT = 60 * 1024 * 1024
N_STATE_TILES = N_STATE // MXU_TILE

F32 = jnp.float32
BF16 = jnp.bfloat16


def _cparams(sem):
    return pltpu.CompilerParams(dimension_semantics=sem, vmem_limit_bytes=VMEM_LIMIT)


def _const_spec(shape):
    nd = len(shape)
    return pl.BlockSpec(shape, lambda *_: (0,) * nd, pipeline_mode=pl.Buffered(1))


def _rms(x, g):
    return x * lax.rsqrt(jnp.mean(x * x, axis=-1, keepdims=True) + EPS) * g


def _ssm_prep_body(are_ref, aim_ref, ldt_ref, bre_ref, bim_ref, cre_ref, cim_ref, d_ref,
                   lbre_ref, lbim_ref, dflat_ref, wbre_ref, wbim_ref, wcre_ref, wcim_ref):
    G, H, P = N_SSM_GROUPS, SSM_GROUP, SSM_STATE
    a_re = are_ref[...]
    a_im = aim_ref[...]
    dt = jnp.exp(ldt_ref[...])
    mag = jnp.exp(a_re * dt)
    lb_re = mag * jnp.cos(a_im * dt)
    lb_im = mag * jnp.sin(a_im * dt)
    den = a_re * a_re + a_im * a_im
    nr = lb_re - 1.0
    ni = lb_im
    k_re = (nr * a_re + ni * a_im) / den
    k_im = (ni * a_re - nr * a_im) / den
    b_re = bre_ref[...]
    b_im = bim_ref[...]
    bb_re = k_re[:, None, :] * b_re - k_im[:, None, :] * b_im
    bb_im = k_re[:, None, :] * b_im + k_im[:, None, :] * b_re
    c_re = cre_ref[...]
    c_im = cim_ref[...]
    d = d_ref[...]

    zeros = jnp.zeros((N_STATE_TILES, MXU_TILE, MXU_TILE), BF16)
    wbre_ref[...] = zeros
    wbim_ref[...] = zeros
    wcre_ref[...] = zeros
    wcim_ref[...] = zeros
    g_per_tile = MXU_TILE // P
    g_per_blk = MXU_TILE // H
    for g in range(G):
        lbre_ref[:, g * P:(g + 1) * P] = lb_re[g:g + 1, :]
        lbim_ref[:, g * P:(g + 1) * P] = lb_im[g:g + 1, :]
        dflat_ref[:, g * H:(g + 1) * H] = d[g:g + 1, :]
        n, gi = divmod(g, g_per_tile)
        r0 = (g % g_per_blk) * H
        c0 = gi * P
        wbre_ref[n, r0:r0 + H, c0:c0 + P] = bb_re[g].astype(BF16)
        wbim_ref[n, r0:r0 + H, c0:c0 + P] = bb_im[g].astype(BF16)
        wcre_ref[n, c0:c0 + P, r0:r0 + H] = c_re[g].astype(BF16)
        wcim_ref[n, c0:c0 + P, r0:r0 + H] = (-c_im[g]).astype(BF16)


def _ssm_prep(a_re, a_im, log_dt, b_re_t, b_im_t, c_re_t, c_im_t, d):
    G = N_SSM_GROUPS
    tile = jax.ShapeDtypeStruct((N_STATE_TILES, MXU_TILE, MXU_TILE), BF16)
    return pl.pallas_call(
        _ssm_prep_body,
        out_shape=(jax.ShapeDtypeStruct((1, N_STATE), F32), jax.ShapeDtypeStruct((1, N_STATE), F32),
                   jax.ShapeDtypeStruct((1, SSM_WIDTH), F32), tile, tile, tile, tile),
        name="ssm_prep",
    )(a_re, a_im, log_dt.reshape(G, 1), b_re_t, b_im_t, c_re_t, c_im_t, d)


def _adaln_body(cp_ref, cs_ref, w_ref, b_ref, op_ref, os_ref):
    w = w_ref[...].astype(BF16)
    b = b_ref[...]
    op_ref[...] = jnp.dot(jax.nn.silu(cp_ref[...]).astype(BF16), w, preferred_element_type=F32) + b
    os_ref[...] = jnp.dot(jax.nn.silu(cs_ref[...]).astype(BF16), w, preferred_element_type=F32) + b


def _adaln(c_p, c_s, w_ada, b_ada):
    n_p, n_s = c_p.shape[0], c_s.shape[0]
    tn = D_MODEL
    return pl.pallas_call(
        _adaln_body,
        grid=(N_MOD * D_MODEL // tn,),
        in_specs=[pl.BlockSpec((n_p, D_MODEL), lambda j: (0, 0)),
                  pl.BlockSpec((n_s, D_MODEL), lambda j: (0, 0)),
                  pl.BlockSpec((D_MODEL, tn), lambda j: (0, j)),
                  pl.BlockSpec((1, tn), lambda j: (0, j))],
        out_specs=(pl.BlockSpec((n_p, tn), lambda j: (0, j)), pl.BlockSpec((n_s, tn), lambda j: (0, j))),
        out_shape=(jax.ShapeDtypeStruct((n_p, N_MOD * D_MODEL), F32),
                   jax.ShapeDtypeStruct((n_s, N_MOD * D_MODEL), F32)),
        compiler_params=_cparams(("arbitrary",)),
        name="adaln",
    )(c_p, c_s, w_ada, b_ada.reshape(1, -1))


def _mixer_body(*refs, S, Tt, Ts, start_pos, seq_major, has_state):
    refs = list(refs)
    x_ref, sh_ref, sc_ref, g_ref, win_ref = refs[:5]
    k = 5
    if has_state:
        buf0_ref, hre0_ref, him0_ref = refs[k:k + 3]
        k += 3
    (poolw_ref, pscale_ref, lbre_ref, lbim_ref, wbre_ref, wbim_ref, wcre_ref, wcim_ref, d_ref, gluw_ref,
     glub_ref) = refs[k:k + 11]
    k += 11
    mix_ref, newbuf_ref, hre_out_ref, him_out_ref = refs[k:k + 4]
    z_scr, sre_scr, sim_scr, hre_scr, him_scr, winb_scr, glub_scr = refs[k + 4:]

    i = pl.program_id(0)
    R = Tt * S
    HR = HIST * S
    D = D_MODEL

    @pl.when(i == 0)
    def _init():
        z_scr[0:S, :] = jnp.zeros((S, POOL_WIDTH), F32)
        if has_state:
            for j in range(POOL_BUF):
                z_scr[(j + 1) * S:(j + 2) * S, :] = buf0_ref[:, j, :]
            hre_scr[...] = hre0_ref[...]
            him_scr[...] = him0_ref[...]
        else:
            z_scr[S:HR, :] = jnp.zeros((HR - S, POOL_WIDTH), F32)
            hre_scr[...] = jnp.zeros((S, N_STATE), F32)
            him_scr[...] = jnp.zeros((S, N_STATE), F32)
        winb_scr[...] = win_ref[...].astype(BF16)
        glub_scr[...] = gluw_ref[...].astype(BF16)

    g = g_ref[...]
    Rs = Ts * S
    CB = 512
    n_tiles = S // SUBLANES
    n_ct = SSM_WIDTH // MXU_TILE
    k_per = N_STATE_TILES // n_ct
    for sub in range(Tt // Ts):
        t0 = sub * Ts
        r_lo = sub * Rs

        if seq_major:
            x3 = x_ref[:, t0:t0 + Ts, :]
            h3 = _rms(x3, g) * (1.0 + sc_ref[...][:, None, :]) + sh_ref[...][:, None, :]
            u_nm = jnp.dot(h3.reshape(Rs, D).astype(BF16), winb_scr[...], preferred_element_type=F32)
            u = jnp.swapaxes(u_nm.reshape(S, Ts, D), 0, 1).reshape(Rs, D)
        else:
            x_tm = jnp.concatenate([x_ref[:, t, :] for t in range(t0, t0 + Ts)], axis=0)
            h3 = _rms(x_tm, g).reshape(Ts, S, D) * (1.0 + sc_ref[...]) + sh_ref[...]
            u = jnp.dot(h3.reshape(Rs, D).astype(BF16), winb_scr[...], preferred_element_type=F32)

        up = u[:, 0:POOL_WIDTH]
        us = u[:, POOL_WIDTH:D]
        z_scr[HR + r_lo:HR + r_lo + Rs, :] = up

        row = lax.broadcasted_iota(jnp.int32, (Rs, 1), 0)
        pos = start_pos + i * Tt + t0 + lax.shift_right_logical(row, S.bit_length() - 1)
        outs = []
        for kk, w in enumerate(POOL_WINDOWS):
            lo, hi = kk * POOL_GROUP, (kk + 1) * POOL_GROUP
            cur = z_scr[r_lo:r_lo + HR + Rs, lo:hi]
            step = 1
            while step < w:
                cur = cur[step * S:, :] + cur[:cur.shape[0] - step * S, :]
                step *= 2
            s = cur[cur.shape[0] - Rs:, :]
            cnt = jnp.minimum(w, pos + 1).astype(F32)
            pooled = s / cnt - up[:, lo:hi]
            mixed = jnp.dot(pooled.astype(BF16), poolw_ref[kk].astype(BF16), preferred_element_type=F32)
            outs.append(mixed * pscale_ref[:, lo:hi])

        usb = us.astype(BF16)
        for n in range(N_STATE_TILES):
            kb = (n * MXU_TILE // SSM_STATE * SSM_GROUP) // MXU_TILE
            lhs = usb[:, kb * MXU_TILE:(kb + 1) * MXU_TILE]
            cols = slice(n * MXU_TILE, (n + 1) * MXU_TILE)
            sre_scr[r_lo:r_lo + Rs, cols] = jnp.dot(lhs, wbre_ref[n], preferred_element_type=F32)
            sim_scr[r_lo:r_lo + Rs, cols] = jnp.dot(lhs, wbim_ref[n], preferred_element_type=F32)

        for cb in range(N_STATE // CB):
            c0 = cb * CB
            lr = jnp.broadcast_to(lbre_ref[:, c0:c0 + CB], (SUBLANES, CB))
            li = jnp.broadcast_to(lbim_ref[:, c0:c0 + CB], (SUBLANES, CB))

            def scan_tile(s0, c0=c0, lr=lr, li=li, r_lo=r_lo):
                hr = hre_scr[pl.ds(s0, SUBLANES), c0:c0 + CB]
                hi_ = him_scr[pl.ds(s0, SUBLANES), c0:c0 + CB]
                for t in range(Ts):
                    r0 = r_lo + t * S + s0
                    br = sre_scr[pl.ds(r0, SUBLANES), c0:c0 + CB]
                    bi = sim_scr[pl.ds(r0, SUBLANES), c0:c0 + CB]
                    hr, hi_ = lr * hr - li * hi_ + br, lr * hi_ + li * hr + bi
                    sre_scr[pl.ds(r0, SUBLANES), c0:c0 + CB] = hr
                    sim_scr[pl.ds(r0, SUBLANES), c0:c0 + CB] = hi_
                hre_scr[pl.ds(s0, SUBLANES), c0:c0 + CB] = hr
                him_scr[pl.ds(s0, SUBLANES), c0:c0 + CB] = hi_

            if n_tiles == 1:
                scan_tile(0)
            else:
                def tile_body(j, carry, scan_tile=scan_tile):
                    scan_tile(pl.multiple_of(j * SUBLANES, SUBLANES))
                    return carry

                lax.fori_loop(0, n_tiles, tile_body, 0)

        ys = []
        for m in range(n_ct):
            acc = d_ref[:, m * MXU_TILE:(m + 1) * MXU_TILE] * us[:, m * MXU_TILE:(m + 1) * MXU_TILE]
            for kk in range(k_per):
                kt = m * k_per + kk
                cols = slice(kt * MXU_TILE, (kt + 1) * MXU_TILE)
                acc = acc + jnp.dot(sre_scr[r_lo:r_lo + Rs, cols].astype(BF16), wcre_ref[kt],
                                    preferred_element_type=F32)
                acc = acc + jnp.dot(sim_scr[r_lo:r_lo + Rs, cols].astype(BF16), wcim_ref[kt],
                                    preferred_element_type=F32)
            ys.append(acc)
        y = jnp.concatenate(ys, axis=-1)
        gl = jax.nn.gelu(y)
        gate = jax.nn.sigmoid(jnp.dot(gl.astype(BF16), glub_scr[...], preferred_element_type=F32) + glub_ref[...])
        outs.append(gl * gate)

        mix_tm = jnp.concatenate(outs, axis=-1)
        if seq_major:
            mix_ref[:, t0:t0 + Ts, :] = jnp.swapaxes(mix_tm.reshape(Ts, S, D), 0, 1).astype(mix_ref.dtype)
        else:
            mix_ref[r_lo:r_lo + Rs, :] = mix_tm.astype(mix_ref.dtype)

    for j in range(POOL_BUF):
        r0 = (Tt + 1 + j) * S
        newbuf_ref[:, j, :] = z_scr[r0:r0 + S, :]
    hist = z_scr[R:R + HR, :]
    z_scr[0:HR, :] = hist
    hre_out_ref[...] = hre_scr[...]
    him_out_ref[...] = him_scr[...]


def _mixer(x, mod, g1, w_in, state, wts, *, Tt, Ts, start_pos, seq_major):
    S, L, D = x.shape
    R = Tt * S
    has_state = state is not None
    consts = [g1, w_in] + (list(state) if has_state else []) + list(wts)
    x_spec = pl.BlockSpec((S, Tt, D), lambda i: (0, i, 0))
    mod_specs = [pl.BlockSpec((S, D), lambda i: (0, 0)), pl.BlockSpec((S, D), lambda i: (0, 1))]
    if seq_major:
        mix_spec = pl.BlockSpec((S, Tt, D), lambda i: (0, i, 0))
        mix_shape = jax.ShapeDtypeStruct((S, L, D), BF16)
    else:
        assert Tt == L
        mix_spec = pl.BlockSpec((R, D), lambda i: (0, 0))
        mix_shape = jax.ShapeDtypeStruct((L * S, D), BF16)
    body = functools.partial(_mixer_body, S=S, Tt=Tt, Ts=Ts, start_pos=start_pos, seq_major=seq_major,
                             has_state=has_state)
    return pl.pallas_call(
        body,
        grid=(L // Tt,),
        in_specs=[x_spec] + mod_specs + [_const_spec(a.shape) for a in consts],
        out_specs=(mix_spec,
                   pl.BlockSpec((S, POOL_BUF, POOL_WIDTH), lambda i: (0, 0, 0)),
                   pl.BlockSpec((S, N_STATE), lambda i: (0, 0)),
                   pl.BlockSpec((S, N_STATE), lambda i: (0, 0))),
        out_shape=(mix_shape,
                   jax.ShapeDtypeStruct((S, POOL_BUF, POOL_WIDTH), F32),
                   jax.ShapeDtypeStruct((S, N_STATE), F32),
                   jax.ShapeDtypeStruct((S, N_STATE), F32)),
        scratch_shapes=[pltpu.VMEM(((HIST + Tt) * S, POOL_WIDTH), F32),
                        pltpu.VMEM((R, N_STATE), F32),
                        pltpu.VMEM((R, N_STATE), F32),
                        pltpu.VMEM((S, N_STATE), F32),
                        pltpu.VMEM((S, N_STATE), F32),
                        pltpu.VMEM((D, D), BF16),
                        pltpu.VMEM((SSM_WIDTH, SSM_WIDTH), BF16)],
        compiler_params=_cparams(("arbitrary",)),
        name="mixer_S%d" % S,
    )(x, mod, mod, *consts)


def _split_bf16(v):
    hi = v.astype(BF16)
    lo = (v - hi.astype(F32)).astype(BF16)
    return hi, lo


def _route(h2, rw_ref, rb_ref):
    R = h2.shape[0]
    h2_hi, h2_lo = _split_bf16(h2)
    rw_hi, rw_lo = _split_bf16(rw_ref[...])
    logits = (jnp.dot(h2_hi, rw_hi, preferred_element_type=F32)
              + jnp.dot(h2_lo, rw_hi, preferred_element_type=F32)
              + jnp.dot(h2_hi, rw_lo, preferred_element_type=F32)) + rb_ref[...]
    lane = lax.broadcasted_iota(jnp.int32, (R, ROUTER_LANES), 1).astype(F32)
    ninf = jnp.float32(-jnp.inf)
    none = jnp.float32(ROUTER_LANES)
    is_g = lane < N_EXPERT_GROUPS
    l1 = jnp.where(is_g, logits, ninf)
    m1 = jnp.max(l1, axis=-1, keepdims=True)
    gidx = jnp.min(jnp.where(l1 == m1, lane, none), axis=-1, keepdims=True)
    p_top = 1.0 / jnp.sum(jnp.where(is_g, jnp.exp(logits - m1), 0.0), axis=-1, keepdims=True)
    e_lo = N_EXPERT_GROUPS + gidx * EXPERTS_PER_GROUP
    sel = (lane >= e_lo) & (lane < e_lo + EXPERTS_PER_GROUP)
    l2 = jnp.where(sel, logits, ninf)
    va = jnp.max(l2, axis=-1, keepdims=True)
    ia = jnp.min(jnp.where(l2 == va, lane, none), axis=-1, keepdims=True)
    l2b = jnp.where(lane == ia, ninf, l2)
    vb = jnp.max(l2b, axis=-1, keepdims=True)
    ib = jnp.min(jnp.where(l2b == vb, lane, none), axis=-1, keepdims=True)
    eb = jnp.exp(vb - va)
    den = 1.0 + eb
    gates = jnp.where(lane == ia, (1.0 / den) * p_top, 0.0) + jnp.where(lane == ib, (eb / den) * p_top, 0.0)
    return h2_hi, gates


def _stage3_rows(x, mix, g1, sh2, sc2, g2, n2g_ref, fng_ref, wout_ref, rw_ref, rb_ref, wg_ref, wu_ref, wd_ref):
    R = x.shape[0]
    mixo = jnp.dot(mix, wout_ref[...], preferred_element_type=F32)
    x1 = x + g1 * mixo
    h2 = _rms(x1, n2g_ref[...]) * (1.0 + sc2) + sh2
    h2_hi, gates = _route(h2, rw_ref, rb_ref)

    acc = jnp.zeros((R, D_MODEL), F32)
    for e in range(N_EXPERTS):
        a = jnp.dot(h2_hi, wg_ref[e], preferred_element_type=F32)
        b = jnp.dot(h2_hi, wu_ref[e], preferred_element_type=F32)
        ge = gates[:, N_EXPERT_GROUPS + e:N_EXPERT_GROUPS + e + 1]
        hid = jax.nn.silu(a) * b * ge
        acc = acc + jnp.dot(hid.astype(BF16), wd_ref[e], preferred_element_type=F32)
    x2 = x1 + g2 * acc
    return _rms(x2, fng_ref[...])


def _stage3_prompt_body(x_ref, mix_ref, g1_ref, sh2_ref, sc2_ref, g2_ref, *rest):
    wrefs, y_ref = rest[:-1], rest[-1]
    n = pl.program_id(0)
    mods = [r[pl.ds(n, 1), :] for r in (g1_ref, sh2_ref, sc2_ref, g2_ref)]
    y_ref[0] = _stage3_rows(x_ref[0], mix_ref[0], *mods, *wrefs)


def _stage3_prompt(x, mix, mod, wts, tc):
    nb, L, D = x.shape
    consts = list(wts)

    def mspec(k):
        return pl.BlockSpec((nb, D), lambda n, c, k=k: (0, k))

    return pl.pallas_call(
        _stage3_prompt_body,
        grid=(nb, L // tc),
        in_specs=[pl.BlockSpec((1, tc, D), lambda n, c: (n, c, 0)),
                  pl.BlockSpec((1, tc, D), lambda n, c: (n, c, 0)),
                  mspec(2), mspec(3), mspec(4), mspec(5)] + [_const_spec(a.shape) for a in consts],
        out_specs=pl.BlockSpec((1, tc, D), lambda n, c: (n, c, 0)),
        out_shape=jax.ShapeDtypeStruct((nb, L, D), F32),
        compiler_params=_cparams(("arbitrary", "arbitrary")),
        name="stage3_prompt",
    )(x, mix, mod, mod, mod, mod, *consts)


def _stage3_sample_body(x_ref, mix_ref, g1_ref, sh2_ref, sc2_ref, g2_ref, n2g_ref, fng_ref, wout_ref, rw_ref, rb_ref,
                        wg_ref, wu_ref, wd_ref, y_ref, x1_scr, h2b_scr, gates_scr, acc_scr, *, S, L):
    e = pl.program_id(0)
    R = S * L

    def rows(r):
        return jnp.concatenate([r[...]] * L, axis=0)

    @pl.when(e == 0)
    def _prologue():
        x_tm = jnp.concatenate([x_ref[:, t, :] for t in range(L)], axis=0)
        mixo = jnp.dot(mix_ref[...], wout_ref[...].astype(BF16), preferred_element_type=F32)
        x1 = x_tm + rows(g1_ref) * mixo
        h2 = _rms(x1, n2g_ref[...]) * (1.0 + rows(sc2_ref)) + rows(sh2_ref)
        h2_hi, gates = _route(h2, rw_ref, rb_ref)
        x1_scr[...] = x1
        h2b_scr[...] = h2_hi
        gates_scr[...] = gates
        acc_scr[...] = jnp.zeros((R, D_MODEL), F32)

    lane = lax.broadcasted_iota(jnp.int32, (R, ROUTER_LANES), 1)
    ge = jnp.sum(jnp.where(lane == N_EXPERT_GROUPS + e, gates_scr[...], 0.0), axis=-1, keepdims=True)
    h2b = h2b_scr[...]
    a = jnp.dot(h2b, wg_ref[0].astype(BF16), preferred_element_type=F32)
    b = jnp.dot(h2b, wu_ref[0].astype(BF16), preferred_element_type=F32)
    hid = jax.nn.silu(a) * b * ge
    acc_scr[...] += jnp.dot(hid.astype(BF16), wd_ref[0].astype(BF16), preferred_element_type=F32)

    @pl.when(e == pl.num_programs(0) - 1)
    def _epilogue():
        y = _rms(x1_scr[...] + rows(g2_ref) * acc_scr[...], fng_ref[...])
        for t in range(L):
            y_ref[:, t, :] = y[t * S:(t + 1) * S, :]


def _stage3_sample(x, mix_tm, mod, n2g, fng, w_out, rw, rb, w_gate, w_up, w_down):
    ns, L, D = x.shape
    E, _, F = w_gate.shape
    R = ns * L
    consts = [n2g, fng, w_out, rw, rb]

    def mspec(k):
        return pl.BlockSpec((ns, D), lambda e, k=k: (0, k), pipeline_mode=pl.Buffered(1))

    return pl.pallas_call(
        functools.partial(_stage3_sample_body, S=ns, L=L),
        grid=(E,),
        in_specs=[_const_spec((ns, L, D)), _const_spec((R, D)),
                  mspec(2), mspec(3), mspec(4), mspec(5)] + [_const_spec(a.shape) for a in consts]
                 + [pl.BlockSpec((1, D, F), lambda e: (e, 0, 0)),
                    pl.BlockSpec((1, D, F), lambda e: (e, 0, 0)),
                    pl.BlockSpec((1, F, D), lambda e: (e, 0, 0))],
        out_specs=pl.BlockSpec((ns, L, D), lambda e: (0, 0, 0)),
        out_shape=jax.ShapeDtypeStruct((ns, L, D), F32),
        scratch_shapes=[pltpu.VMEM((R, D), F32), pltpu.VMEM((R, D), BF16),
                        pltpu.VMEM((R, ROUTER_LANES), F32), pltpu.VMEM((R, D), F32)],
        compiler_params=_cparams(("arbitrary",)),
        name="stage3_sample",
    )(x, mix_tm, mod, mod, mod, mod, *consts, w_gate, w_up, w_down)


def kernel(x_prompt, x_sample, c_prompt, c_sample, state_pool, state_ssm_re, state_ssm_im, w_ada, b_ada, norm1_g, w_in, pool_w, pool_scale, ssm_a_re, ssm_a_im, ssm_log_dt, ssm_b_re, ssm_b_im, ssm_c_re, ssm_c_im, ssm_d, glu_w, glu_b, w_out, norm2_g, router_w1, router_b1, router_w2, router_b2, exp_w_gate, exp_w_up, exp_w_down, final_norm_g):
    depth = w_ada.shape[0]
    assert depth == 1
    l = 0
    nb, L, D = x_prompt.shape
    ns, Ls, _ = x_sample.shape

    lb_re, lb_im, d_flat, wbre, wbim, wcre, wcim = _ssm_prep(
        ssm_a_re[l], ssm_a_im[l], ssm_log_dt[l],
        jnp.transpose(ssm_b_re[l], (0, 2, 1)), jnp.transpose(ssm_b_im[l], (0, 2, 1)),
        jnp.transpose(ssm_c_re[l], (0, 2, 1)), jnp.transpose(ssm_c_im[l], (0, 2, 1)), ssm_d[l])
    mix_wts = (pool_w[l], pool_scale[l].reshape(1, -1), lb_re, lb_im, wbre, wbim, wcre, wcim, d_flat,
               glu_w[l], glu_b[l].reshape(1, -1))

    rw = jnp.concatenate([router_w1[l], jnp.transpose(router_w2[l], (1, 0, 2)).reshape(D, N_EXPERTS)], axis=1)
    rw = jnp.pad(rw, ((0, 0), (0, ROUTER_LANES - rw.shape[1])))
    rb = jnp.concatenate([router_b1[l], router_b2[l].reshape(-1)])
    rb = jnp.pad(rb, (0, ROUTER_LANES - rb.shape[0])).reshape(1, -1)
    s3_wts = (norm2_g[l].reshape(1, -1), final_norm_g.reshape(1, -1), w_out[l], rw, rb,
              exp_w_gate[l], exp_w_up[l], exp_w_down[l])

    mod_p, mod_s = _adaln(c_prompt, c_sample, w_ada[l], b_ada[l])
    g1 = norm1_g[l].reshape(1, -1)

    mix_p, pool_p, hre_p, him_p = _mixer(x_prompt, mod_p, g1, w_in[l], None, mix_wts,
                                         Tt=128, Ts=32, start_pos=0, seq_major=True)
    s3_wts_bf = s3_wts[:2] + (w_out[l].astype(BF16), rw, rb, exp_w_gate[l].astype(BF16),
                              exp_w_up[l].astype(BF16), exp_w_down[l].astype(BF16))
    y_p = _stage3_prompt(x_prompt, mix_p, mod_p, s3_wts_bf, 512)

    state = (state_pool[l], state_ssm_re[l].reshape(ns, N_STATE), state_ssm_im[l].reshape(ns, N_STATE))
    mix_s, pool_s, hre_s, him_s = _mixer(x_sample, mod_s, g1, w_in[l], state, mix_wts,
                                         Tt=Ls, Ts=Ls // 2, start_pos=PAST_LEN, seq_major=False)
    y_s = _stage3_sample(x_sample, mix_s, mod_s, *s3_wts)

    def st(a, n):
        return a.reshape(1, n, N_SSM_GROUPS, SSM_STATE)

    return (y_p, y_s, pool_p[None], pool_s[None], st(hre_p, nb), st(him_p, nb), st(hre_s, ns), st(him_s, ns))
```

```python
import functools

import jax
import jax.numpy as jnp
from jax import lax
from jax.experimental import pallas as pl
from jax.experimental.pallas import tpu as pltpu

D_MODEL = 1024
POOL_WIDTH = 512
SSM_WIDTH = 512
POOL_WINDOWS = (2, 4, 8, 16)
POOL_GROUP = 128
POOL_BUF = 15
HIST = 16
SSM_GROUP = 16
N_SSM_GROUPS = 32
SSM_STATE = 64
N_STATE = N_SSM_GROUPS * SSM_STATE
N_EXPERT_GROUPS = 4
EXPERTS_PER_GROUP = 4
N_EXPERTS = 16
EXPERT_HIDDEN = 256
N_MOD = 6
EPS = 1e-6
PAST_LEN = 16384

MXU_TILE = 256
LANES = 128
SUBLANES = 8
ROUTER_LANES = 128
VMEM_LIMIT = 60 * 1024 * 1024
N_STATE_TILES = N_STATE // MXU_TILE

F32 = jnp.float32
BF16 = jnp.bfloat16


def _cparams(sem):
    return pltpu.CompilerParams(dimension_semantics=sem, vmem_limit_bytes=VMEM_LIMIT)


def _const_spec(shape):
    nd = len(shape)
    return pl.BlockSpec(shape, lambda *_: (0,) * nd, pipeline_mode=pl.Buffered(1))


def _rms(x, g):
    return x * lax.rsqrt(jnp.mean(x * x, axis=-1, keepdims=True) + EPS) * g


def _ssm_prep_body(are_ref, aim_ref, ldt_ref, bre_ref, bim_ref, cre_ref, cim_ref, d_ref,
                   lbre_ref, lbim_ref, dflat_ref, wbre_ref, wbim_ref, wcre_ref, wcim_ref):
    G, H, P = N_SSM_GROUPS, SSM_GROUP, SSM_STATE
    a_re = are_ref[...]
    a_im = aim_ref[...]
    dt = jnp.exp(ldt_ref[...])
    mag = jnp.exp(a_re * dt)
    lb_re = mag * jnp.cos(a_im * dt)
    lb_im = mag * jnp.sin(a_im * dt)
    den = a_re * a_re + a_im * a_im
    nr = lb_re - 1.0
    ni = lb_im
    k_re = (nr * a_re + ni * a_im) / den
    k_im = (ni * a_re - nr * a_im) / den
    b_re = bre_ref[...]
    b_im = bim_ref[...]
    bb_re = k_re[:, None, :] * b_re - k_im[:, None, :] * b_im
    bb_im = k_re[:, None, :] * b_im + k_im[:, None, :] * b_re
    c_re = cre_ref[...]
    c_im = cim_ref[...]
    d = d_ref[...]

    zeros = jnp.zeros((N_STATE_TILES, MXU_TILE, MXU_TILE), BF16)
    wbre_ref[...] = zeros
    wbim_ref[...] = zeros
    wcre_ref[...] = zeros
    wcim_ref[...] = zeros
    g_per_tile = MXU_TILE // P
    g_per_blk = MXU_TILE // H
    for g in range(G):
        lbre_ref[:, g * P:(g + 1) * P] = lb_re[g:g + 1, :]
        lbim_ref[:, g * P:(g + 1) * P] = lb_im[g:g + 1, :]
        dflat_ref[:, g * H:(g + 1) * H] = d[g:g + 1, :]
        n, gi = divmod(g, g_per_tile)
        r0 = (g % g_per_blk) * H
        c0 = gi * P
        wbre_ref[n, r0:r0 + H, c0:c0 + P] = bb_re[g].astype(BF16)
        wbim_ref[n, r0:r0 + H, c0:c0 + P] = bb_im[g].astype(BF16)
        wcre_ref[n, c0:c0 + P, r0:r0 + H] = c_re[g].astype(BF16)
        wcim_ref[n, c0:c0 + P, r0:r0 + H] = (-c_im[g]).astype(BF16)


def _ssm_prep(a_re, a_im, log_dt, b_re_t, b_im_t, c_re_t, c_im_t, d):
    G = N_SSM_GROUPS
    tile = jax.ShapeDtypeStruct((N_STATE_TILES, MXU_TILE, MXU_TILE), BF16)
    return pl.pallas_call(
        _ssm_prep_body,
        out_shape=(jax.ShapeDtypeStruct((1, N_STATE), F32), jax.ShapeDtypeStruct((1, N_STATE), F32),
                   jax.ShapeDtypeStruct((1, SSM_WIDTH), F32), tile, tile, tile, tile),
        name="ssm_prep",
    )(a_re, a_im, log_dt.reshape(G, 1), b_re_t, b_im_t, c_re_t, c_im_t, d)


def _adaln_body(cp_ref, cs_ref, w_ref, b_ref, op_ref, os_ref):
    w = w_ref[...].astype(BF16)
    b = b_ref[...]
    op_ref[...] = jnp.dot(jax.nn.silu(cp_ref[...]).astype(BF16), w, preferred_element_type=F32) + b
    os_ref[...] = jnp.dot(jax.nn.silu(cs_ref[...]).astype(BF16), w, preferred_element_type=F32) + b


def _adaln(c_p, c_s, w_ada, b_ada):
    n_p, n_s = c_p.shape[0], c_s.shape[0]
    tn = D_MODEL
    return pl.pallas_call(
        _adaln_body,
        grid=(N_MOD * D_MODEL // tn,),
        in_specs=[pl.BlockSpec((n_p, D_MODEL), lambda j: (0, 0)),
                  pl.BlockSpec((n_s, D_MODEL), lambda j: (0, 0)),
                  pl.BlockSpec((D_MODEL, tn), lambda j: (0, j)),
                  pl.BlockSpec((1, tn), lambda j: (0, j))],
        out_specs=(pl.BlockSpec((n_p, tn), lambda j: (0, j)), pl.BlockSpec((n_s, tn), lambda j: (0, j))),
        out_shape=(jax.ShapeDtypeStruct((n_p, N_MOD * D_MODEL), F32),
                   jax.ShapeDtypeStruct((n_s, N_MOD * D_MODEL), F32)),
        compiler_params=_cparams(("arbitrary",)),
        name="adaln",
    )(c_p, c_s, w_ada, b_ada.reshape(1, -1))


def _mixer_body(*refs, S, Tt, Ts, start_pos, seq_major, has_state):
    refs = list(refs)
    x_ref, sh_ref, sc_ref, g_ref, win_ref = refs[:5]
    k = 5
    if has_state:
        buf0_ref, hre0_ref, him0_ref = refs[k:k + 3]
        k += 3
    (poolw_ref, pscale_ref, lbre_ref, lbim_ref, wbre_ref, wbim_ref, wcre_ref, wcim_ref, d_ref, gluw_ref,
     glub_ref) = refs[k:k + 11]
    k += 11
    mix_ref, newbuf_ref, hre_out_ref, him_out_ref = refs[k:k + 4]
    z_scr, sre_scr, sim_scr, hre_scr, him_scr, winb_scr, glub_scr = refs[k + 4:]

    i = pl.program_id(0)
    R = Tt * S
    HR = HIST * S
    D = D_MODEL

    @pl.when(i == 0)
    def _init():
        z_scr[0:S, :] = jnp.zeros((S, POOL_WIDTH), F32)
        if has_state:
            for j in range(POOL_BUF):
                z_scr[(j + 1) * S:(j + 2) * S, :] = buf0_ref[:, j, :]
            hre_scr[...] = hre0_ref[...]
            him_scr[...] = him0_ref[...]
        else:
            z_scr[S:HR, :] = jnp.zeros((HR - S, POOL_WIDTH), F32)
            hre_scr[...] = jnp.zeros((S, N_STATE), F32)
            him_scr[...] = jnp.zeros((S, N_STATE), F32)
        winb_scr[...] = win_ref[...].astype(BF16)
        glub_scr[...] = gluw_ref[...].astype(BF16)

    g = g_ref[...]
    Rs = Ts * S
    CB = 512
    n_tiles = S // SUBLANES
    n_ct = SSM_WIDTH // MXU_TILE
    k_per = N_STATE_TILES // n_ct
    for sub in range(Tt // Ts):
        t0 = sub * Ts
        r_lo = sub * Rs

        if seq_major:
            x3 = x_ref[:, t0:t0 + Ts, :]
            h3 = _rms(x3, g) * (1.0 + sc_ref[...][:, None, :]) + sh_ref[...][:, None, :]
            u_nm = jnp.dot(h3.reshape(Rs, D).astype(BF16), winb_scr[...], preferred_element_type=F32)
            u = jnp.swapaxes(u_nm.reshape(S, Ts, D), 0, 1).reshape(Rs, D)
        else:
            x_tm = jnp.concatenate([x_ref[:, t, :] for t in range(t0, t0 + Ts)], axis=0)
            h3 = _rms(x_tm, g).reshape(Ts, S, D) * (1.0 + sc_ref[...]) + sh_ref[...]
            u = jnp.dot(h3.reshape(Rs, D).astype(BF16), winb_scr[...], preferred_element_type=F32)

        up = u[:, 0:POOL_WIDTH]
        us = u[:, POOL_WIDTH:D]
        z_scr[HR + r_lo:HR + r_lo + Rs, :] = up

        row = lax.broadcasted_iota(jnp.int32, (Rs, 1), 0)
        pos = start_pos + i * Tt + t0 + lax.shift_right_logical(row, S.bit_length() - 1)
        outs = []
        for kk, w in enumerate(POOL_WINDOWS):
            lo, hi = kk * POOL_GROUP, (kk + 1) * POOL_GROUP
            cur = z_scr[r_lo:r_lo + HR + Rs, lo:hi]
            step = 1
            while step < w:
                cur = cur[step * S:, :] + cur[:cur.shape[0] - step * S, :]
                step *= 2
            s = cur[cur.shape[0] - Rs:, :]
            cnt = jnp.minimum(w, pos + 1).astype(F32)
            pooled = s / cnt - up[:, lo:hi]
            mixed = jnp.dot(pooled.astype(BF16), poolw_ref[kk].astype(BF16), preferred_element_type=F32)
            outs.append(mixed * pscale_ref[:, lo:hi])

        usb = us.astype(BF16)
        for n in range(N_STATE_TILES):
            kb = (n * MXU_TILE // SSM_STATE * SSM_GROUP) // MXU_TILE
            lhs = usb[:, kb * MXU_TILE:(kb + 1) * MXU_TILE]
            cols = slice(n * MXU_TILE, (n + 1) * MXU_TILE)
            sre_scr[r_lo:r_lo + Rs, cols] = jnp.dot(lhs, wbre_ref[n], preferred_element_type=F32)
            sim_scr[r_lo:r_lo + Rs, cols] = jnp.dot(lhs, wbim_ref[n], preferred_element_type=F32)

        for cb in range(N_STATE // CB):
            c0 = cb * CB
            lr = jnp.broadcast_to(lbre_ref[:, c0:c0 + CB], (SUBLANES, CB))
            li = jnp.broadcast_to(lbim_ref[:, c0:c0 + CB], (SUBLANES, CB))

            def scan_tile(s0, c0=c0, lr=lr, li=li, r_lo=r_lo):
                hr = hre_scr[pl.ds(s0, SUBLANES), c0:c0 + CB]
                hi_ = him_scr[pl.ds(s0, SUBLANES), c0:c0 + CB]
                for t in range(Ts):
                    r0 = r_lo + t * S + s0
                    br = sre_scr[pl.ds(r0, SUBLANES), c0:c0 + CB]
                    bi = sim_scr[pl.ds(r0, SUBLANES), c0:c0 + CB]
                    hr, hi_ = lr * hr - li * hi_ + br, lr * hi_ + li * hr + bi
                    sre_scr[pl.ds(r0, SUBLANES), c0:c0 + CB] = hr
                    sim_scr[pl.ds(r0, SUBLANES), c0:c0 + CB] = hi_
                hre_scr[pl.ds(s0, SUBLANES), c0:c0 + CB] = hr
                him_scr[pl.ds(s0, SUBLANES), c0:c0 + CB] = hi_

            if n_tiles == 1:
                scan_tile(0)
            else:
                def tile_body(j, carry, scan_tile=scan_tile):
                    scan_tile(pl.multiple_of(j * SUBLANES, SUBLANES))
                    return carry

                lax.fori_loop(0, n_tiles, tile_body, 0)

        ys = []
        for m in range(n_ct):
            acc = d_ref[:, m * MXU_TILE:(m + 1) * MXU_TILE] * us[:, m * MXU_TILE:(m + 1) * MXU_TILE]
            for kk in range(k_per):
                kt = m * k_per + kk
                cols = slice(kt * MXU_TILE, (kt + 1) * MXU_TILE)
                acc = acc + jnp.dot(sre_scr[r_lo:r_lo + Rs, cols].astype(BF16), wcre_ref[kt],
                                    preferred_element_type=F32)
                acc = acc + jnp.dot(sim_scr[r_lo:r_lo + Rs, cols].astype(BF16), wcim_ref[kt],
                                    preferred_element_type=F32)
            ys.append(acc)
        y = jnp.concatenate(ys, axis=-1)
        gl = jax.nn.gelu(y)
        gate = jax.nn.sigmoid(jnp.dot(gl.astype(BF16), glub_scr[...], preferred_element_type=F32) + glub_ref[...])
        outs.append(gl * gate)

        mix_tm = jnp.concatenate(outs, axis=-1)
        if seq_major:
            mix_ref[:, t0:t0 + Ts, :] = jnp.swapaxes(mix_tm.reshape(Ts, S, D), 0, 1).astype(mix_ref.dtype)
        else:
            mix_ref[r_lo:r_lo + Rs, :] = mix_tm.astype(mix_ref.dtype)

    for j in range(POOL_BUF):
        r0 = (Tt + 1 + j) * S
        newbuf_ref[:, j, :] = z_scr[r0:r0 + S, :]
    hist = z_scr[R:R + HR, :]
    z_scr[0:HR, :] = hist
    hre_out_ref[...] = hre_scr[...]
    him_out_ref[...] = him_scr[...]


def _mixer(x, mod, g1, w_in, state, wts, *, Tt, Ts, start_pos, seq_major):
    S, L, D = x.shape
    R = Tt * S
    has_state = state is not None
    consts = [g1, w_in] + (list(state) if has_state else []) + list(wts)
    x_spec = pl.BlockSpec((S, Tt, D), lambda i: (0, i, 0))
    mod_specs = [pl.BlockSpec((S, D), lambda i: (0, 0)), pl.BlockSpec((S, D), lambda i: (0, 1))]
    if seq_major:
        mix_spec = pl.BlockSpec((S, Tt, D), lambda i: (0, i, 0))
        mix_shape = jax.ShapeDtypeStruct((S, L, D), BF16)
    else:
        assert Tt == L
        mix_spec = pl.BlockSpec((R, D), lambda i: (0, 0))
        mix_shape = jax.ShapeDtypeStruct((L * S, D), BF16)
    body = functools.partial(_mixer_body, S=S, Tt=Tt, Ts=Ts, start_pos=start_pos, seq_major=seq_major,
                             has_state=has_state)
    return pl.pallas_call(
        body,
        grid=(L // Tt,),
        in_specs=[x_spec] + mod_specs + [_const_spec(a.shape) for a in consts],
        out_specs=(mix_spec,
                   pl.BlockSpec((S, POOL_BUF, POOL_WIDTH), lambda i: (0, 0, 0)),
                   pl.BlockSpec((S, N_STATE), lambda i: (0, 0)),
                   pl.BlockSpec((S, N_STATE), lambda i: (0, 0))),
        out_shape=(mix_shape,
                   jax.ShapeDtypeStruct((S, POOL_BUF, POOL_WIDTH), F32),
                   jax.ShapeDtypeStruct((S, N_STATE), F32),
                   jax.ShapeDtypeStruct((S, N_STATE), F32)),
        scratch_shapes=[pltpu.VMEM(((HIST + Tt) * S, POOL_WIDTH), F32),
                        pltpu.VMEM((R, N_STATE), F32),
                        pltpu.VMEM((R, N_STATE), F32),
                        pltpu.VMEM((S, N_STATE), F32),
                        pltpu.VMEM((S, N_STATE), F32),
                        pltpu.VMEM((D, D), BF16),
                        pltpu.VMEM((SSM_WIDTH, SSM_WIDTH), BF16)],
        compiler_params=_cparams(("arbitrary",)),
        name="mixer_S%d" % S,
    )(x, mod, mod, *consts)


def _split_bf16(v):
    hi = v.astype(BF16)
    lo = (v - hi.astype(F32)).astype(BF16)
    return hi, lo


def _route(h2, rw_ref, rb_ref):
    R = h2.shape[0]
    h2_hi, h2_lo = _split_bf16(h2)
    rw_hi, rw_lo = _split_bf16(rw_ref[...])
    logits = (jnp.dot(h2_hi, rw_hi, preferred_element_type=F32)
              + jnp.dot(h2_lo, rw_hi, preferred_element_type=F32)
              + jnp.dot(h2_hi, rw_lo, preferred_element_type=F32)) + rb_ref[...]
    lane = lax.broadcasted_iota(jnp.int32, (R, ROUTER_LANES), 1).astype(F32)
    ninf = jnp.float32(-jnp.inf)
    none = jnp.float32(ROUTER_LANES)
    is_g = lane < N_EXPERT_GROUPS
    l1 = jnp.where(is_g, logits, ninf)
    m1 = jnp.max(l1, axis=-1, keepdims=True)
    gidx = jnp.min(jnp.where(l1 == m1, lane, none), axis=-1, keepdims=True)
    p_top = 1.0 / jnp.sum(jnp.where(is_g, jnp.exp(logits - m1), 0.0), axis=-1, keepdims=True)
    e_lo = N_EXPERT_GROUPS + gidx * EXPERTS_PER_GROUP
    sel = (lane >= e_lo) & (lane < e_lo + EXPERTS_PER_GROUP)
    l2 = jnp.where(sel, logits, ninf)
    va = jnp.max(l2, axis=-1, keepdims=True)
    ia = jnp.min(jnp.where(l2 == va, lane, none), axis=-1, keepdims=True)
    l2b = jnp.where(lane == ia, ninf, l2)
    vb = jnp.max(l2b, axis=-1, keepdims=True)
    ib = jnp.min(jnp.where(l2b == vb, lane, none), axis=-1, keepdims=True)
    eb = jnp.exp(vb - va)
    den = 1.0 + eb
    gates = jnp.where(lane == ia, (1.0 / den) * p_top, 0.0) + jnp.where(lane == ib, (eb / den) * p_top, 0.0)
    return h2_hi, gates


def _stage3_rows(x, mix, g1, sh2, sc2, g2, n2g_ref, fng_ref, wout_ref, rw_ref, rb_ref, wg_ref, wu_ref, wd_ref):
    R = x.shape[0]
    mixo = jnp.dot(mix, wout_ref[...], preferred_element_type=F32)
    x1 = x + g1 * mixo
    h2 = _rms(x1, n2g_ref[...]) * (1.0 + sc2) + sh2
    h2_hi, gates = _route(h2, rw_ref, rb_ref)

    acc = jnp.zeros((R, D_MODEL), F32)
    for e in range(N_EXPERTS):
        a = jnp.dot(h2_hi, wg_ref[e], preferred_element_type=F32)
        b = jnp.dot(h2_hi, wu_ref[e], preferred_element_type=F32)
        ge = gates[:, N_EXPERT_GROUPS + e:N_EXPERT_GROUPS + e + 1]
        hid = jax.nn.silu(a) * b * ge
        acc = acc + jnp.dot(hid.astype(BF16), wd_ref[e], preferred_element_type=F32)
    x2 = x1 + g2 * acc
    return _rms(x2, fng_ref[...])


def _stage3_prompt_body(x_ref, mix_ref, g1_ref, sh2_ref, sc2_ref, g2_ref, *rest):
    wrefs, y_ref = rest[:-1], rest[-1]
    n = pl.program_id(0)
    mods = [r[pl.ds(n, 1), :] for r in (g1_ref, sh2_ref, sc2_ref, g2_ref)]
    y_ref[0] = _stage3_rows(x_ref[0], mix_ref[0], *mods, *wrefs)


def _stage3_prompt(x, mix, mod, wts, tc):
    nb, L, D = x.shape
    consts = list(wts)

    def mspec(k):
        return pl.BlockSpec((nb, D), lambda n, c, k=k: (0, k))

    return pl.pallas_call(
        _stage3_prompt_body,
        grid=(nb, L // tc),
        in_specs=[pl.BlockSpec((1, tc, D), lambda n, c: (n, c, 0)),
                  pl.BlockSpec((1, tc, D), lambda n, c: (n, c, 0)),
                  mspec(2), mspec(3), mspec(4), mspec(5)] + [_const_spec(a.shape) for a in consts],
        out_specs=pl.BlockSpec((1, tc, D), lambda n, c: (n, c, 0)),
        out_shape=jax.ShapeDtypeStruct((nb, L, D), F32),
        compiler_params=_cparams(("arbitrary", "arbitrary")),
        name="stage3_prompt",
    )(x, mix, mod, mod, mod, mod, *consts)


SAMPLE_EXPERTS_PER_STEP = 2


def _stage3_sample_body(x_ref, mix_ref, g1_ref, sh2_ref, sc2_ref, g2_ref, n2g_ref, fng_ref, wout_ref, rw_ref, rb_ref,
                        wg_ref, wu_ref, wd_ref, y_ref, x1_scr, h2b_scr, gates_scr, acc_scr, *, S, L):
    e = pl.program_id(0)
    R = S * L

    def rows(r):
        return jnp.concatenate([r[...]] * L, axis=0)

    @pl.when(e == 0)
    def _prologue():
        x_tm = jnp.concatenate([x_ref[:, t, :] for t in range(L)], axis=0)
        mixo = jnp.dot(mix_ref[...], wout_ref[...].astype(BF16), preferred_element_type=F32)
        x1 = x_tm + rows(g1_ref) * mixo
        h2 = _rms(x1, n2g_ref[...]) * (1.0 + rows(sc2_ref)) + rows(sh2_ref)
        h2_hi, gates = _route(h2, rw_ref, rb_ref)
        x1_scr[...] = x1
        h2b_scr[...] = h2_hi
        gates_scr[...] = gates
        acc_scr[...] = jnp.zeros((R, D_MODEL), F32)

    lane = lax.broadcasted_iota(jnp.int32, (R, ROUTER_LANES), 1)
    gates = gates_scr[...]
    h2b = h2b_scr[...]
    acc = acc_scr[...]
    for j in range(SAMPLE_EXPERTS_PER_STEP):
        ge = jnp.sum(jnp.where(lane == N_EXPERT_GROUPS + e * SAMPLE_EXPERTS_PER_STEP + j, gates, 0.0),
                     axis=-1, keepdims=True)
        a = jnp.dot(h2b, wg_ref[j].astype(BF16), preferred_element_type=F32)
        b = jnp.dot(h2b, wu_ref[j].astype(BF16), preferred_element_type=F32)
        hid = jax.nn.silu(a) * b * ge
        acc = acc + jnp.dot(hid.astype(BF16), wd_ref[j].astype(BF16), preferred_element_type=F32)
    acc_scr[...] = acc

    @pl.when(e == pl.num_programs(0) - 1)
    def _epilogue():
        y = _rms(x1_scr[...] + rows(g2_ref) * acc_scr[...], fng_ref[...])
        for t in range(L):
            y_ref[:, t, :] = y[t * S:(t + 1) * S, :]


def _stage3_sample(x, mix_tm, mod, n2g, fng, w_out, rw, rb, w_gate, w_up, w_down):
    ns, L, D = x.shape
    E, _, F = w_gate.shape
    R = ns * L
    consts = [n2g, fng, w_out, rw, rb]

    def mspec(k):
        return pl.BlockSpec((ns, D), lambda e, k=k: (0, k), pipeline_mode=pl.Buffered(1))

    return pl.pallas_call(
        functools.partial(_stage3_sample_body, S=ns, L=L),
        grid=(E // SAMPLE_EXPERTS_PER_STEP,),
        in_specs=[_const_spec((ns, L, D)), _const_spec((R, D)),
                  mspec(2), mspec(3), mspec(4), mspec(5)] + [_const_spec(a.shape) for a in consts]
                 + [pl.BlockSpec((SAMPLE_EXPERTS_PER_STEP, D, F), lambda e: (e, 0, 0)),
                    pl.BlockSpec((SAMPLE_EXPERTS_PER_STEP, D, F), lambda e: (e, 0, 0)),
                    pl.BlockSpec((SAMPLE_EXPERTS_PER_STEP, F, D), lambda e: (e, 0, 0))],
        out_specs=pl.BlockSpec((ns, L, D), lambda e: (0, 0, 0)),
        out_shape=jax.ShapeDtypeStruct((ns, L, D), F32),
        scratch_shapes=[pltpu.VMEM((R, D), F32), pltpu.VMEM((R, D), BF16),
                        pltpu.VMEM((R, ROUTER_LANES), F32), pltpu.VMEM((R, D), F32)],
        compiler_params=_cparams(("arbitrary",)),
        name="stage3_sample",
    )(x, mix_tm, mod, mod, mod, mod, *consts, w_gate, w_up, w_down)


def kernel(x_prompt, x_sample, c_prompt, c_sample, state_pool, state_ssm_re, state_ssm_im, w_ada, b_ada, norm1_g, w_in, pool_w, pool_scale, ssm_a_re, ssm_a_im, ssm_log_dt, ssm_b_re, ssm_b_im, ssm_c_re, ssm_c_im, ssm_d, glu_w, glu_b, w_out, norm2_g, router_w1, router_b1, router_w2, router_b2, exp_w_gate, exp_w_up, exp_w_down, final_norm_g):
    depth = w_ada.shape[0]
    assert depth == 1
    l = 0
    nb, L, D = x_prompt.shape
    ns, Ls, _ = x_sample.shape

    lb_re, lb_im, d_flat, wbre, wbim, wcre, wcim = _ssm_prep(
        ssm_a_re[l], ssm_a_im[l], ssm_log_dt[l],
        jnp.transpose(ssm_b_re[l], (0, 2, 1)), jnp.transpose(ssm_b_im[l], (0, 2, 1)),
        jnp.transpose(ssm_c_re[l], (0, 2, 1)), jnp.transpose(ssm_c_im[l], (0, 2, 1)), ssm_d[l])
    mix_wts = (pool_w[l], pool_scale[l].reshape(1, -1), lb_re, lb_im, wbre, wbim, wcre, wcim, d_flat,
               glu_w[l], glu_b[l].reshape(1, -1))

    rw = jnp.concatenate([router_w1[l], jnp.transpose(router_w2[l], (1, 0, 2)).reshape(D, N_EXPERTS)], axis=1)
    rw = jnp.pad(rw, ((0, 0), (0, ROUTER_LANES - rw.shape[1])))
    rb = jnp.concatenate([router_b1[l], router_b2[l].reshape(-1)])
    rb = jnp.pad(rb, (0, ROUTER_LANES - rb.shape[0])).reshape(1, -1)
    s3_wts = (norm2_g[l].reshape(1, -1), final_norm_g.reshape(1, -1), w_out[l], rw, rb,
              exp_w_gate[l], exp_w_up[l], exp_w_down[l])

    mod_p, mod_s = _adaln(c_prompt, c_sample, w_ada[l], b_ada[l])
    g1 = norm1_g[l].reshape(1, -1)

    mix_p, pool_p, hre_p, him_p = _mixer(x_prompt, mod_p, g1, w_in[l], None, mix_wts,
                                         Tt=128, Ts=32, start_pos=0, seq_major=True)
    s3_wts_bf = s3_wts[:2] + (w_out[l].astype(BF16), rw, rb, exp_w_gate[l].astype(BF16),
                              exp_w_up[l].astype(BF16), exp_w_down[l].astype(BF16))
    y_p = _stage3_prompt(x_prompt, mix_p, mod_p, s3_wts_bf, 512)

    state = (state_pool[l], state_ssm_re[l].reshape(ns, N_STATE), state_ssm_im[l].reshape(ns, N_STATE))
    mix_s, pool_s, hre_s, him_s = _mixer(x_sample, mod_s, g1, w_in[l], state, mix_wts,
                                         Tt=Ls, Ts=Ls // 2, start_pos=PAST_LEN, seq_major=False)
    y_s = _stage3_sample(x_sample, mix_s, mod_s, *s3_wts)

    def st(a, n):
        return a.reshape(1, n, N_SSM_GROUPS, SSM_STATE)

    return (y_p, y_s, pool_p[None], pool_s[None], st(hre_p, nb), st(him_p, nb), st(hre_s, ns), st(him_s, ns))
```

```python
import functools

import jax
import jax.numpy as jnp
from jax import lax
from jax.experimental import pallas as pl
from jax.experimental.pallas import tpu as pltpu

D_MODEL = 1024
POOL_WIDTH = 512
SSM_WIDTH = 512
POOL_WINDOWS = (2, 4, 8, 16)
POOL_GROUP = 128
POOL_BUF = 15
HIST = 16
SSM_GROUP = 16
N_SSM_GROUPS = 32
SSM_STATE = 64
N_STATE = N_SSM_GROUPS * SSM_STATE
N_EXPERT_GROUPS = 4
EXPERTS_PER_GROUP = 4
N_EXPERTS = 16
EXPERT_HIDDEN = 256
N_MOD = 6
EPS = 1e-6
PAST_LEN = 16384

MXU_TILE = 256
LANES = 128
SUBLANES = 8
ROUTER_LANES = 128
VMEM_LIMIT = 60 * 1024 * 1024
N_STATE_TILES = N_STATE // MXU_TILE

F32 = jnp.float32
BF16 = jnp.bfloat16


def _cparams(sem):
    return pltpu.CompilerParams(dimension_semantics=sem, vmem_limit_bytes=VMEM_LIMIT)


def _const_spec(shape):
    nd = len(shape)
    return pl.BlockSpec(shape, lambda *_: (0,) * nd, pipeline_mode=pl.Buffered(1))


def _rms(x, g):
    return x * lax.rsqrt(jnp.mean(x * x, axis=-1, keepdims=True) + EPS) * g


def _ssm_prep_body(are_ref, aim_ref, ldt_ref, bre_ref, bim_ref, cre_ref, cim_ref, d_ref,
                   lbre_ref, lbim_ref, dflat_ref, wbre_ref, wbim_ref, wcre_ref, wcim_ref):
    G, H, P = N_SSM_GROUPS, SSM_GROUP, SSM_STATE
    a_re = are_ref[...]
    a_im = aim_ref[...]
    dt = jnp.exp(ldt_ref[...])
    mag = jnp.exp(a_re * dt)
    lb_re = mag * jnp.cos(a_im * dt)
    lb_im = mag * jnp.sin(a_im * dt)
    den = a_re * a_re + a_im * a_im
    nr = lb_re - 1.0
    ni = lb_im
    k_re = (nr * a_re + ni * a_im) / den
    k_im = (ni * a_re - nr * a_im) / den
    b_re = bre_ref[...]
    b_im = bim_ref[...]
    bb_re = k_re[:, None, :] * b_re - k_im[:, None, :] * b_im
    bb_im = k_re[:, None, :] * b_im + k_im[:, None, :] * b_re
    c_re = cre_ref[...]
    c_im = cim_ref[...]
    d = d_ref[...]

    zeros = jnp.zeros((N_STATE_TILES, MXU_TILE, MXU_TILE), BF16)
    wbre_ref[...] = zeros
    wbim_ref[...] = zeros
    wcre_ref[...] = zeros
    wcim_ref[...] = zeros
    g_per_tile = MXU_TILE // P
    g_per_blk = MXU_TILE // H
    for g in range(G):
        lbre_ref[:, g * P:(g + 1) * P] = lb_re[g:g + 1, :]
        lbim_ref[:, g * P:(g + 1) * P] = lb_im[g:g + 1, :]
        dflat_ref[:, g * H:(g + 1) * H] = d[g:g + 1, :]
        n, gi = divmod(g, g_per_tile)
        r0 = (g % g_per_blk) * H
        c0 = gi * P
        wbre_ref[n, r0:r0 + H, c0:c0 + P] = bb_re[g].astype(BF16)
        wbim_ref[n, r0:r0 + H, c0:c0 + P] = bb_im[g].astype(BF16)
        wcre_ref[n, c0:c0 + P, r0:r0 + H] = c_re[g].astype(BF16)
        wcim_ref[n, c0:c0 + P, r0:r0 + H] = (-c_im[g]).astype(BF16)


def _ssm_prep(a_re, a_im, log_dt, b_re_t, b_im_t, c_re_t, c_im_t, d):
    G = N_SSM_GROUPS
    tile = jax.ShapeDtypeStruct((N_STATE_TILES, MXU_TILE, MXU_TILE), BF16)
    return pl.pallas_call(
        _ssm_prep_body,
        out_shape=(jax.ShapeDtypeStruct((1, N_STATE), F32), jax.ShapeDtypeStruct((1, N_STATE), F32),
                   jax.ShapeDtypeStruct((1, SSM_WIDTH), F32), tile, tile, tile, tile),
        name="ssm_prep",
    )(a_re, a_im, log_dt.reshape(G, 1), b_re_t, b_im_t, c_re_t, c_im_t, d)


def _adaln_body(cp_ref, cs_ref, w_ref, b_ref, op_ref, os_ref):
    w = w_ref[...].astype(BF16)
    b = b_ref[...]
    op_ref[...] = jnp.dot(jax.nn.silu(cp_ref[...]).astype(BF16), w, preferred_element_type=F32) + b
    os_ref[...] = jnp.dot(jax.nn.silu(cs_ref[...]).astype(BF16), w, preferred_element_type=F32) + b


def _adaln(c_p, c_s, w_ada, b_ada):
    n_p, n_s = c_p.shape[0], c_s.shape[0]
    tn = D_MODEL
    return pl.pallas_call(
        _adaln_body,
        grid=(N_MOD * D_MODEL // tn,),
        in_specs=[pl.BlockSpec((n_p, D_MODEL), lambda j: (0, 0)),
                  pl.BlockSpec((n_s, D_MODEL), lambda j: (0, 0)),
                  pl.BlockSpec((D_MODEL, tn), lambda j: (0, j)),
                  pl.BlockSpec((1, tn), lambda j: (0, j))],
        out_specs=(pl.BlockSpec((n_p, tn), lambda j: (0, j)), pl.BlockSpec((n_s, tn), lambda j: (0, j))),
        out_shape=(jax.ShapeDtypeStruct((n_p, N_MOD * D_MODEL), F32),
                   jax.ShapeDtypeStruct((n_s, N_MOD * D_MODEL), F32)),
        compiler_params=_cparams(("arbitrary",)),
        name="adaln",
    )(c_p, c_s, w_ada, b_ada.reshape(1, -1))


def _mixer_body(*refs, S, Tt, Ts, start_pos, seq_major, has_state):
    refs = list(refs)
    x_ref, sh_ref, sc_ref, g_ref, win_ref = refs[:5]
    k = 5
    if has_state:
        buf0_ref, hre0_ref, him0_ref = refs[k:k + 3]
        k += 3
    (poolw_ref, pscale_ref, lbre_ref, lbim_ref, wbre_ref, wbim_ref, wcre_ref, wcim_ref, d_ref, gluw_ref,
     glub_ref) = refs[k:k + 11]
    k += 11
    mix_ref, newbuf_ref, hre_out_ref, him_out_ref = refs[k:k + 4]
    z_scr, sre_scr, sim_scr, hre_scr, him_scr, winb_scr, glub_scr = refs[k + 4:]

    i = pl.program_id(0)
    R = Tt * S
    HR = HIST * S
    D = D_MODEL

    @pl.when(i == 0)
    def _init():
        z_scr[0:S, :] = jnp.zeros((S, POOL_WIDTH), F32)
        if has_state:
            for j in range(POOL_BUF):
                z_scr[(j + 1) * S:(j + 2) * S, :] = buf0_ref[:, j, :]
            hre_scr[...] = hre0_ref[...]
            him_scr[...] = him0_ref[...]
        else:
            z_scr[S:HR, :] = jnp.zeros((HR - S, POOL_WIDTH), F32)
            hre_scr[...] = jnp.zeros((S, N_STATE), F32)
            him_scr[...] = jnp.zeros((S, N_STATE), F32)
        winb_scr[...] = win_ref[...].astype(BF16)
        glub_scr[...] = gluw_ref[...].astype(BF16)

    g = g_ref[...]
    Rs = Ts * S
    CB = 512
    n_tiles = S // SUBLANES
    n_ct = SSM_WIDTH // MXU_TILE
    k_per = N_STATE_TILES // n_ct
    for sub in range(Tt // Ts):
        t0 = sub * Ts
        r_lo = sub * Rs

        if seq_major:
            x3 = x_ref[:, t0:t0 + Ts, :]
            h3 = _rms(x3, g) * (1.0 + sc_ref[...][:, None, :]) + sh_ref[...][:, None, :]
            u_nm = jnp.dot(h3.reshape(Rs, D).astype(BF16), winb_scr[...], preferred_element_type=F32)
            u = jnp.swapaxes(u_nm.reshape(S, Ts, D), 0, 1).reshape(Rs, D)
        else:
            x_tm = jnp.concatenate([x_ref[:, t, :] for t in range(t0, t0 + Ts)], axis=0)
            h3 = _rms(x_tm, g).reshape(Ts, S, D) * (1.0 + sc_ref[...]) + sh_ref[...]
            u = jnp.dot(h3.reshape(Rs, D).astype(BF16), winb_scr[...], preferred_element_type=F32)

        up = u[:, 0:POOL_WIDTH]
        us = u[:, POOL_WIDTH:D]
        z_scr[HR + r_lo:HR + r_lo + Rs, :] = up

        row = lax.broadcasted_iota(jnp.int32, (Rs, 1), 0)
        pos = start_pos + i * Tt + t0 + lax.shift_right_logical(row, S.bit_length() - 1)
        outs = []
        for kk, w in enumerate(POOL_WINDOWS):
            lo, hi = kk * POOL_GROUP, (kk + 1) * POOL_GROUP
            cur = z_scr[r_lo:r_lo + HR + Rs, lo:hi]
            step = 1
            while step < w:
                cur = cur[step * S:, :] + cur[:cur.shape[0] - step * S, :]
                step *= 2
            s = cur[cur.shape[0] - Rs:, :]
            cnt = jnp.minimum(w, pos + 1).astype(F32)
            pooled = s / cnt - up[:, lo:hi]
            mixed = jnp.dot(pooled.astype(BF16), poolw_ref[kk].astype(BF16), preferred_element_type=F32)
            outs.append(mixed * pscale_ref[:, lo:hi])

        usb = us.astype(BF16)
        for n in range(N_STATE_TILES):
            kb = (n * MXU_TILE // SSM_STATE * SSM_GROUP) // MXU_TILE
            lhs = usb[:, kb * MXU_TILE:(kb + 1) * MXU_TILE]
            cols = slice(n * MXU_TILE, (n + 1) * MXU_TILE)
            sre_scr[r_lo:r_lo + Rs, cols] = jnp.dot(lhs, wbre_ref[n], preferred_element_type=F32)
            sim_scr[r_lo:r_lo + Rs, cols] = jnp.dot(lhs, wbim_ref[n], preferred_element_type=F32)

        for cb in range(N_STATE // CB):
            c0 = cb * CB
            lr = jnp.broadcast_to(lbre_ref[:, c0:c0 + CB], (SUBLANES, CB))
            li = jnp.broadcast_to(lbim_ref[:, c0:c0 + CB], (SUBLANES, CB))

            def scan_tile(s0, c0=c0, lr=lr, li=li, r_lo=r_lo):
                hr = hre_scr[pl.ds(s0, SUBLANES), c0:c0 + CB]
                hi_ = him_scr[pl.ds(s0, SUBLANES), c0:c0 + CB]
                for t in range(Ts):
                    r0 = r_lo + t * S + s0
                    br = sre_scr[pl.ds(r0, SUBLANES), c0:c0 + CB]
                    bi = sim_scr[pl.ds(r0, SUBLANES), c0:c0 + CB]
                    hr, hi_ = lr * hr - li * hi_ + br, lr * hi_ + li * hr + bi
                    sre_scr[pl.ds(r0, SUBLANES), c0:c0 + CB] = hr
                    sim_scr[pl.ds(r0, SUBLANES), c0:c0 + CB] = hi_
                hre_scr[pl.ds(s0, SUBLANES), c0:c0 + CB] = hr
                him_scr[pl.ds(s0, SUBLANES), c0:c0 + CB] = hi_

            if n_tiles == 1:
                scan_tile(0)
            else:
                def tile_body(j, carry, scan_tile=scan_tile):
                    scan_tile(pl.multiple_of(j * SUBLANES, SUBLANES))
                    return carry

                lax.fori_loop(0, n_tiles, tile_body, 0)

        ys = []
        for m in range(n_ct):
            acc = d_ref[:, m * MXU_TILE:(m + 1) * MXU_TILE] * us[:, m * MXU_TILE:(m + 1) * MXU_TILE]
            for kk in range(k_per):
                kt = m * k_per + kk
                cols = slice(kt * MXU_TILE, (kt + 1) * MXU_TILE)
                acc = acc + jnp.dot(sre_scr[r_lo:r_lo + Rs, cols].astype(BF16), wcre_ref[kt],
                                    preferred_element_type=F32)
                acc = acc + jnp.dot(sim_scr[r_lo:r_lo + Rs, cols].astype(BF16), wcim_ref[kt],
                                    preferred_element_type=F32)
            ys.append(acc)
        y = jnp.concatenate(ys, axis=-1)
        gl = jax.nn.gelu(y)
        gate = jax.nn.sigmoid(jnp.dot(gl.astype(BF16), glub_scr[...], preferred_element_type=F32) + glub_ref[...])
        outs.append(gl * gate)

        mix_tm = jnp.concatenate(outs, axis=-1)
        if seq_major:
            mix_ref[:, t0:t0 + Ts, :] = jnp.swapaxes(mix_tm.reshape(Ts, S, D), 0, 1).astype(mix_ref.dtype)
        else:
            mix_ref[r_lo:r_lo + Rs, :] = mix_tm.astype(mix_ref.dtype)

    for j in range(POOL_BUF):
        r0 = (Tt + 1 + j) * S
        newbuf_ref[:, j, :] = z_scr[r0:r0 + S, :]
    hist = z_scr[R:R + HR, :]
    z_scr[0:HR, :] = hist
    hre_out_ref[...] = hre_scr[...]
    him_out_ref[...] = him_scr[...]


def _mixer(x, mod, g1, w_in, state, wts, *, Tt, Ts, start_pos, seq_major):
    S, L, D = x.shape
    R = Tt * S
    has_state = state is not None
    consts = [g1, w_in] + (list(state) if has_state else []) + list(wts)
    x_spec = pl.BlockSpec((S, Tt, D), lambda i: (0, i, 0))
    mod_specs = [pl.BlockSpec((S, D), lambda i: (0, 0)), pl.BlockSpec((S, D), lambda i: (0, 1))]
    if seq_major:
        mix_spec = pl.BlockSpec((S, Tt, D), lambda i: (0, i, 0))
        mix_shape = jax.ShapeDtypeStruct((S, L, D), BF16)
    else:
        assert Tt == L
        mix_spec = pl.BlockSpec((R, D), lambda i: (0, 0))
        mix_shape = jax.ShapeDtypeStruct((L * S, D), BF16)
    body = functools.partial(_mixer_body, S=S, Tt=Tt, Ts=Ts, start_pos=start_pos, seq_major=seq_major,
                             has_state=has_state)
    return pl.pallas_call(
        body,
        grid=(L // Tt,),
        in_specs=[x_spec] + mod_specs + [_const_spec(a.shape) for a in consts],
        out_specs=(mix_spec,
                   pl.BlockSpec((S, POOL_BUF, POOL_WIDTH), lambda i: (0, 0, 0)),
                   pl.BlockSpec((S, N_STATE), lambda i: (0, 0)),
                   pl.BlockSpec((S, N_STATE), lambda i: (0, 0))),
        out_shape=(mix_shape,
                   jax.ShapeDtypeStruct((S, POOL_BUF, POOL_WIDTH), F32),
                   jax.ShapeDtypeStruct((S, N_STATE), F32),
                   jax.ShapeDtypeStruct((S, N_STATE), F32)),
        scratch_shapes=[pltpu.VMEM(((HIST + Tt) * S, POOL_WIDTH), F32),
                        pltpu.VMEM((R, N_STATE), F32),
                        pltpu.VMEM((R, N_STATE), F32),
                        pltpu.VMEM((S, N_STATE), F32),
                        pltpu.VMEM((S, N_STATE), F32),
                        pltpu.VMEM((D, D), BF16),
                        pltpu.VMEM((SSM_WIDTH, SSM_WIDTH), BF16)],
        compiler_params=_cparams(("arbitrary",)),
        name="mixer_S%d" % S,
    )(x, mod, mod, *consts)


def _split_bf16(v):
    hi = v.astype(BF16)
    lo = (v - hi.astype(F32)).astype(BF16)
    return hi, lo


def _route(h2, rw_ref, rb_ref):
    R = h2.shape[0]
    h2_hi, h2_lo = _split_bf16(h2)
    rw_hi, rw_lo = _split_bf16(rw_ref[...])
    logits = (jnp.dot(h2_hi, rw_hi, preferred_element_type=F32)
              + jnp.dot(h2_lo, rw_hi, preferred_element_type=F32)
              + jnp.dot(h2_hi, rw_lo, preferred_element_type=F32)) + rb_ref[...]
    lane = lax.broadcasted_iota(jnp.int32, (R, ROUTER_LANES), 1).astype(F32)
    ninf = jnp.float32(-jnp.inf)
    none = jnp.float32(ROUTER_LANES)
    is_g = lane < N_EXPERT_GROUPS
    l1 = jnp.where(is_g, logits, ninf)
    m1 = jnp.max(l1, axis=-1, keepdims=True)
    gidx = jnp.min(jnp.where(l1 == m1, lane, none), axis=-1, keepdims=True)
    p_top = 1.0 / jnp.sum(jnp.where(is_g, jnp.exp(logits - m1), 0.0), axis=-1, keepdims=True)
    e_lo = N_EXPERT_GROUPS + gidx * EXPERTS_PER_GROUP
    sel = (lane >= e_lo) & (lane < e_lo + EXPERTS_PER_GROUP)
    l2 = jnp.where(sel, logits, ninf)
    va = jnp.max(l2, axis=-1, keepdims=True)
    ia = jnp.min(jnp.where(l2 == va, lane, none), axis=-1, keepdims=True)
    l2b = jnp.where(lane == ia, ninf, l2)
    vb = jnp.max(l2b, axis=-1, keepdims=True)
    ib = jnp.min(jnp.where(l2b == vb, lane, none), axis=-1, keepdims=True)
    eb = jnp.exp(vb - va)
    den = 1.0 + eb
    gates = jnp.where(lane == ia, (1.0 / den) * p_top, 0.0) + jnp.where(lane == ib, (eb / den) * p_top, 0.0)
    return h2_hi, gates


def _stage3_rows(x, mix, g1, sh2, sc2, g2, n2g_ref, fng_ref, wout_ref, rw_ref, rb_ref, wg_ref, wu_ref, wd_ref):
    R = x.shape[0]
    mixo = jnp.dot(mix, wout_ref[...], preferred_element_type=F32)
    x1 = x + g1 * mixo
    h2 = _rms(x1, n2g_ref[...]) * (1.0 + sc2) + sh2
    h2_hi, gates = _route(h2, rw_ref, rb_ref)

    acc = jnp.zeros((R, D_MODEL), F32)
    for e in range(N_EXPERTS):
        a = jnp.dot(h2_hi, wg_ref[e], preferred_element_type=F32)
        b = jnp.dot(h2_hi, wu_ref[e], preferred_element_type=F32)
        ge = gates[:, N_EXPERT_GROUPS + e:N_EXPERT_GROUPS + e + 1]
        hid = jax.nn.silu(a) * b * ge
        acc = acc + jnp.dot(hid.astype(BF16), wd_ref[e], preferred_element_type=F32)
    x2 = x1 + g2 * acc
    return _rms(x2, fng_ref[...])


def _stage3_prompt_body(x_ref, mix_ref, g1_ref, sh2_ref, sc2_ref, g2_ref, *rest):
    wrefs, y_ref = rest[:-1], rest[-1]
    n = pl.program_id(0)
    mods = [r[pl.ds(n, 1), :] for r in (g1_ref, sh2_ref, sc2_ref, g2_ref)]
    y_ref[0] = _stage3_rows(x_ref[0], mix_ref[0], *mods, *wrefs)


def _stage3_prompt(x, mix, mod, wts, tc):
    nb, L, D = x.shape
    consts = list(wts)

    def mspec(k):
        return pl.BlockSpec((nb, D), lambda n, c, k=k: (0, k))

    return pl.pallas_call(
        _stage3_prompt_body,
        grid=(nb, L // tc),
        in_specs=[pl.BlockSpec((1, tc, D), lambda n, c: (n, c, 0)),
                  pl.BlockSpec((1, tc, D), lambda n, c: (n, c, 0)),
                  mspec(2), mspec(3), mspec(4), mspec(5)] + [_const_spec(a.shape) for a in consts],
        out_specs=pl.BlockSpec((1, tc, D), lambda n, c: (n, c, 0)),
        out_shape=jax.ShapeDtypeStruct((nb, L, D), F32),
        compiler_params=_cparams(("arbitrary", "arbitrary")),
        name="stage3_prompt",
    )(x, mix, mod, mod, mod, mod, *consts)


SAMPLE_EXPERTS_PER_STEP = 2


def _stage3_sample_body(x_ref, mix_ref, g1_ref, sh2_ref, sc2_ref, g2_ref, n2g_ref, fng_ref, wout_ref, rw_ref, rb_ref,
                        wg_ref, wu_ref, wd_ref, y_ref, x1_scr, h2b_scr, gates_scr, acc_scr, *, S, L):
    e = pl.program_id(0)
    R = S * L

    def rows(r):
        return jnp.concatenate([r[...]] * L, axis=0)

    @pl.when(e == 0)
    def _prologue():
        x_tm = jnp.concatenate([x_ref[:, t, :] for t in range(L)], axis=0)
        mixo = jnp.dot(mix_ref[...], wout_ref[...], preferred_element_type=F32)
        x1 = x_tm + rows(g1_ref) * mixo
        h2 = _rms(x1, n2g_ref[...]) * (1.0 + rows(sc2_ref)) + rows(sh2_ref)
        h2_hi, gates = _route(h2, rw_ref, rb_ref)
        x1_scr[...] = x1
        h2b_scr[...] = h2_hi
        gates_scr[...] = gates
        acc_scr[...] = jnp.zeros((R, D_MODEL), F32)

    lane = lax.broadcasted_iota(jnp.int32, (R, ROUTER_LANES), 1)
    gates = gates_scr[...]
    h2b = h2b_scr[...]
    acc = acc_scr[...]
    for j in range(SAMPLE_EXPERTS_PER_STEP):
        ge = jnp.sum(jnp.where(lane == N_EXPERT_GROUPS + e * SAMPLE_EXPERTS_PER_STEP + j, gates, 0.0),
                     axis=-1, keepdims=True)
        a = jnp.dot(h2b, wg_ref[j], preferred_element_type=F32)
        b = jnp.dot(h2b, wu_ref[j], preferred_element_type=F32)
        hid = jax.nn.silu(a) * b * ge
        acc = acc + jnp.dot(hid.astype(BF16), wd_ref[j], preferred_element_type=F32)
    acc_scr[...] = acc

    @pl.when(e == pl.num_programs(0) - 1)
    def _epilogue():
        y = _rms(x1_scr[...] + rows(g2_ref) * acc_scr[...], fng_ref[...])
        for t in range(L):
            y_ref[:, t, :] = y[t * S:(t + 1) * S, :]


def _stage3_sample(x, mix_tm, mod, n2g, fng, w_out, rw, rb, w_gate, w_up, w_down):
    ns, L, D = x.shape
    E, _, F = w_gate.shape
    R = ns * L
    consts = [n2g, fng, w_out, rw, rb]

    def mspec(k):
        return pl.BlockSpec((ns, D), lambda e, k=k: (0, k), pipeline_mode=pl.Buffered(1))

    return pl.pallas_call(
        functools.partial(_stage3_sample_body, S=ns, L=L),
        grid=(E // SAMPLE_EXPERTS_PER_STEP,),
        in_specs=[_const_spec((ns, L, D)), _const_spec((R, D)),
                  mspec(2), mspec(3), mspec(4), mspec(5)] + [_const_spec(a.shape) for a in consts]
                 + [pl.BlockSpec((SAMPLE_EXPERTS_PER_STEP, D, F), lambda e: (e, 0, 0)),
                    pl.BlockSpec((SAMPLE_EXPERTS_PER_STEP, D, F), lambda e: (e, 0, 0)),
                    pl.BlockSpec((SAMPLE_EXPERTS_PER_STEP, F, D), lambda e: (e, 0, 0))],
        out_specs=pl.BlockSpec((ns, L, D), lambda e: (0, 0, 0)),
        out_shape=jax.ShapeDtypeStruct((ns, L, D), F32),
        scratch_shapes=[pltpu.VMEM((R, D), F32), pltpu.VMEM((R, D), BF16),
                        pltpu.VMEM((R, ROUTER_LANES), F32), pltpu.VMEM((R, D), F32)],
        compiler_params=_cparams(("arbitrary",)),
        name="stage3_sample",
    )(x, mix_tm, mod, mod, mod, mod, *consts, w_gate, w_up, w_down)


def kernel(x_prompt, x_sample, c_prompt, c_sample, state_pool, state_ssm_re, state_ssm_im, w_ada, b_ada, norm1_g, w_in, pool_w, pool_scale, ssm_a_re, ssm_a_im, ssm_log_dt, ssm_b_re, ssm_b_im, ssm_c_re, ssm_c_im, ssm_d, glu_w, glu_b, w_out, norm2_g, router_w1, router_b1, router_w2, router_b2, exp_w_gate, exp_w_up, exp_w_down, final_norm_g):
    depth = w_ada.shape[0]
    assert depth == 1
    l = 0
    nb, L, D = x_prompt.shape
    ns, Ls, _ = x_sample.shape

    lb_re, lb_im, d_flat, wbre, wbim, wcre, wcim = _ssm_prep(
        ssm_a_re[l], ssm_a_im[l], ssm_log_dt[l],
        jnp.transpose(ssm_b_re[l], (0, 2, 1)), jnp.transpose(ssm_b_im[l], (0, 2, 1)),
        jnp.transpose(ssm_c_re[l], (0, 2, 1)), jnp.transpose(ssm_c_im[l], (0, 2, 1)), ssm_d[l])
    mix_wts = (pool_w[l], pool_scale[l].reshape(1, -1), lb_re, lb_im, wbre, wbim, wcre, wcim, d_flat,
               glu_w[l], glu_b[l].reshape(1, -1))

    rw = jnp.concatenate([router_w1[l], jnp.transpose(router_w2[l], (1, 0, 2)).reshape(D, N_EXPERTS)], axis=1)
    rw = jnp.pad(rw, ((0, 0), (0, ROUTER_LANES - rw.shape[1])))
    rb = jnp.concatenate([router_b1[l], router_b2[l].reshape(-1)])
    rb = jnp.pad(rb, (0, ROUTER_LANES - rb.shape[0])).reshape(1, -1)
    s3_wts = (norm2_g[l].reshape(1, -1), final_norm_g.reshape(1, -1), w_out[l].astype(BF16), rw, rb,
              exp_w_gate[l].astype(BF16), exp_w_up[l].astype(BF16), exp_w_down[l].astype(BF16))

    mod_p, mod_s = _adaln(c_prompt, c_sample, w_ada[l], b_ada[l])
    g1 = norm1_g[l].reshape(1, -1)

    mix_p, pool_p, hre_p, him_p = _mixer(x_prompt, mod_p, g1, w_in[l], None, mix_wts,
                                         Tt=128, Ts=32, start_pos=0, seq_major=True)
    y_p = _stage3_prompt(x_prompt, mix_p, mod_p, s3_wts, 512)

    state = (state_pool[l], state_ssm_re[l].reshape(ns, N_STATE), state_ssm_im[l].reshape(ns, N_STATE))
    mix_s, pool_s, hre_s, him_s = _mixer(x_sample, mod_s, g1, w_in[l], state, mix_wts,
                                         Tt=Ls, Ts=Ls // 2, start_pos=PAST_LEN, seq_major=False)
    y_s = _stage3_sample(x_sample, mix_s, mod_s, *s3_wts)

    def st(a, n):
        return a.reshape(1, n, N_SSM_GROUPS, SSM_STATE)

    return (y_p, y_s, pool_p[None], pool_s[None], st(hre_p, nb), st(him_p, nb), st(hre_s, ns), st(him_s, ns))
```

```python
import functools

import jax
import jax.numpy as jnp
from jax import lax
from jax.experimental import pallas as pl
from jax.experimental.pallas import tpu as pltpu

D_MODEL = 1024
POOL_WIDTH = 512
SSM_WIDTH = 512
POOL_WINDOWS = (2, 4, 8, 16)
POOL_GROUP = 128
POOL_BUF = 15
HIST = 16
SSM_GROUP = 16
N_SSM_GROUPS = 32
SSM_STATE = 64
N_STATE = N_SSM_GROUPS * SSM_STATE
N_EXPERT_GROUPS = 4
EXPERTS_PER_GROUP = 4
N_EXPERTS = 16
EXPERT_HIDDEN = 256
N_MOD = 6
EPS = 1e-6
PAST_LEN = 16384

MXU_TILE = 256
LANES = 128
SUBLANES = 8
ROUTER_LANES = 128
VMEM_LIMIT = 60 * 1024 * 1024
N_STATE_TILES = N_STATE // MXU_TILE

F32 = jnp.float32
BF16 = jnp.bfloat16


def _cparams(sem):
    return pltpu.CompilerParams(dimension_semantics=sem, vmem_limit_bytes=VMEM_LIMIT)


def _const_spec(shape):
    nd = len(shape)
    return pl.BlockSpec(shape, lambda *_: (0,) * nd, pipeline_mode=pl.Buffered(1))


def _rms(x, g):
    return x * lax.rsqrt(jnp.mean(x * x, axis=-1, keepdims=True) + EPS) * g


def _ssm_prep_body(are_ref, aim_ref, ldt_ref, bre_ref, bim_ref, cre_ref, cim_ref, d_ref,
                   lbre_ref, lbim_ref, dflat_ref, wbre_ref, wbim_ref, wcre_ref, wcim_ref):
    G, H, P = N_SSM_GROUPS, SSM_GROUP, SSM_STATE
    a_re = are_ref[...]
    a_im = aim_ref[...]
    dt = jnp.exp(ldt_ref[...])
    mag = jnp.exp(a_re * dt)
    lb_re = mag * jnp.cos(a_im * dt)
    lb_im = mag * jnp.sin(a_im * dt)
    den = a_re * a_re + a_im * a_im
    nr = lb_re - 1.0
    ni = lb_im
    k_re = (nr * a_re + ni * a_im) / den
    k_im = (ni * a_re - nr * a_im) / den
    b_re = bre_ref[...]
    b_im = bim_ref[...]
    bb_re = k_re[:, None, :] * b_re - k_im[:, None, :] * b_im
    bb_im = k_re[:, None, :] * b_im + k_im[:, None, :] * b_re
    c_re = cre_ref[...]
    c_im = cim_ref[...]
    d = d_ref[...]

    zeros = jnp.zeros((N_STATE_TILES, MXU_TILE, MXU_TILE), BF16)
    wbre_ref[...] = zeros
    wbim_ref[...] = zeros
    wcre_ref[...] = zeros
    wcim_ref[...] = zeros
    g_per_tile = MXU_TILE // P
    g_per_blk = MXU_TILE // H
    for g in range(G):
        lbre_ref[:, g * P:(g + 1) * P] = lb_re[g:g + 1, :]
        lbim_ref[:, g * P:(g + 1) * P] = lb_im[g:g + 1, :]
        dflat_ref[:, g * H:(g + 1) * H] = d[g:g + 1, :]
        n, gi = divmod(g, g_per_tile)
        r0 = (g % g_per_blk) * H
        c0 = gi * P
        wbre_ref[n, r0:r0 + H, c0:c0 + P] = bb_re[g].astype(BF16)
        wbim_ref[n, r0:r0 + H, c0:c0 + P] = bb_im[g].astype(BF16)
        wcre_ref[n, c0:c0 + P, r0:r0 + H] = c_re[g].astype(BF16)
        wcim_ref[n, c0:c0 + P, r0:r0 + H] = (-c_im[g]).astype(BF16)


def _ssm_prep(a_re, a_im, log_dt, b_re_t, b_im_t, c_re_t, c_im_t, d):
    G = N_SSM_GROUPS
    tile = jax.ShapeDtypeStruct((N_STATE_TILES, MXU_TILE, MXU_TILE), BF16)
    return pl.pallas_call(
        _ssm_prep_body,
        out_shape=(jax.ShapeDtypeStruct((1, N_STATE), F32), jax.ShapeDtypeStruct((1, N_STATE), F32),
                   jax.ShapeDtypeStruct((1, SSM_WIDTH), F32), tile, tile, tile, tile),
        name="ssm_prep",
    )(a_re, a_im, log_dt.reshape(G, 1), b_re_t, b_im_t, c_re_t, c_im_t, d)


def _adaln_body(cp_ref, cs_ref, w_ref, b_ref, op_ref, os_ref):
    n_p = cp_ref.shape[0]
    s = jax.nn.silu(jnp.concatenate([cp_ref[...], cs_ref[...]], axis=0)).astype(BF16)
    mod = jnp.dot(s, w_ref[...].astype(BF16), preferred_element_type=F32) + b_ref[...]
    op_ref[...] = mod[:n_p]
    os_ref[...] = mod[n_p:]


def _adaln(c_p, c_s, w_ada, b_ada):
    n_p, n_s = c_p.shape[0], c_s.shape[0]
    tn = D_MODEL
    return pl.pallas_call(
        _adaln_body,
        grid=(N_MOD * D_MODEL // tn,),
        in_specs=[pl.BlockSpec((n_p, D_MODEL), lambda j: (0, 0)),
                  pl.BlockSpec((n_s, D_MODEL), lambda j: (0, 0)),
                  pl.BlockSpec((D_MODEL, tn), lambda j: (0, j)),
                  pl.BlockSpec((1, tn), lambda j: (0, j))],
        out_specs=(pl.BlockSpec((n_p, tn), lambda j: (0, j)), pl.BlockSpec((n_s, tn), lambda j: (0, j))),
        out_shape=(jax.ShapeDtypeStruct((n_p, N_MOD * D_MODEL), F32),
                   jax.ShapeDtypeStruct((n_s, N_MOD * D_MODEL), F32)),
        compiler_params=_cparams(("arbitrary",)),
        name="adaln",
    )(c_p, c_s, w_ada, b_ada.reshape(1, -1))


def _mixer_body(*refs, S, Tt, Ts, start_pos, seq_major, has_state):
    refs = list(refs)
    x_ref, sh_ref, sc_ref, g_ref, win_ref = refs[:5]
    k = 5
    if has_state:
        buf0_ref, hre0_ref, him0_ref = refs[k:k + 3]
        k += 3
    (poolw_ref, pscale_ref, lbre_ref, lbim_ref, wbre_ref, wbim_ref, wcre_ref, wcim_ref, d_ref, gluw_ref,
     glub_ref) = refs[k:k + 11]
    k += 11
    mix_ref, newbuf_ref, hre_out_ref, him_out_ref = refs[k:k + 4]
    z_scr, sre_scr, sim_scr, hre_scr, him_scr, winb_scr, glub_scr = refs[k + 4:]

    i = pl.program_id(0)
    R = Tt * S
    HR = HIST * S
    D = D_MODEL

    @pl.when(i == 0)
    def _init():
        z_scr[0:S, :] = jnp.zeros((S, POOL_WIDTH), F32)
        if has_state:
            for j in range(POOL_BUF):
                z_scr[(j + 1) * S:(j + 2) * S, :] = buf0_ref[:, j, :]
            hre_scr[...] = hre0_ref[...]
            him_scr[...] = him0_ref[...]
        else:
            z_scr[S:HR, :] = jnp.zeros((HR - S, POOL_WIDTH), F32)
            hre_scr[...] = jnp.zeros((S, N_STATE), F32)
            him_scr[...] = jnp.zeros((S, N_STATE), F32)
        winb_scr[...] = win_ref[...].astype(BF16)
        glub_scr[...] = gluw_ref[...].astype(BF16)

    g = g_ref[...]
    Rs = Ts * S
    CB = 512
    n_tiles = S // SUBLANES
    n_ct = SSM_WIDTH // MXU_TILE
    k_per = N_STATE_TILES // n_ct
    for sub in range(Tt // Ts):
        t0 = sub * Ts
        r_lo = sub * Rs

        if seq_major:
            x3 = x_ref[:, t0:t0 + Ts, :]
            h3 = _rms(x3, g) * (1.0 + sc_ref[...][:, None, :]) + sh_ref[...][:, None, :]
            u_nm = jnp.dot(h3.reshape(Rs, D).astype(BF16), winb_scr[...], preferred_element_type=F32)
            u = jnp.swapaxes(u_nm.reshape(S, Ts, D), 0, 1).reshape(Rs, D)
        else:
            x_tm = jnp.concatenate([x_ref[:, t, :] for t in range(t0, t0 + Ts)], axis=0)
            h3 = _rms(x_tm, g).reshape(Ts, S, D) * (1.0 + sc_ref[...]) + sh_ref[...]
            u = jnp.dot(h3.reshape(Rs, D).astype(BF16), winb_scr[...], preferred_element_type=F32)

        up = u[:, 0:POOL_WIDTH]
        us = u[:, POOL_WIDTH:D]
        z_scr[HR + r_lo:HR + r_lo + Rs, :] = up

        row = lax.broadcasted_iota(jnp.int32, (Rs, 1), 0)
        pos = start_pos + i * Tt + t0 + lax.shift_right_logical(row, S.bit_length() - 1)
        outs = []
        for kk, w in enumerate(POOL_WINDOWS):
            lo, hi = kk * POOL_GROUP, (kk + 1) * POOL_GROUP
            cur = z_scr[r_lo:r_lo + HR + Rs, lo:hi]
            step = 1
            while step < w:
                cur = cur[step * S:, :] + cur[:cur.shape[0] - step * S, :]
                step *= 2
            s = cur[cur.shape[0] - Rs:, :]
            cnt = jnp.minimum(w, pos + 1).astype(F32)
            pooled = s / cnt - up[:, lo:hi]
            mixed = jnp.dot(pooled.astype(BF16), poolw_ref[kk].astype(BF16), preferred_element_type=F32)
            outs.append(mixed * pscale_ref[:, lo:hi])

        usb = us.astype(BF16)
        for n in range(N_STATE_TILES):
            kb = (n * MXU_TILE // SSM_STATE * SSM_GROUP) // MXU_TILE
            lhs = usb[:, kb * MXU_TILE:(kb + 1) * MXU_TILE]
            cols = slice(n * MXU_TILE, (n + 1) * MXU_TILE)
            sre_scr[r_lo:r_lo + Rs, cols] = jnp.dot(lhs, wbre_ref[n], preferred_element_type=F32)
            sim_scr[r_lo:r_lo + Rs, cols] = jnp.dot(lhs, wbim_ref[n], preferred_element_type=F32)

        for cb in range(N_STATE // CB):
            c0 = cb * CB
            lr = jnp.broadcast_to(lbre_ref[:, c0:c0 + CB], (SUBLANES, CB))
            li = jnp.broadcast_to(lbim_ref[:, c0:c0 + CB], (SUBLANES, CB))

            def scan_tile(s0, c0=c0, lr=lr, li=li, r_lo=r_lo):
                hr = hre_scr[pl.ds(s0, SUBLANES), c0:c0 + CB]
                hi_ = him_scr[pl.ds(s0, SUBLANES), c0:c0 + CB]
                for t in range(Ts):
                    r0 = r_lo + t * S + s0
                    br = sre_scr[pl.ds(r0, SUBLANES), c0:c0 + CB]
                    bi = sim_scr[pl.ds(r0, SUBLANES), c0:c0 + CB]
                    hr, hi_ = lr * hr - li * hi_ + br, lr * hi_ + li * hr + bi
                    sre_scr[pl.ds(r0, SUBLANES), c0:c0 + CB] = hr
                    sim_scr[pl.ds(r0, SUBLANES), c0:c0 + CB] = hi_
                hre_scr[pl.ds(s0, SUBLANES), c0:c0 + CB] = hr
                him_scr[pl.ds(s0, SUBLANES), c0:c0 + CB] = hi_

            if n_tiles == 1:
                scan_tile(0)
            else:
                def tile_body(j, carry, scan_tile=scan_tile):
                    scan_tile(pl.multiple_of(j * SUBLANES, SUBLANES))
                    return carry

                lax.fori_loop(0, n_tiles, tile_body, 0)

        ys = []
        for m in range(n_ct):
            acc = d_ref[:, m * MXU_TILE:(m + 1) * MXU_TILE] * us[:, m * MXU_TILE:(m + 1) * MXU_TILE]
            for kk in range(k_per):
                kt = m * k_per + kk
                cols = slice(kt * MXU_TILE, (kt + 1) * MXU_TILE)
                acc = acc + jnp.dot(sre_scr[r_lo:r_lo + Rs, cols].astype(BF16), wcre_ref[kt],
                                    preferred_element_type=F32)
                acc = acc + jnp.dot(sim_scr[r_lo:r_lo + Rs, cols].astype(BF16), wcim_ref[kt],
                                    preferred_element_type=F32)
            ys.append(acc)
        y = jnp.concatenate(ys, axis=-1)
        gl = jax.nn.gelu(y)
        gate = jax.nn.sigmoid(jnp.dot(gl.astype(BF16), glub_scr[...], preferred_element_type=F32) + glub_ref[...])
        outs.append(gl * gate)

        mix_tm = jnp.concatenate(outs, axis=-1)
        if seq_major:
            mix_ref[:, t0:t0 + Ts, :] = jnp.swapaxes(mix_tm.reshape(Ts, S, D), 0, 1).astype(mix_ref.dtype)
        else:
            mix_ref[r_lo:r_lo + Rs, :] = mix_tm.astype(mix_ref.dtype)

    for j in range(POOL_BUF):
        r0 = (Tt + 1 + j) * S
        newbuf_ref[:, j, :] = z_scr[r0:r0 + S, :]
    hist = z_scr[R:R + HR, :]
    z_scr[0:HR, :] = hist
    hre_out_ref[...] = hre_scr[...]
    him_out_ref[...] = him_scr[...]


def _mixer(x, mod, g1, w_in, state, wts, *, Tt, Ts, start_pos, seq_major):
    S, L, D = x.shape
    R = Tt * S
    has_state = state is not None
    consts = [g1, w_in] + (list(state) if has_state else []) + list(wts)
    x_spec = pl.BlockSpec((S, Tt, D), lambda i: (0, i, 0))
    mod_specs = [pl.BlockSpec((S, D), lambda i: (0, 0)), pl.BlockSpec((S, D), lambda i: (0, 1))]
    if seq_major:
        mix_spec = pl.BlockSpec((S, Tt, D), lambda i: (0, i, 0))
        mix_shape = jax.ShapeDtypeStruct((S, L, D), BF16)
    else:
        assert Tt == L
        mix_spec = pl.BlockSpec((R, D), lambda i: (0, 0))
        mix_shape = jax.ShapeDtypeStruct((L * S, D), BF16)
    body = functools.partial(_mixer_body, S=S, Tt=Tt, Ts=Ts, start_pos=start_pos, seq_major=seq_major,
                             has_state=has_state)
    return pl.pallas_call(
        body,
        grid=(L // Tt,),
        in_specs=[x_spec] + mod_specs + [_const_spec(a.shape) for a in consts],
        out_specs=(mix_spec,
                   pl.BlockSpec((S, POOL_BUF, POOL_WIDTH), lambda i: (0, 0, 0)),
                   pl.BlockSpec((S, N_STATE), lambda i: (0, 0)),
                   pl.BlockSpec((S, N_STATE), lambda i: (0, 0))),
        out_shape=(mix_shape,
                   jax.ShapeDtypeStruct((S, POOL_BUF, POOL_WIDTH), F32),
                   jax.ShapeDtypeStruct((S, N_STATE), F32),
                   jax.ShapeDtypeStruct((S, N_STATE), F32)),
        scratch_shapes=[pltpu.VMEM(((HIST + Tt) * S, POOL_WIDTH), F32),
                        pltpu.VMEM((R, N_STATE), F32),
                        pltpu.VMEM((R, N_STATE), F32),
                        pltpu.VMEM((S, N_STATE), F32),
                        pltpu.VMEM((S, N_STATE), F32),
                        pltpu.VMEM((D, D), BF16),
                        pltpu.VMEM((SSM_WIDTH, SSM_WIDTH), BF16)],
        compiler_params=_cparams(("arbitrary",)),
        name="mixer_S%d" % S,
    )(x, mod, mod, *consts)


def _split_bf16(v):
    hi = v.astype(BF16)
    lo = (v - hi.astype(F32)).astype(BF16)
    return hi, lo


def _route(h2, rw_ref, rb_ref):
    R = h2.shape[0]
    h2_hi, h2_lo = _split_bf16(h2)
    rw_hi, rw_lo = _split_bf16(rw_ref[...])
    logits = (jnp.dot(h2_hi, rw_hi, preferred_element_type=F32)
              + jnp.dot(h2_lo, rw_hi, preferred_element_type=F32)
              + jnp.dot(h2_hi, rw_lo, preferred_element_type=F32)) + rb_ref[...]
    lane = lax.broadcasted_iota(jnp.int32, (R, ROUTER_LANES), 1).astype(F32)
    ninf = jnp.float32(-jnp.inf)
    none = jnp.float32(ROUTER_LANES)
    is_g = lane < N_EXPERT_GROUPS
    l1 = jnp.where(is_g, logits, ninf)
    m1 = jnp.max(l1, axis=-1, keepdims=True)
    gidx = jnp.min(jnp.where(l1 == m1, lane, none), axis=-1, keepdims=True)
    p_top = 1.0 / jnp.sum(jnp.where(is_g, jnp.exp(logits - m1), 0.0), axis=-1, keepdims=True)
    e_lo = N_EXPERT_GROUPS + gidx * EXPERTS_PER_GROUP
    sel = (lane >= e_lo) & (lane < e_lo + EXPERTS_PER_GROUP)
    l2 = jnp.where(sel, logits, ninf)
    va = jnp.max(l2, axis=-1, keepdims=True)
    ia = jnp.min(jnp.where(l2 == va, lane, none), axis=-1, keepdims=True)
    l2b = jnp.where(lane == ia, ninf, l2)
    vb = jnp.max(l2b, axis=-1, keepdims=True)
    ib = jnp.min(jnp.where(l2b == vb, lane, none), axis=-1, keepdims=True)
    eb = jnp.exp(vb - va)
    den = 1.0 + eb
    gates = jnp.where(lane == ia, (1.0 / den) * p_top, 0.0) + jnp.where(lane == ib, (eb / den) * p_top, 0.0)
    return h2_hi, gates


def _stage3_rows(x, mix, g1, sh2, sc2, g2, n2g_ref, fng_ref, wout_ref, rw_ref, rb_ref, wg_ref, wu_ref, wd_ref):
    R = x.shape[0]
    mixo = jnp.dot(mix, wout_ref[...], preferred_element_type=F32)
    x1 = x + g1 * mixo
    h2 = _rms(x1, n2g_ref[...]) * (1.0 + sc2) + sh2
    h2_hi, gates = _route(h2, rw_ref, rb_ref)

    acc = jnp.zeros((R, D_MODEL), F32)
    for e in range(N_EXPERTS):
        a = jnp.dot(h2_hi, wg_ref[e], preferred_element_type=F32)
        b = jnp.dot(h2_hi, wu_ref[e], preferred_element_type=F32)
        ge = gates[:, N_EXPERT_GROUPS + e:N_EXPERT_GROUPS + e + 1]
        hid = jax.nn.silu(a) * b * ge
        acc = acc + jnp.dot(hid.astype(BF16), wd_ref[e], preferred_element_type=F32)
    x2 = x1 + g2 * acc
    return _rms(x2, fng_ref[...])


def _stage3_prompt_body(x_ref, mix_ref, g1_ref, sh2_ref, sc2_ref, g2_ref, *rest):
    wrefs, y_ref = rest[:-1], rest[-1]
    n = pl.program_id(0)
    mods = [r[pl.ds(n, 1), :] for r in (g1_ref, sh2_ref, sc2_ref, g2_ref)]
    y_ref[0] = _stage3_rows(x_ref[0], mix_ref[0], *mods, *wrefs)


def _stage3_prompt(x, mix, mod, wts, tc):
    nb, L, D = x.shape
    consts = list(wts)

    def mspec(k):
        return pl.BlockSpec((nb, D), lambda n, c, k=k: (0, k))

    return pl.pallas_call(
        _stage3_prompt_body,
        grid=(nb, L // tc),
        in_specs=[pl.BlockSpec((1, tc, D), lambda n, c: (n, c, 0)),
                  pl.BlockSpec((1, tc, D), lambda n, c: (n, c, 0)),
                  mspec(2), mspec(3), mspec(4), mspec(5)] + [_const_spec(a.shape) for a in consts],
        out_specs=pl.BlockSpec((1, tc, D), lambda n, c: (n, c, 0)),
        out_shape=jax.ShapeDtypeStruct((nb, L, D), F32),
        compiler_params=_cparams(("arbitrary", "arbitrary")),
        name="stage3_prompt",
    )(x, mix, mod, mod, mod, mod, *consts)


SAMPLE_EXPERTS_PER_STEP = 2


def _stage3_sample_body(x_ref, mix_ref, g1_ref, sh2_ref, sc2_ref, g2_ref, n2g_ref, fng_ref, wout_ref, rw_ref, rb_ref,
                        wg_ref, wu_ref, wd_ref, y_ref, x1_scr, h2b_scr, gates_scr, acc_scr, *, S, L):
    e = pl.program_id(0)
    R = S * L

    def rows(r):
        return jnp.concatenate([r[...]] * L, axis=0)

    @pl.when(e == 0)
    def _prologue():
        x_tm = jnp.concatenate([x_ref[:, t, :] for t in range(L)], axis=0)
        mixo = jnp.dot(mix_ref[...], wout_ref[...], preferred_element_type=F32)
        x1 = x_tm + rows(g1_ref) * mixo
        h2 = _rms(x1, n2g_ref[...]) * (1.0 + rows(sc2_ref)) + rows(sh2_ref)
        h2_hi, gates = _route(h2, rw_ref, rb_ref)
        x1_scr[...] = x1
        h2b_scr[...] = h2_hi
        gates_scr[...] = gates
        acc_scr[...] = jnp.zeros((R, D_MODEL), F32)

    lane = lax.broadcasted_iota(jnp.int32, (R, ROUTER_LANES), 1)
    gates = gates_scr[...]
    h2b = h2b_scr[...]
    acc = acc_scr[...]
    for j in range(SAMPLE_EXPERTS_PER_STEP):
        ge = jnp.sum(jnp.where(lane == N_EXPERT_GROUPS + e * SAMPLE_EXPERTS_PER_STEP + j, gates, 0.0),
                     axis=-1, keepdims=True)
        a = jnp.dot(h2b, wg_ref[j], preferred_element_type=F32)
        b = jnp.dot(h2b, wu_ref[j], preferred_element_type=F32)
        hid = jax.nn.silu(a) * b * ge
        acc = acc + jnp.dot(hid.astype(BF16), wd_ref[j], preferred_element_type=F32)
    acc_scr[...] = acc

    @pl.when(e == pl.num_programs(0) - 1)
    def _epilogue():
        y = _rms(x1_scr[...] + rows(g2_ref) * acc_scr[...], fng_ref[...])
        for t in range(L):
            y_ref[:, t, :] = y[t * S:(t + 1) * S, :]


def _stage3_sample(x, mix_tm, mod, n2g, fng, w_out, rw, rb, w_gate, w_up, w_down):
    ns, L, D = x.shape
    E, _, F = w_gate.shape
    R = ns * L
    consts = [n2g, fng, w_out, rw, rb]

    def mspec(k):
        return pl.BlockSpec((ns, D), lambda e, k=k: (0, k), pipeline_mode=pl.Buffered(1))

    return pl.pallas_call(
        functools.partial(_stage3_sample_body, S=ns, L=L),
        grid=(E // SAMPLE_EXPERTS_PER_STEP,),
        in_specs=[_const_spec((ns, L, D)), _const_spec((R, D)),
                  mspec(2), mspec(3), mspec(4), mspec(5)] + [_const_spec(a.shape) for a in consts]
                 + [pl.BlockSpec((SAMPLE_EXPERTS_PER_STEP, D, F), lambda e: (e, 0, 0)),
                    pl.BlockSpec((SAMPLE_EXPERTS_PER_STEP, D, F), lambda e: (e, 0, 0)),
                    pl.BlockSpec((SAMPLE_EXPERTS_PER_STEP, F, D), lambda e: (e, 0, 0))],
        out_specs=pl.BlockSpec((ns, L, D), lambda e: (0, 0, 0)),
        out_shape=jax.ShapeDtypeStruct((ns, L, D), F32),
        scratch_shapes=[pltpu.VMEM((R, D), F32), pltpu.VMEM((R, D), BF16),
                        pltpu.VMEM((R, ROUTER_LANES), F32), pltpu.VMEM((R, D), F32)],
        compiler_params=_cparams(("arbitrary",)),
        name="stage3_sample",
    )(x, mix_tm, mod, mod, mod, mod, *consts, w_gate, w_up, w_down)


def kernel(x_prompt, x_sample, c_prompt, c_sample, state_pool, state_ssm_re, state_ssm_im, w_ada, b_ada, norm1_g, w_in, pool_w, pool_scale, ssm_a_re, ssm_a_im, ssm_log_dt, ssm_b_re, ssm_b_im, ssm_c_re, ssm_c_im, ssm_d, glu_w, glu_b, w_out, norm2_g, router_w1, router_b1, router_w2, router_b2, exp_w_gate, exp_w_up, exp_w_down, final_norm_g):
    depth = w_ada.shape[0]
    assert depth == 1
    l = 0
    nb, L, D = x_prompt.shape
    ns, Ls, _ = x_sample.shape

    lb_re, lb_im, d_flat, wbre, wbim, wcre, wcim = _ssm_prep(
        ssm_a_re[l], ssm_a_im[l], ssm_log_dt[l],
        jnp.transpose(ssm_b_re[l], (0, 2, 1)), jnp.transpose(ssm_b_im[l], (0, 2, 1)),
        jnp.transpose(ssm_c_re[l], (0, 2, 1)), jnp.transpose(ssm_c_im[l], (0, 2, 1)), ssm_d[l])
    mix_wts = (pool_w[l], pool_scale[l].reshape(1, -1), lb_re, lb_im, wbre, wbim, wcre, wcim, d_flat,
               glu_w[l], glu_b[l].reshape(1, -1))

    rw = jnp.concatenate([router_w1[l], jnp.transpose(router_w2[l], (1, 0, 2)).reshape(D, N_EXPERTS)], axis=1)
    rw = jnp.pad(rw, ((0, 0), (0, ROUTER_LANES - rw.shape[1])))
    rb = jnp.concatenate([router_b1[l], router_b2[l].reshape(-1)])
    rb = jnp.pad(rb, (0, ROUTER_LANES - rb.shape[0])).reshape(1, -1)
    s3_wts = (norm2_g[l].reshape(1, -1), final_norm_g.reshape(1, -1), w_out[l].astype(BF16), rw, rb,
              exp_w_gate[l].astype(BF16), exp_w_up[l].astype(BF16), exp_w_down[l].astype(BF16))

    mod_p, mod_s = _adaln(c_prompt, c_sample, w_ada[l], b_ada[l])
    g1 = norm1_g[l].reshape(1, -1)

    mix_p, pool_p, hre_p, him_p = _mixer(x_prompt, mod_p, g1, w_in[l], None, mix_wts,
                                         Tt=128, Ts=32, start_pos=0, seq_major=True)
    y_p = _stage3_prompt(x_prompt, mix_p, mod_p, s3_wts, 512)

    state = (state_pool[l], state_ssm_re[l].reshape(ns, N_STATE), state_ssm_im[l].reshape(ns, N_STATE))
    mix_s, pool_s, hre_s, him_s = _mixer(x_sample, mod_s, g1, w_in[l], state, mix_wts,
                                         Tt=Ls, Ts=Ls // 2, start_pos=PAST_LEN, seq_major=False)
    y_s = _stage3_sample(x_sample, mix_s, mod_s, *s3_wts)

    def st(a, n):
        return a.reshape(1, n, N_SSM_GROUPS, SSM_STATE)

    return (y_p, y_s, pool_p[None], pool_s[None], st(hre_p, nb), st(him_p, nb), st(hre_s, ns), st(him_s, ns))
```

```python
import functools

import jax
import jax.numpy as jnp
from jax import lax
from jax.experimental import pallas as pl
from jax.experimental.pallas import tpu as pltpu

D_MODEL = 1024
POOL_WIDTH = 512
SSM_WIDTH = 512
POOL_WINDOWS = (2, 4, 8, 16)
POOL_GROUP = 128
POOL_BUF = 15
HIST = 16
SSM_GROUP = 16
N_SSM_GROUPS = 32
SSM_STATE = 64
N_STATE = N_SSM_GROUPS * SSM_STATE
N_EXPERT_GROUPS = 4
EXPERTS_PER_GROUP = 4
N_EXPERTS = 16
EXPERT_HIDDEN = 256
N_MOD = 6
EPS = 1e-6
PAST_LEN = 16384

MXU_TILE = 256
LANES = 128
SUBLANES = 8
ROUTER_LANES = 128
VMEM_LIMIT = 60 * 1024 * 1024
N_STATE_TILES = N_STATE // MXU_TILE

F32 = jnp.float32
BF16 = jnp.bfloat16


def _cparams(sem):
    return pltpu.CompilerParams(dimension_semantics=sem, vmem_limit_bytes=VMEM_LIMIT)


def _const_spec(shape):
    nd = len(shape)
    return pl.BlockSpec(shape, lambda *_: (0,) * nd, pipeline_mode=pl.Buffered(1))


def _rms(x, g):
    return x * lax.rsqrt(jnp.mean(x * x, axis=-1, keepdims=True) + EPS) * g


def _ssm_prep_body(are_ref, aim_ref, ldt_ref, bre_ref, bim_ref, cre_ref, cim_ref, d_ref,
                   lbre_ref, lbim_ref, dflat_ref, wbre_ref, wbim_ref, wcre_ref, wcim_ref):
    G, H, P = N_SSM_GROUPS, SSM_GROUP, SSM_STATE
    a_re = are_ref[...]
    a_im = aim_ref[...]
    dt = jnp.exp(ldt_ref[...])
    mag = jnp.exp(a_re * dt)
    lb_re = mag * jnp.cos(a_im * dt)
    lb_im = mag * jnp.sin(a_im * dt)
    den = a_re * a_re + a_im * a_im
    nr = lb_re - 1.0
    ni = lb_im
    k_re = (nr * a_re + ni * a_im) / den
    k_im = (ni * a_re - nr * a_im) / den
    b_re = bre_ref[...]
    b_im = bim_ref[...]
    bb_re = k_re[:, None, :] * b_re - k_im[:, None, :] * b_im
    bb_im = k_re[:, None, :] * b_im + k_im[:, None, :] * b_re
    c_re = cre_ref[...]
    c_im = cim_ref[...]
    d = d_ref[...]

    zeros = jnp.zeros((N_STATE_TILES, MXU_TILE, MXU_TILE), BF16)
    wbre_ref[...] = zeros
    wbim_ref[...] = zeros
    wcre_ref[...] = zeros
    wcim_ref[...] = zeros
    g_per_tile = MXU_TILE // P
    g_per_blk = MXU_TILE // H
    for g in range(G):
        lbre_ref[:, g * P:(g + 1) * P] = lb_re[g:g + 1, :]
        lbim_ref[:, g * P:(g + 1) * P] = lb_im[g:g + 1, :]
        dflat_ref[:, g * H:(g + 1) * H] = d[g:g + 1, :]
        n, gi = divmod(g, g_per_tile)
        r0 = (g % g_per_blk) * H
        c0 = gi * P
        wbre_ref[n, r0:r0 + H, c0:c0 + P] = bb_re[g].astype(BF16)
        wbim_ref[n, r0:r0 + H, c0:c0 + P] = bb_im[g].astype(BF16)
        wcre_ref[n, c0:c0 + P, r0:r0 + H] = c_re[g].astype(BF16)
        wcim_ref[n, c0:c0 + P, r0:r0 + H] = (-c_im[g]).astype(BF16)


def _ssm_prep(a_re, a_im, log_dt, b_re_t, b_im_t, c_re_t, c_im_t, d):
    G = N_SSM_GROUPS
    tile = jax.ShapeDtypeStruct((N_STATE_TILES, MXU_TILE, MXU_TILE), BF16)
    return pl.pallas_call(
        _ssm_prep_body,
        out_shape=(jax.ShapeDtypeStruct((1, N_STATE), F32), jax.ShapeDtypeStruct((1, N_STATE), F32),
                   jax.ShapeDtypeStruct((1, SSM_WIDTH), F32), tile, tile, tile, tile),
        name="ssm_prep",
    )(a_re, a_im, log_dt.reshape(G, 1), b_re_t, b_im_t, c_re_t, c_im_t, d)


def _adaln_body(cp_ref, cs_ref, w_ref, b_ref, op_ref, os_ref):
    n_p = cp_ref.shape[0]
    s = jax.nn.silu(jnp.concatenate([cp_ref[...], cs_ref[...]], axis=0)).astype(BF16)
    mod = jnp.dot(s, w_ref[...].astype(BF16), preferred_element_type=F32) + b_ref[...]
    op_ref[...] = mod[:n_p]
    os_ref[...] = mod[n_p:]


def _adaln(c_p, c_s, w_ada, b_ada):
    n_p, n_s = c_p.shape[0], c_s.shape[0]
    tn = D_MODEL
    return pl.pallas_call(
        _adaln_body,
        grid=(N_MOD * D_MODEL // tn,),
        in_specs=[pl.BlockSpec((n_p, D_MODEL), lambda j: (0, 0)),
                  pl.BlockSpec((n_s, D_MODEL), lambda j: (0, 0)),
                  pl.BlockSpec((D_MODEL, tn), lambda j: (0, j)),
                  pl.BlockSpec((1, tn), lambda j: (0, j))],
        out_specs=(pl.BlockSpec((n_p, tn), lambda j: (0, j)), pl.BlockSpec((n_s, tn), lambda j: (0, j))),
        out_shape=(jax.ShapeDtypeStruct((n_p, N_MOD * D_MODEL), F32),
                   jax.ShapeDtypeStruct((n_s, N_MOD * D_MODEL), F32)),
        compiler_params=_cparams(("arbitrary",)),
        name="adaln",
    )(c_p, c_s, w_ada, b_ada.reshape(1, -1))


def _mixer_body(*refs, S, Tt, Ts, start_pos, seq_major, has_state):
    refs = list(refs)
    x_ref, sh_ref, sc_ref, g_ref, win_ref = refs[:5]
    k = 5
    if has_state:
        buf0_ref, hre0_ref, him0_ref = refs[k:k + 3]
        k += 3
    (poolw_ref, pscale_ref, lbre_ref, lbim_ref, wbre_ref, wbim_ref, wcre_ref, wcim_ref, d_ref, gluw_ref,
     glub_ref) = refs[k:k + 11]
    k += 11
    mix_ref, newbuf_ref, hre_out_ref, him_out_ref = refs[k:k + 4]
    z_scr, sre_scr, sim_scr, hre_scr, him_scr, winb_scr, glub_scr = refs[k + 4:]

    i = pl.program_id(0)
    R = Tt * S
    HR = HIST * S
    D = D_MODEL

    @pl.when(i == 0)
    def _init():
        z_scr[0:S, :] = jnp.zeros((S, POOL_WIDTH), F32)
        if has_state:
            for j in range(POOL_BUF):
                z_scr[(j + 1) * S:(j + 2) * S, :] = buf0_ref[:, j, :]
            hre_scr[...] = hre0_ref[...]
            him_scr[...] = him0_ref[...]
        else:
            z_scr[S:HR, :] = jnp.zeros((HR - S, POOL_WIDTH), F32)
            hre_scr[...] = jnp.zeros((S, N_STATE), F32)
            him_scr[...] = jnp.zeros((S, N_STATE), F32)
        winb_scr[...] = win_ref[...].astype(BF16)
        glub_scr[...] = gluw_ref[...].astype(BF16)

    g = g_ref[...]
    Rs = Ts * S
    CB = 512
    n_tiles = S // SUBLANES
    n_ct = SSM_WIDTH // MXU_TILE
    k_per = N_STATE_TILES // n_ct
    for sub in range(Tt // Ts):
        t0 = sub * Ts
        r_lo = sub * Rs

        if seq_major:
            x3 = x_ref[:, t0:t0 + Ts, :]
            h3 = _rms(x3, g) * (1.0 + sc_ref[...][:, None, :]) + sh_ref[...][:, None, :]
            u_nm = jnp.dot(h3.reshape(Rs, D).astype(BF16), winb_scr[...], preferred_element_type=F32)
            u = jnp.swapaxes(u_nm.reshape(S, Ts, D), 0, 1).reshape(Rs, D)
        else:
            x_tm = jnp.concatenate([x_ref[:, t, :] for t in range(t0, t0 + Ts)], axis=0)
            h3 = _rms(x_tm, g).reshape(Ts, S, D) * (1.0 + sc_ref[...]) + sh_ref[...]
            u = jnp.dot(h3.reshape(Rs, D).astype(BF16), winb_scr[...], preferred_element_type=F32)

        up = u[:, 0:POOL_WIDTH]
        us = u[:, POOL_WIDTH:D]
        z_scr[HR + r_lo:HR + r_lo + Rs, :] = up

        row = lax.broadcasted_iota(jnp.int32, (Rs, 1), 0)
        pos = start_pos + i * Tt + t0 + lax.shift_right_logical(row, S.bit_length() - 1)
        outs = []
        for kk, w in enumerate(POOL_WINDOWS):
            lo, hi = kk * POOL_GROUP, (kk + 1) * POOL_GROUP
            cur = z_scr[r_lo:r_lo + HR + Rs, lo:hi]
            step = 1
            while step < w:
                cur = cur[step * S:, :] + cur[:cur.shape[0] - step * S, :]
                step *= 2
            s = cur[cur.shape[0] - Rs:, :]
            cnt = jnp.minimum(w, pos + 1).astype(F32)
            pooled = s / cnt - up[:, lo:hi]
            mixed = jnp.dot(pooled.astype(BF16), poolw_ref[kk].astype(BF16), preferred_element_type=F32)
            outs.append(mixed * pscale_ref[:, lo:hi])

        usb = us.astype(BF16)
        for n in range(N_STATE_TILES):
            kb = (n * MXU_TILE // SSM_STATE * SSM_GROUP) // MXU_TILE
            lhs = usb[:, kb * MXU_TILE:(kb + 1) * MXU_TILE]
            cols = slice(n * MXU_TILE, (n + 1) * MXU_TILE)
            sre_scr[r_lo:r_lo + Rs, cols] = jnp.dot(lhs, wbre_ref[n], preferred_element_type=F32)
            sim_scr[r_lo:r_lo + Rs, cols] = jnp.dot(lhs, wbim_ref[n], preferred_element_type=F32)

        for cb in range(N_STATE // CB):
            c0 = cb * CB
            lr = jnp.broadcast_to(lbre_ref[:, c0:c0 + CB], (SUBLANES, CB))
            li = jnp.broadcast_to(lbim_ref[:, c0:c0 + CB], (SUBLANES, CB))

            def scan_tile(s0, c0=c0, lr=lr, li=li, r_lo=r_lo):
                hr = hre_scr[pl.ds(s0, SUBLANES), c0:c0 + CB]
                hi_ = him_scr[pl.ds(s0, SUBLANES), c0:c0 + CB]
                for t in range(Ts):
                    r0 = r_lo + t * S + s0
                    br = sre_scr[pl.ds(r0, SUBLANES), c0:c0 + CB]
                    bi = sim_scr[pl.ds(r0, SUBLANES), c0:c0 + CB]
                    hr, hi_ = lr * hr - li * hi_ + br, lr * hi_ + li * hr + bi
                    sre_scr[pl.ds(r0, SUBLANES), c0:c0 + CB] = hr
                    sim_scr[pl.ds(r0, SUBLANES), c0:c0 + CB] = hi_
                hre_scr[pl.ds(s0, SUBLANES), c0:c0 + CB] = hr
                him_scr[pl.ds(s0, SUBLANES), c0:c0 + CB] = hi_

            if n_tiles == 1:
                scan_tile(0)
            else:
                def tile_body(j, carry, scan_tile=scan_tile):
                    scan_tile(pl.multiple_of(j * SUBLANES, SUBLANES))
                    return carry

                lax.fori_loop(0, n_tiles, tile_body, 0)

        ys = []
        for m in range(n_ct):
            acc = d_ref[:, m * MXU_TILE:(m + 1) * MXU_TILE] * us[:, m * MXU_TILE:(m + 1) * MXU_TILE]
            for kk in range(k_per):
                kt = m * k_per + kk
                cols = slice(kt * MXU_TILE, (kt + 1) * MXU_TILE)
                acc = acc + jnp.dot(sre_scr[r_lo:r_lo + Rs, cols].astype(BF16), wcre_ref[kt],
                                    preferred_element_type=F32)
                acc = acc + jnp.dot(sim_scr[r_lo:r_lo + Rs, cols].astype(BF16), wcim_ref[kt],
                                    preferred_element_type=F32)
            ys.append(acc)
        y = jnp.concatenate(ys, axis=-1)
        gl = jax.nn.gelu(y)
        gate = jax.nn.sigmoid(jnp.dot(gl.astype(BF16), glub_scr[...], preferred_element_type=F32) + glub_ref[...])
        outs.append(gl * gate)

        mix_tm = jnp.concatenate(outs, axis=-1)
        if seq_major:
            mix_ref[:, t0:t0 + Ts, :] = jnp.swapaxes(mix_tm.reshape(Ts, S, D), 0, 1).astype(mix_ref.dtype)
        else:
            mix_ref[r_lo:r_lo + Rs, :] = mix_tm.astype(mix_ref.dtype)

    for j in range(POOL_BUF):
        r0 = (Tt + 1 + j) * S
        newbuf_ref[:, j, :] = z_scr[r0:r0 + S, :]
    hist = z_scr[R:R + HR, :]
    z_scr[0:HR, :] = hist
    hre_out_ref[...] = hre_scr[...]
    him_out_ref[...] = him_scr[...]


def _mixer(x, mod, g1, w_in, state, wts, *, Tt, Ts, start_pos, seq_major):
    S, L, D = x.shape
    R = Tt * S
    has_state = state is not None
    consts = [g1, w_in] + (list(state) if has_state else []) + list(wts)
    x_spec = pl.BlockSpec((S, Tt, D), lambda i: (0, i, 0))
    mod_specs = [pl.BlockSpec((S, D), lambda i: (0, 0)), pl.BlockSpec((S, D), lambda i: (0, 1))]
    if seq_major:
        mix_spec = pl.BlockSpec((S, Tt, D), lambda i: (0, i, 0))
        mix_shape = jax.ShapeDtypeStruct((S, L, D), BF16)
    else:
        assert Tt == L
        mix_spec = pl.BlockSpec((R, D), lambda i: (0, 0))
        mix_shape = jax.ShapeDtypeStruct((L * S, D), BF16)
    body = functools.partial(_mixer_body, S=S, Tt=Tt, Ts=Ts, start_pos=start_pos, seq_major=seq_major,
                             has_state=has_state)
    return pl.pallas_call(
        body,
        grid=(L // Tt,),
        in_specs=[x_spec] + mod_specs + [_const_spec(a.shape) for a in consts],
        out_specs=(mix_spec,
                   pl.BlockSpec((S, POOL_BUF, POOL_WIDTH), lambda i: (0, 0, 0)),
                   pl.BlockSpec((S, N_STATE), lambda i: (0, 0)),
                   pl.BlockSpec((S, N_STATE), lambda i: (0, 0))),
        out_shape=(mix_shape,
                   jax.ShapeDtypeStruct((S, POOL_BUF, POOL_WIDTH), F32),
                   jax.ShapeDtypeStruct((S, N_STATE), F32),
                   jax.ShapeDtypeStruct((S, N_STATE), F32)),
        scratch_shapes=[pltpu.VMEM(((HIST + Tt) * S, POOL_WIDTH), F32),
                        pltpu.VMEM((R, N_STATE), F32),
                        pltpu.VMEM((R, N_STATE), F32),
                        pltpu.VMEM((S, N_STATE), F32),
                        pltpu.VMEM((S, N_STATE), F32),
                        pltpu.VMEM((D, D), BF16),
                        pltpu.VMEM((SSM_WIDTH, SSM_WIDTH), BF16)],
        compiler_params=_cparams(("arbitrary",)),
        name="mixer_S%d" % S,
    )(x, mod, mod, *consts)


def _split_bf16(v):
    hi = v.astype(BF16)
    lo = (v - hi.astype(F32)).astype(BF16)
    return hi, lo


def _route(h2, rw_ref, rb_ref):
    R = h2.shape[0]
    h2_hi, h2_lo = _split_bf16(h2)
    rw_hi, rw_lo = _split_bf16(rw_ref[...])
    logits = (jnp.dot(h2_hi, rw_hi, preferred_element_type=F32)
              + jnp.dot(h2_lo, rw_hi, preferred_element_type=F32)
              + jnp.dot(h2_hi, rw_lo, preferred_element_type=F32)) + rb_ref[...]
    lane = lax.broadcasted_iota(jnp.int32, (R, ROUTER_LANES), 1).astype(F32)
    ninf = jnp.float32(-jnp.inf)
    none = jnp.float32(ROUTER_LANES)
    is_g = lane < N_EXPERT_GROUPS
    l1 = jnp.where(is_g, logits, ninf)
    m1 = jnp.max(l1, axis=-1, keepdims=True)
    gidx = jnp.min(jnp.where(l1 == m1, lane, none), axis=-1, keepdims=True)
    p_top = 1.0 / jnp.sum(jnp.where(is_g, jnp.exp(logits - m1), 0.0), axis=-1, keepdims=True)
    e_lo = N_EXPERT_GROUPS + gidx * EXPERTS_PER_GROUP
    sel = (lane >= e_lo) & (lane < e_lo + EXPERTS_PER_GROUP)
    l2 = jnp.where(sel, logits, ninf)
    va = jnp.max(l2, axis=-1, keepdims=True)
    ia = jnp.min(jnp.where(l2 == va, lane, none), axis=-1, keepdims=True)
    l2b = jnp.where(lane == ia, ninf, l2)
    vb = jnp.max(l2b, axis=-1, keepdims=True)
    ib = jnp.min(jnp.where(l2b == vb, lane, none), axis=-1, keepdims=True)
    eb = jnp.exp(vb - va)
    den = 1.0 + eb
    gates = jnp.where(lane == ia, (1.0 / den) * p_top, 0.0) + jnp.where(lane == ib, (eb / den) * p_top, 0.0)
    return h2_hi, gates


def _stage3_rows(x, mix, g1, sh2, sc2, g2, n2g_ref, fng_ref, wout_ref, rw_ref, rb_ref, wg_ref, wu_ref, wd_ref):
    R = x.shape[0]
    mixo = jnp.dot(mix, wout_ref[...], preferred_element_type=F32)
    x1 = x + g1 * mixo
    h2 = _rms(x1, n2g_ref[...]) * (1.0 + sc2) + sh2
    h2_hi, gates = _route(h2, rw_ref, rb_ref)

    acc = jnp.zeros((R, D_MODEL), F32)
    for e in range(N_EXPERTS):
        a = jnp.dot(h2_hi, wg_ref[e], preferred_element_type=F32)
        b = jnp.dot(h2_hi, wu_ref[e], preferred_element_type=F32)
        ge = gates[:, N_EXPERT_GROUPS + e:N_EXPERT_GROUPS + e + 1]
        hid = jax.nn.silu(a) * b * ge
        acc = acc + jnp.dot(hid.astype(BF16), wd_ref[e], preferred_element_type=F32)
    x2 = x1 + g2 * acc
    return _rms(x2, fng_ref[...])


def _stage3_prompt_body(x_ref, mix_ref, g1_ref, sh2_ref, sc2_ref, g2_ref, *rest):
    wrefs, y_ref = rest[:-1], rest[-1]
    n = pl.program_id(0)
    mods = [r[pl.ds(n, 1), :] for r in (g1_ref, sh2_ref, sc2_ref, g2_ref)]
    y_ref[0] = _stage3_rows(x_ref[0], mix_ref[0], *mods, *wrefs)


def _stage3_prompt(x, mix, mod, wts, tc):
    nb, L, D = x.shape
    consts = list(wts)

    def mspec(k):
        return pl.BlockSpec((nb, D), lambda n, c, k=k: (0, k))

    return pl.pallas_call(
        _stage3_prompt_body,
        grid=(nb, L // tc),
        in_specs=[pl.BlockSpec((1, tc, D), lambda n, c: (n, c, 0)),
                  pl.BlockSpec((1, tc, D), lambda n, c: (n, c, 0)),
                  mspec(2), mspec(3), mspec(4), mspec(5)] + [_const_spec(a.shape) for a in consts],
        out_specs=pl.BlockSpec((1, tc, D), lambda n, c: (n, c, 0)),
        out_shape=jax.ShapeDtypeStruct((nb, L, D), F32),
        compiler_params=_cparams(("arbitrary", "arbitrary")),
        name="stage3_prompt",
    )(x, mix, mod, mod, mod, mod, *consts)


SAMPLE_EXPERTS_PER_STEP = 4


def _stage3_sample_body(x_ref, mix_ref, g1_ref, sh2_ref, sc2_ref, g2_ref, n2g_ref, fng_ref, wout_ref, rw_ref, rb_ref,
                        wg_ref, wu_ref, wd_ref, y_ref, x1_scr, h2b_scr, gates_scr, acc_scr, *, S, L):
    e = pl.program_id(0)
    R = S * L

    def rows(r):
        return jnp.concatenate([r[...]] * L, axis=0)

    @pl.when(e == 0)
    def _prologue():
        x_tm = jnp.concatenate([x_ref[:, t, :] for t in range(L)], axis=0)
        mixo = jnp.dot(mix_ref[...], wout_ref[...], preferred_element_type=F32)
        x1 = x_tm + rows(g1_ref) * mixo
        h2 = _rms(x1, n2g_ref[...]) * (1.0 + rows(sc2_ref)) + rows(sh2_ref)
        h2_hi, gates = _route(h2, rw_ref, rb_ref)
        x1_scr[...] = x1
        h2b_scr[...] = h2_hi
        gates_scr[...] = gates
        acc_scr[...] = jnp.zeros((R, D_MODEL), F32)

    lane = lax.broadcasted_iota(jnp.int32, (R, ROUTER_LANES), 1)
    gates = gates_scr[...]
    h2b = h2b_scr[...]
    acc = acc_scr[...]
    for j in range(SAMPLE_EXPERTS_PER_STEP):
        ge = jnp.sum(jnp.where(lane == N_EXPERT_GROUPS + e * SAMPLE_EXPERTS_PER_STEP + j, gates, 0.0),
                     axis=-1, keepdims=True)
        a = jnp.dot(h2b, wg_ref[j], preferred_element_type=F32)
        b = jnp.dot(h2b, wu_ref[j], preferred_element_type=F32)
        hid = jax.nn.silu(a) * b * ge
        acc = acc + jnp.dot(hid.astype(BF16), wd_ref[j], preferred_element_type=F32)
    acc_scr[...] = acc

    @pl.when(e == pl.num_programs(0) - 1)
    def _epilogue():
        y = _rms(x1_scr[...] + rows(g2_ref) * acc_scr[...], fng_ref[...])
        for t in range(L):
            y_ref[:, t, :] = y[t * S:(t + 1) * S, :]


def _stage3_sample(x, mix_tm, mod, n2g, fng, w_out, rw, rb, w_gate, w_up, w_down):
    ns, L, D = x.shape
    E, _, F = w_gate.shape
    R = ns * L
    consts = [n2g, fng, w_out, rw, rb]

    def mspec(k):
        return pl.BlockSpec((ns, D), lambda e, k=k: (0, k), pipeline_mode=pl.Buffered(1))

    return pl.pallas_call(
        functools.partial(_stage3_sample_body, S=ns, L=L),
        grid=(E // SAMPLE_EXPERTS_PER_STEP,),
        in_specs=[_const_spec((ns, L, D)), _const_spec((R, D)),
                  mspec(2), mspec(3), mspec(4), mspec(5)] + [_const_spec(a.shape) for a in consts]
                 + [pl.BlockSpec((SAMPLE_EXPERTS_PER_STEP, D, F), lambda e: (e, 0, 0)),
                    pl.BlockSpec((SAMPLE_EXPERTS_PER_STEP, D, F), lambda e: (e, 0, 0)),
                    pl.BlockSpec((SAMPLE_EXPERTS_PER_STEP, F, D), lambda e: (e, 0, 0))],
        out_specs=pl.BlockSpec((ns, L, D), lambda e: (0, 0, 0)),
        out_shape=jax.ShapeDtypeStruct((ns, L, D), F32),
        scratch_shapes=[pltpu.VMEM((R, D), F32), pltpu.VMEM((R, D), BF16),
                        pltpu.VMEM((R, ROUTER_LANES), F32), pltpu.VMEM((R, D), F32)],
        compiler_params=_cparams(("arbitrary",)),
        name="stage3_sample",
    )(x, mix_tm, mod, mod, mod, mod, *consts, w_gate, w_up, w_down)


def kernel(x_prompt, x_sample, c_prompt, c_sample, state_pool, state_ssm_re, state_ssm_im, w_ada, b_ada, norm1_g, w_in, pool_w, pool_scale, ssm_a_re, ssm_a_im, ssm_log_dt, ssm_b_re, ssm_b_im, ssm_c_re, ssm_c_im, ssm_d, glu_w, glu_b, w_out, norm2_g, router_w1, router_b1, router_w2, router_b2, exp_w_gate, exp_w_up, exp_w_down, final_norm_g):
    depth = w_ada.shape[0]
    assert depth == 1
    l = 0
    nb, L, D = x_prompt.shape
    ns, Ls, _ = x_sample.shape

    lb_re, lb_im, d_flat, wbre, wbim, wcre, wcim = _ssm_prep(
        ssm_a_re[l], ssm_a_im[l], ssm_log_dt[l],
        jnp.transpose(ssm_b_re[l], (0, 2, 1)), jnp.transpose(ssm_b_im[l], (0, 2, 1)),
        jnp.transpose(ssm_c_re[l], (0, 2, 1)), jnp.transpose(ssm_c_im[l], (0, 2, 1)), ssm_d[l])
    mix_wts = (pool_w[l], pool_scale[l].reshape(1, -1), lb_re, lb_im, wbre, wbim, wcre, wcim, d_flat,
               glu_w[l], glu_b[l].reshape(1, -1))

    rw = jnp.concatenate([router_w1[l], jnp.transpose(router_w2[l], (1, 0, 2)).reshape(D, N_EXPERTS)], axis=1)
    rw = jnp.pad(rw, ((0, 0), (0, ROUTER_LANES - rw.shape[1])))
    rb = jnp.concatenate([router_b1[l], router_b2[l].reshape(-1)])
    rb = jnp.pad(rb, (0, ROUTER_LANES - rb.shape[0])).reshape(1, -1)
    s3_wts = (norm2_g[l].reshape(1, -1), final_norm_g.reshape(1, -1), w_out[l].astype(BF16), rw, rb,
              exp_w_gate[l].astype(BF16), exp_w_up[l].astype(BF16), exp_w_down[l].astype(BF16))

    mod_p, mod_s = _adaln(c_prompt, c_sample, w_ada[l], b_ada[l])
    g1 = norm1_g[l].reshape(1, -1)

    mix_p, pool_p, hre_p, him_p = _mixer(x_prompt, mod_p, g1, w_in[l], None, mix_wts,
                                         Tt=128, Ts=32, start_pos=0, seq_major=True)
    y_p = _stage3_prompt(x_prompt, mix_p, mod_p, s3_wts, 512)

    state = (state_pool[l], state_ssm_re[l].reshape(ns, N_STATE), state_ssm_im[l].reshape(ns, N_STATE))
    mix_s, pool_s, hre_s, him_s = _mixer(x_sample, mod_s, g1, w_in[l], state, mix_wts,
                                         Tt=Ls, Ts=Ls // 2, start_pos=PAST_LEN, seq_major=False)
    y_s = _stage3_sample(x_sample, mix_s, mod_s, *s3_wts)

    def st(a, n):
        return a.reshape(1, n, N_SSM_GROUPS, SSM_STATE)

    return (y_p, y_s, pool_p[None], pool_s[None], st(hre_p, nb), st(him_p, nb), st(hre_s, ns), st(him_s, ns))
```
